```python
import math
import jax
import jax.numpy as jnp
from jax import lax
import numpy as np

D_MODEL = 1024
BATCH = 2
SEQ = 8192
DEPTH = 4
DEC_BATCH = 32
DEC_SEQ = 64
PAST_LEN = 2048

CHUNK = 64
HEAD_DIM = 64
D_MIX = D_MODEL
RMS_EPS = 1e-6
NEG_INF = -1e30
N_EVEN = (DEPTH + 1) // 2
N_ODD = DEPTH // 2

H_A = (D_MIX // 2) // HEAD_DIM
KVH_A = 2
G_A = H_A // KVH_A
WINDOW = 128
A_LEFT = WINDOW // CHUNK
A_KEEP = min(WINDOW, PAST_LEN)
T5_BUCKETS = 32
T5_MAX_DIST = 128

C_B = D_MIX // 2
H_B = C_B // HEAD_DIM
DECAY_LORA = 64
ICLR_LORA = 64
VRES_LORA = 32
GATE_LORA = 128
LNX_EPS = 64e-5
B_SPLITS = (C_B, DECAY_LORA, C_B, C_B, ICLR_LORA, GATE_LORA)
B_PROJ = sum(B_SPLITS)

H_C = (D_MIX // 2) // HEAD_DIM
C_LEFT = 8
C_KEEP = min(C_LEFT * CHUNK, PAST_LEN)
REL_CLIP = 128

C_D = D_MIX // 2
CONV_W = 31
LN_EPS = 1e-5

D_FF = -(-8 * D_MODEL // (3 * 256)) * 256

E_SPLITS = (H_A * HEAD_DIM, KVH_A * HEAD_DIM, KVH_A * HEAD_DIM, B_PROJ)
O_SPLITS = (H_C * HEAD_DIM, H_C * HEAD_DIM, H_C * HEAD_DIM, 2 * C_D)
E_IN = sum(E_SPLITS)
O_IN = sum(O_SPLITS)

kernel_name = 'hybrid_streaming_encoder_step'


def split_cols(p, sizes):
    return jnp.split(p, np.cumsum(sizes)[:-1].tolist(), axis=-1)


def rmsnorm(x, g):
    xf = x.astype(jnp.float32)
    y = xf * lax.rsqrt(jnp.mean(xf * xf, axis=-1, keepdims=True) + RMS_EPS)
    return (y * g.astype(jnp.float32)).astype(x.dtype)


def t5_bucket(rel):
    nb = T5_BUCKETS // 2
    exact = nb // 2
    n = np.abs(rel)
    nf = np.maximum(n, exact).astype(np.float32)
    large = exact + (np.log(nf / exact) / math.log(T5_MAX_DIST / exact) * (nb - exact)).astype(np.int32)
    return np.where(rel > 0, nb, 0) + np.where(n < exact, n, np.minimum(large, nb - 1))


def t5_bias(table, rel):
    b = table[jnp.asarray(t5_bucket(rel))]
    return jnp.moveaxis(b, -1, 0).reshape((KVH_A, G_A) + rel.shape).astype(jnp.float32)


def clipped_rel_bias(table, dist):
    idx = np.clip(dist, -REL_CLIP, REL_CLIP) + REL_CLIP
    return table[:, jnp.asarray(idx)][:, None].astype(jnp.float32)


def chunk_band(t, n_left):
    b, s = t.shape[:2]
    nc = s // CHUNK
    tp = jnp.pad(t, ((0, 0), (n_left * CHUNK, 0)) + ((0, 0),) * (t.ndim - 2))
    tc = tp.reshape((b, nc + n_left, CHUNK) + t.shape[2:])
    return jnp.concatenate([tc[:, o:o + nc] for o in range(n_left + 1)], axis=2)


def prompt_band_geometry(n_left, seq):
    nk = (n_left + 1) * CHUNK
    rel = np.arange(nk)[None, :] - n_left * CHUNK - np.arange(CHUNK)[:, None]
    kpos = (np.arange(seq // CHUNK)[:, None] - n_left) * CHUNK + np.arange(nk)[None, :]
    return rel, jnp.asarray(kpos >= 0)[:, None, None, None, :]


def sample_rel(keep, t):
    qpos = PAST_LEN + np.arange(t)
    kpos = np.concatenate([PAST_LEN - keep + np.arange(keep), PAST_LEN + np.arange(t)])
    return kpos[None, :] - qpos[:, None]


def attend(q, k, v, bias, mask, sink):
    s = jnp.einsum('...qhgd,...khd->...hgqk', q, k).astype(jnp.float32) * (HEAD_DIM ** -0.5) + bias
    if mask is not None:
        s = jnp.where(mask, s, NEG_INF)
    if sink is None:
        p = jax.nn.softmax(s, axis=-1)
    else:
        sk = sink.astype(jnp.float32)[..., None, None]
        m = jnp.maximum(jnp.max(s, axis=-1, keepdims=True), sk)
        e = jnp.exp(s - m)
        p = e / (jnp.sum(e, axis=-1, keepdims=True) + jnp.exp(sk - m))
    return jnp.einsum('...hgqk,...khd->...qhgd', p.astype(v.dtype), v)


def mixer_a(q, k, v, sink, t5_table, cache):
    b, t = q.shape[:2]
    q = q.reshape(b, t, KVH_A, G_A, HEAD_DIM)
    sk = sink.reshape(KVH_A, G_A)
    if cache is None:
        rel, valid = prompt_band_geometry(A_LEFT, t)
        qc = q.reshape(b, t // CHUNK, CHUNK, KVH_A, G_A, HEAD_DIM)
        o = attend(qc, chunk_band(k, A_LEFT), chunk_band(v, A_LEFT), t5_bias(t5_table, rel), valid, sk)
        new = (k[:, -WINDOW:], v[:, -WINDOW:])
    else:
        ck, cv = cache
        rel = sample_rel(ck.shape[1], t)
        kk = jnp.concatenate([ck.astype(k.dtype), k], axis=1)
        vv = jnp.concatenate([cv.astype(v.dtype), v], axis=1)
        o = attend(q, kk, vv, t5_bias(t5_table, rel), None, sk)
        new = (k, v)
    return o.reshape(b, t, H_A * HEAD_DIM), new


def mixer_c(q, k, v, rel_table, cache):
    b, t = q.shape[:2]
    q = q[:, :, :, None, :]
    if cache is None:
        rel, valid = prompt_band_geometry(C_LEFT, t)
        qc = q.reshape(b, t // CHUNK, CHUNK, H_C, 1, HEAD_DIM)
        o = attend(qc, chunk_band(k, C_LEFT), chunk_band(v, C_LEFT), clipped_rel_bias(rel_table, -rel), valid, None)
        new = (k[:, -C_LEFT * CHUNK:], v[:, -C_LEFT * CHUNK:])
    else:
        ck, cv = cache
        rel = sample_rel(ck.shape[1], t)
        kk = jnp.concatenate([ck.astype(k.dtype), k], axis=1)
        vv = jnp.concatenate([cv.astype(v.dtype), v], axis=1)
        o = attend(q, kk, vv, clipped_rel_bias(rel_table, -rel), None, None)
        new = (k, v)
    return o.reshape(b, t, H_C * HEAD_DIM), new


def rwkv7_scan(r, decay, k, v, a, bb, s0):
    def step(s, inp):
        r_t, w_t, k_t, v_t, a_t, b_t = inp
        sa = jnp.einsum('bhij,bhj->bhi', s, a_t)
        s = s * w_t[:, :, None, :] + sa[..., None] * b_t[:, :, None, :] + v_t[..., None] * k_t[:, :, None, :]
        return s, jnp.einsum('bhij,bhj->bhi', s, r_t)
    xs = tuple(jnp.moveaxis(z, 1, 0) for z in (r, decay, k, v, a, bb))
    s, ys = lax.scan(step, s0, xs)
    return jnp.moveaxis(ys, 0, 1), s


def mixer_b(pb, shift_prev, wkv0, v_first, mu, w0, w2, a0, a2, g2, kk_scale, k_a, r_k, lnx_w, lnx_b, vres):
    b, t, _ = pb.shape
    f32 = jnp.float32
    prev = jnp.concatenate([shift_prev[:, None].astype(pb.dtype), pb[:, :-1]], axis=1)
    xsh = pb + (prev - pb) * mu
    r, wd, k, v, ad, gd = split_cols(xsh, B_SPLITS)
    w = -jax.nn.softplus(-(w0 + jnp.tanh(wd) @ w2)) - 0.5
    a = jax.nn.sigmoid(a0 + ad @ a2)
    g = jax.nn.sigmoid(gd) @ g2
    if vres is None:
        v_first = v
    else:
        v0, v1, v2 = vres
        v = v + (v_first - v) * jax.nn.sigmoid(v0 + (v @ v1) @ v2)
    heads = lambda z: z.reshape(b, t, H_B, HEAD_DIM).astype(f32)
    kk = heads(k * kk_scale)
    kk = kk / jnp.maximum(jnp.sqrt(jnp.sum(kk * kk, axis=-1, keepdims=True)), 1e-12)
    kh = heads(k * (1 + (a - 1) * k_a))
    rh, vh, ah = heads(r), heads(v), heads(a)
    decay = jnp.exp(-jnp.exp(heads(w)))
    y, wkv = rwkv7_scan(rh, decay, kh, vh, -kk, kk * ah, wkv0.astype(f32))
    mean = jnp.mean(y, axis=-1, keepdims=True)
    var = jnp.mean(jnp.square(y - mean), axis=-1, keepdims=True)
    y = (y - mean) * lax.rsqrt(var + LNX_EPS) * lnx_w.reshape(H_B, HEAD_DIM).astype(f32) + lnx_b.reshape(H_B, HEAD_DIM).astype(f32)
    y = y + jnp.sum(rh * kh * r_k.astype(f32), axis=-1, keepdims=True) * vh
    out = y.reshape(b, t, C_B).astype(pb.dtype) * g
    return out, v_first, pb[:, -1], wkv


def mixer_d(pd, conv_prev, dw_w, dw_b, ln_g, ln_b):
    u = pd[..., :C_D] * jax.nn.sigmoid(pd[..., C_D:])
    b = u.shape[0]
    buf = jnp.zeros((b, CONV_W - 1, C_D), u.dtype) if conv_prev is None else conv_prev.astype(u.dtype)
    up = jnp.concatenate([buf, u], axis=1)
    z = lax.conv_general_dilated(up, dw_w[:, None, :].astype(u.dtype), window_strides=(1,), padding='VALID',
                                 dimension_numbers=('NWC', 'WIO', 'NWC'), feature_group_count=C_D) + dw_b
    zf = z.astype(jnp.float32)
    mean = jnp.mean(zf, axis=-1, keepdims=True)
    var = jnp.mean(jnp.square(zf - mean), axis=-1, keepdims=True)
    zn = (zf - mean) * lax.rsqrt(var + LN_EPS) * ln_g.astype(jnp.float32) + ln_b.astype(jnp.float32)
    return jax.nn.silu(zn).astype(u.dtype), up[:, -(CONV_W - 1):]


def swiglu(h, wg, wu, wd):
    return (jax.nn.silu(h @ wg) * (h @ wu)) @ wd


def setup_inputs(seed: int = 0) -> dict:
    key = jax.random.key(seed)
    keys = iter(jax.random.split(key, 48))

    def nrm(shape, scale=1.0, mean=0.0):
        return mean + scale * jax.random.normal(next(keys), shape, jnp.float32)

    def uni(shape, lo, hi):
        return jax.random.uniform(next(keys), shape, jnp.float32, lo, hi)

    NE, NO = N_EVEN, N_ODD
    return {
        'x_prompt': nrm((BATCH, SEQ, D_MODEL)),
        'x_sample': nrm((DEC_BATCH, DEC_SEQ, D_MODEL)),
        'cache_a_k': nrm((NE, DEC_BATCH, A_KEEP, KVH_A, HEAD_DIM)),
        'cache_a_v': nrm((NE, DEC_BATCH, A_KEEP, KVH_A, HEAD_DIM)),
        'state_b_wkv': nrm((NE, DEC_BATCH, H_B, HEAD_DIM, HEAD_DIM), 0.5),
        'state_b_shift': nrm((NE, DEC_BATCH, B_PROJ)),
        'cache_c_k': nrm((NO, DEC_BATCH, C_KEEP, H_C, HEAD_DIM)),
        'cache_c_v': nrm((NO, DEC_BATCH, C_KEEP, H_C, HEAD_DIM)),
        'state_d_conv': nrm((NO, DEC_BATCH, CONV_W - 1, C_D), 0.5),
        'norm_mix_g': nrm((DEPTH, D_MODEL), 0.05, 1.0),
        'norm_ffn_g': nrm((DEPTH, D_MODEL), 0.05, 1.0),
        'norm_final_g': nrm((D_MODEL,), 0.05, 1.0),
        't5_table': nrm((T5_BUCKETS, H_A), 0.5),
        'w_in_e': nrm((NE, D_MODEL, E_IN), D_MODEL ** -0.5),
        'w_out_e': nrm((NE, D_MIX, D_MODEL), D_MIX ** -0.5),
        'a_sink': nrm((NE, H_A), 0.5),
        'b_mu': uni((NE, B_PROJ), 0.0, 1.0),
        'b_w0': uni((NE, C_B), -6.0, -1.0),
        'b_w2': nrm((NE, DECAY_LORA, C_B), 0.1 * DECAY_LORA ** -0.5),
        'b_a0': nrm((NE, C_B), 0.1),
        'b_a2': nrm((NE, ICLR_LORA, C_B), ICLR_LORA ** -0.5),
        'b_g2': nrm((NE, GATE_LORA, C_B), GATE_LORA ** -0.5),
        'b_kk': nrm((NE, C_B), 0.05, 0.85),
        'b_ka': nrm((NE, C_B), 0.05, 1.0),
        'b_rk': nrm((NE, H_B, HEAD_DIM), 0.1),
        'b_lnx_w': nrm((NE, C_B), 0.05, 1.0),
        'b_lnx_b': nrm((NE, C_B), 0.02),
        'b_v0': nrm((NE - 1, C_B), 0.1),
        'b_v1': nrm((NE - 1, C_B, VRES_LORA), C_B ** -0.5),
        'b_v2': nrm((NE - 1, VRES_LORA, C_B), VRES_LORA ** -0.5),
        'w_in_o': nrm((NO, D_MODEL, O_IN), D_MODEL ** -0.5),
        'w_out_o': nrm((NO, D_MIX, D_MODEL), D_MIX ** -0.5),
        'c_rel_table': nrm((NO, H_C, 2 * REL_CLIP + 1), 0.5),
        'd_dw_w': nrm((NO, CONV_W, C_D), CONV_W ** -0.5),
        'd_dw_b': nrm((NO, C_D), 0.02),
        'd_ln_g': nrm((NO, C_D), 0.05, 1.0),
        'd_ln_b': nrm((NO, C_D), 0.02),
        'ffn_w_gate': nrm((DEPTH, D_MODEL, D_FF), D_MODEL ** -0.5),
        'ffn_w_up': nrm((DEPTH, D_MODEL, D_FF), D_MODEL ** -0.5),
        'ffn_w_down': nrm((DEPTH, D_FF, D_MODEL), D_FF ** -0.5),
    }


def reference(x_prompt, x_sample, cache_a_k, cache_a_v, state_b_wkv, state_b_shift,
              cache_c_k, cache_c_v, state_d_conv,
              norm_mix_g, norm_ffn_g, norm_final_g, t5_table,
              w_in_e, w_out_e, a_sink,
              b_mu, b_w0, b_w2, b_a0, b_a2, b_g2, b_kk, b_ka, b_rk, b_lnx_w, b_lnx_b,
              b_v0, b_v1, b_v2,
              w_in_o, w_out_o, c_rel_table, d_dw_w, d_dw_b, d_ln_g, d_ln_b,
              ffn_w_gate, ffn_w_up, ffn_w_down):
    xs = [x_prompt, x_sample]
    v_first = [None, None]
    st = [[[] for _ in range(7)] for _ in range(2)]
    for i in range(DEPTH):
        j = i // 2
        for gi in range(2):
            x = xs[gi]
            b, t = x.shape[:2]
            h = rmsnorm(x, norm_mix_g[i])
            if i % 2 == 0:
                q, k, v, pb = split_cols(h @ w_in_e[j], E_SPLITS)
                cache = None if gi == 0 else (cache_a_k[j], cache_a_v[j])
                oa, (nk, nv) = mixer_a(q.reshape(b, t, H_A, HEAD_DIM), k.reshape(b, t, KVH_A, HEAD_DIM),
                                       v.reshape(b, t, KVH_A, HEAD_DIM), a_sink[j], t5_table, cache)
                if gi == 0:
                    shift0 = jnp.zeros((b, B_PROJ), x.dtype)
                    wkv0 = jnp.zeros((b, H_B, HEAD_DIM, HEAD_DIM), jnp.float32)
                else:
                    shift0, wkv0 = state_b_shift[j], state_b_wkv[j]
                vres = None if j == 0 else (b_v0[j - 1], b_v1[j - 1], b_v2[j - 1])
                ob, v_first[gi], nshift, nwkv = mixer_b(pb, shift0, wkv0, v_first[gi], b_mu[j], b_w0[j], b_w2[j],
                                                        b_a0[j], b_a2[j], b_g2[j], b_kk[j], b_ka[j], b_rk[j],
                                                        b_lnx_w[j], b_lnx_b[j], vres)
                x = x + jnp.concatenate([oa, ob], axis=-1) @ w_out_e[j]
                for s_list, val in zip(st[gi][0:4], (nk, nv, nwkv, nshift)):
                    s_list.append(val)
            else:
                q, k, v, pd = split_cols(h @ w_in_o[j], O_SPLITS)
                cache = None if gi == 0 else (cache_c_k[j], cache_c_v[j])
                oc, (nk, nv) = mixer_c(q.reshape(b, t, H_C, HEAD_DIM), k.reshape(b, t, H_C, HEAD_DIM),
                                       v.reshape(b, t, H_C, HEAD_DIM), c_rel_table[j], cache)
                conv_prev = None if gi == 0 else state_d_conv[j]
                od, nconv = mixer_d(pd, conv_prev, d_dw_w[j], d_dw_b[j], d_ln_g[j], d_ln_b[j])
                x = x + jnp.concatenate([oc, od], axis=-1) @ w_out_o[j]
                for s_list, val in zip(st[gi][4:7], (nk, nv, nconv)):
                    s_list.append(val)
            h = rmsnorm(x, norm_ffn_g[i])
            xs[gi] = x + swiglu(h, ffn_w_gate[i], ffn_w_up[i], ffn_w_down[i])
    y_prompt = rmsnorm(xs[0], norm_final_g)
    y_sample = rmsnorm(xs[1], norm_final_g)
    (pak, pav, pbw, pbs, pck, pcv, pdc), (sak, sav, sbw, sbs, sck, scv, sdc) = [[jnp.stack(s) for s in g] for g in st]
    return (y_prompt, y_sample, pak, pav, pbw, pbs, pck, pcv, pdc, sak, sav, sbw, sbs, sck, scv, sdc)
```

```python
import functools
import math

import jax
import jax.numpy as jnp
import numpy as np
from jax import lax
from jax.experimental import pallas as pl
from jax.experimental.pallas import tpu as pltpu

F32 = jnp.float32
BF16 = jnp.bfloat16
HIGHEST = lax.Precision.HIGHEST

CHUNK = 64
HEAD_DIM = 64
RMS_EPS = 1e-6
NEG_INF = -1e30
LNX_EPS = 64e-5
LN_EPS = 1e-5
CONV_W = 31
T5_BUCKETS = 32
T5_MAX_DIST = 128
REL_CLIP = 128
A_LEFT = 2
C_LEFT = 8
KVH_A = 2
DECAY_LORA = 64
ICLR_LORA = 64
GATE_LORA = 128
LORA_W = DECAY_LORA + ICLR_LORA + GATE_LORA
CONV_HALO = 32
CONV_ROWS = 64

VMEM_LIMIT = 56 * 1024 * 1024


def _cparams(sem):
    return pltpu.CompilerParams(dimension_semantics=sem, vmem_limit_bytes=VMEM_LIMIT)


def _rms(x, g):
    return x * lax.rsqrt(jnp.mean(x * x, axis=-1, keepdims=True) + RMS_EPS) * g


def _dot(a, b):
    return jnp.dot(a, b, preferred_element_type=F32)


def _dot_hp(a, b):
    return jnp.dot(a, b, precision=HIGHEST, preferred_element_type=F32)


def _dot_nt_hp(a, b):
    return lax.dot_general(a, b, (((1,), (1,)), ((), ())), precision=HIGHEST, preferred_element_type=F32)


def _dot_tn_hp(a, b):
    return lax.dot_general(a, b, (((0,), (0,)), ((), ())), precision=HIGHEST, preferred_element_type=F32)


def _proj_in_kernel(x_ref, g_ref, w_ref, o_ref):
    h = _rms(x_ref[...], g_ref[...]).astype(BF16)
    o_ref[...] = _dot(h, w_ref[...])


def proj_in(x2d, g, w_bf16, tm):
    m, d = x2d.shape
    n = w_bf16.shape[1]
    return pl.pallas_call(
        _proj_in_kernel,
        grid=(m // tm,),
        in_specs=[pl.BlockSpec((tm, d), lambda i: (i, 0)),
                  pl.BlockSpec((1, d), lambda i: (0, 0)),
                  pl.BlockSpec((d, n), lambda i: (0, 0))],
        out_specs=pl.BlockSpec((tm, n), lambda i: (i, 0)),
        out_shape=jax.ShapeDtypeStruct((m, n), F32),
        compiler_params=_cparams(("parallel",)),
        name="proj_in",
    )(x2d, g.reshape(1, d), w_bf16)


def _post_kernel(x_ref, o1_ref, o2_ref, wo_ref, g_ref, wg_ref, wu_ref, wd_ref, gf_ref, out_ref,
                 xn_s, h_s, acc_s, *, final_norm):
    f = pl.program_id(1)
    half = o1_ref.shape[-1]

    @pl.when(f == 0)
    def _():
        xn = (x_ref[...] + _dot(o1_ref[...].astype(BF16), wo_ref[0:half, :])
              + _dot(o2_ref[...].astype(BF16), wo_ref[half:2 * half, :]))
        xn_s[...] = xn
        h_s[...] = _rms(xn, g_ref[...]).astype(BF16)
        acc_s[...] = jnp.zeros_like(acc_s)

    h = h_s[...]
    gate = _dot(h, wg_ref[...])
    up = _dot(h, wu_ref[...])
    act = gate * jax.nn.sigmoid(gate) * up
    acc_s[...] += _dot(act.astype(BF16), wd_ref[...])

    @pl.when(f == pl.num_programs(1) - 1)
    def _():
        y = xn_s[...] + acc_s[...]
        if final_norm:
            y = _rms(y, gf_ref[...])
        out_ref[...] = y


def post(x2d, o1, o2, wo, g, wg, wu, wd, gf, *, tm, tf, final_norm):
    m, d = x2d.shape
    half = o1.shape[-1]
    dff = wg.shape[1]
    return pl.pallas_call(
        functools.partial(_post_kernel, final_norm=final_norm),
        grid=(m // tm, dff // tf),
        in_specs=[pl.BlockSpec((tm, d), lambda i, f: (i, 0)),
                  pl.BlockSpec((tm, half), lambda i, f: (i, 0)),
                  pl.BlockSpec((tm, half), lambda i, f: (i, 0)),
                  pl.BlockSpec((2 * half, d), lambda i, f: (0, 0)),
                  pl.BlockSpec((1, d), lambda i, f: (0, 0)),
                  pl.BlockSpec((d, tf), lambda i, f: (0, f)),
                  pl.BlockSpec((d, tf), lambda i, f: (0, f)),
                  pl.BlockSpec((tf, d), lambda i, f: (f, 0)),
                  pl.BlockSpec((1, d), lambda i, f: (0, 0))],
        out_specs=pl.BlockSpec((tm, d), lambda i, f: (i, 0)),
        out_shape=jax.ShapeDtypeStruct((m, d), F32),
        scratch_shapes=[pltpu.VMEM((tm, d), F32), pltpu.VMEM((tm, d), BF16), pltpu.VMEM((tm, d), F32)],
        compiler_params=_cparams(("parallel", "arbitrary")),
        name="post",
    )(x2d, o1, o2, wo, g.reshape(1, d), wg, wu, wd, gf.reshape(1, d))


def _attn_kernel(*refs, n_blocks, n_heads, group, has_sink, masked):
    q_ref = refs[0]
    k_refs = refs[1:1 + n_blocks]
    v_refs = refs[1 + n_blocks:1 + 2 * n_blocks]
    bias_ref = refs[1 + 2 * n_blocks]
    sink_ref = refs[2 + 2 * n_blocks] if has_sink else None
    o_ref = refs[-1]
    nk = n_blocks * CHUNK

    q = (q_ref[0] * (HEAD_DIM ** -0.5)).astype(BF16)
    kband = jnp.concatenate([r[0] for r in k_refs], axis=0).astype(BF16)
    vband = jnp.concatenate([r[0] for r in v_refs], axis=0).astype(BF16)
    if masked:
        col = lax.broadcasted_iota(jnp.int32, (CHUNK, nk), 1)
        valid = col >= (n_blocks - 1 - pl.program_id(1)) * CHUNK
    for h in range(n_heads):
        kh = h // group
        qh = q[:, h * HEAD_DIM:(h + 1) * HEAD_DIM]
        s = lax.dot_general(qh, kband[:, kh * HEAD_DIM:(kh + 1) * HEAD_DIM], (((1,), (1,)), ((), ())),
                            preferred_element_type=F32) + bias_ref[h]
        if masked:
            s = jnp.where(valid, s, NEG_INF)
        m = jnp.max(s, axis=-1, keepdims=True)
        if has_sink:
            sk = sink_ref[h:h + 1, :]
            m = jnp.maximum(m, sk)
        e = jnp.exp(s - m)
        den = jnp.sum(e, axis=-1, keepdims=True)
        if has_sink:
            den = den + jnp.exp(sk - m)
        o = _dot(e.astype(BF16), vband[:, kh * HEAD_DIM:(kh + 1) * HEAD_DIM]) / den
        o_ref[0, :, h * HEAD_DIM:(h + 1) * HEAD_DIM] = o


def banded_attention(p, kv_srcs, bias, sink, *, n_left, n_heads, group, kcol, vcol, masked):
    b, t, _ = p.shape
    nc = t // CHUNK
    kvw = (n_heads // group) * HEAD_DIM
    qw = n_heads * HEAD_DIM
    n_blocks = n_left + 1
    operands = [p]
    in_specs = [pl.BlockSpec((1, CHUNK, qw), lambda bi, c: (bi, c, 0))]
    for col in (kcol, vcol):
        for j in range(n_blocks):
            if kv_srcs is None:
                operands.append(p)
                in_specs.append(pl.BlockSpec(
                    (1, CHUNK, kvw), lambda bi, c, j=j, col=col: (bi, jnp.maximum(c - n_left + j, 0), col)))
            elif j < n_left:
                operands.append(kv_srcs[0] if col == kcol else kv_srcs[1])
                in_specs.append(pl.BlockSpec((1, CHUNK, kvw), lambda bi, c, j=j: (bi, j, 0)))
            else:
                operands.append(p)
                in_specs.append(pl.BlockSpec((1, CHUNK, kvw), lambda bi, c, col=col: (bi, c, col)))
    operands.append(bias)
    in_specs.append(pl.BlockSpec(bias.shape, lambda bi, c: (0, 0, 0)))
    if sink is not None:
        operands.append(sink.reshape(n_heads, 1))
        in_specs.append(pl.BlockSpec((n_heads, 1), lambda bi, c: (0, 0)))
    return pl.pallas_call(
        functools.partial(_attn_kernel, n_blocks=n_blocks, n_heads=n_heads, group=group,
                          has_sink=sink is not None, masked=masked),
        grid=(b, nc),
        in_specs=in_specs,
        out_specs=pl.BlockSpec((1, CHUNK, qw), lambda bi, c: (bi, c, 0)),
        out_shape=jax.ShapeDtypeStruct((b, t, qw), F32),
        compiler_params=_cparams(("parallel", "parallel")),
        name="banded_attention",
    )(*operands)


def _conv_kernel(*refs, tq, from_state):
    if from_state:
        a_ref, gt_ref, st_in_ref, dw_ref, db_ref, lg_ref, lb_ref, o_ref, st_ref, buf = refs
    else:
        a_ref, gt_ref, pa_ref, pg_ref, dw_ref, db_ref, lg_ref, lb_ref, o_ref, st_ref, buf = refs
    keep = CONV_W - 1
    pad = CONV_HALO - keep
    if from_state:
        buf[0:pad, :] = jnp.zeros((pad, buf.shape[1]), F32)
        buf[pad:CONV_HALO, :] = st_in_ref[0]
    else:
        prev = pa_ref[0] * jax.nn.sigmoid(pg_ref[0])
        buf[0:CONV_HALO, :] = jnp.where(pl.program_id(1) > 0, prev, 0.0)
    buf[CONV_HALO:CONV_HALO + tq, :] = a_ref[0] * jax.nn.sigmoid(gt_ref[0])
    rs = min(tq, CONV_ROWS)
    for r0 in range(0, tq, rs):
        z = jnp.zeros((rs, buf.shape[1]), F32) + db_ref[...]
        for w in range(CONV_W):
            z = z + buf[r0 + pad + w:r0 + pad + w + rs, :] * dw_ref[w:w + 1, :]
        mean = jnp.mean(z, axis=-1, keepdims=True)
        zc = z - mean
        var = jnp.mean(zc * zc, axis=-1, keepdims=True)
        zn = zc * lax.rsqrt(var + LN_EPS) * lg_ref[...] + lb_ref[...]
        o_ref[0, r0:r0 + rs, :] = zn * jax.nn.sigmoid(zn)
    st_ref[0] = buf[tq + pad:tq + CONV_HALO, :]


def conv_module(p, state, dw_w, dw_b, ln_g, ln_b, *, acol, gcol, tq):
    b, t, _ = p.shape
    cd = dw_w.shape[1]
    keep = CONV_W - 1
    from_state = state is not None
    operands = [p, p]
    in_specs = [pl.BlockSpec((1, tq, cd), lambda bi, i: (bi, i, acol)),
                pl.BlockSpec((1, tq, cd), lambda bi, i: (bi, i, gcol))]
    if from_state:
        operands.append(state)
        in_specs.append(pl.BlockSpec((1, keep, cd), lambda bi, i: (bi, 0, 0)))
    else:
        r = tq // CONV_HALO
        for col in (acol, gcol):
            operands.append(p)
            in_specs.append(pl.BlockSpec((1, CONV_HALO, cd),
                                         lambda bi, i, col=col: (bi, jnp.maximum(i * r - 1, 0), col)))
    operands += [dw_w, dw_b.reshape(1, cd), ln_g.reshape(1, cd), ln_b.reshape(1, cd)]
    in_specs += [pl.BlockSpec((CONV_W, cd), lambda bi, i: (0, 0))] + [pl.BlockSpec((1, cd), lambda bi, i: (0, 0))] * 3
    return pl.pallas_call(
        functools.partial(_conv_kernel, tq=tq, from_state=from_state),
        grid=(b, t // tq),
        in_specs=in_specs,
        out_specs=[pl.BlockSpec((1, tq, cd), lambda bi, i: (bi, i, 0)),
                   pl.BlockSpec((1, keep, cd), lambda bi, i: (bi, 0, 0))],
        out_shape=[jax.ShapeDtypeStruct((b, t, cd), F32), jax.ShapeDtypeStruct((b, keep, cd), F32)],
        scratch_shapes=[pltpu.VMEM((CONV_HALO + tq, cd), F32)],
        compiler_params=_cparams(("parallel", "arbitrary")),
        name="conv_module",
    )(*operands)


PV_MU_R, PV_MU_K, PV_MU_V, PV_W0, PV_A0, PV_KK, PV_KA, PV_V0, PV_LNW, PV_LNB, PV_RK = range(11)
PV_ROWS = 16


def _shift_rows(x, first_row):
    rolled = pltpu.roll(x, 1, axis=0)
    row = lax.broadcasted_iota(jnp.int32, x.shape, 0)
    return jnp.where(row == 0, first_row, rolled)


def _rwkv_kernel(*refs, n_heads, has_vres):
    (xr_ref, xk_ref, xv_ref, xl_ref, shr_ref, shk_ref, shv_ref, shl_ref, s0_ref, pv_ref, mul_ref,
     w2_ref, a2_ref, g2_ref) = refs[:14]
    pos = 14
    if has_vres:
        v1_ref, v2_ref, vf_ref = refs[pos:pos + 3]
        pos += 3
    ob_ref, sout_ref = refs[pos:pos + 2]
    pos += 2
    if not has_vres:
        vf_out_ref = refs[pos]
        pos += 1
    s_s, lr_s, lk_s, lv_s, ll_s = refs[pos:pos + 5]
    L = CHUNK
    c = pl.program_id(1)

    @pl.when(c == 0)
    def _():
        s_s[...] = s0_ref[0]
        lr_s[0:1, :] = shr_ref[0]
        lk_s[0:1, :] = shk_ref[0]
        lv_s[0:1, :] = shv_ref[0]
        ll_s[0:1, :] = shl_ref[0]

    def token_shift(x_ref, last_s, mu):
        x = x_ref[0]
        prev = _shift_rows(x, last_s[0:1, :])
        last_s[0:1, :] = x[L - 1:L, :]
        return x + (prev - x) * mu

    pv = lambda i: pv_ref[i:i + 1, :]
    r = token_shift(xr_ref, lr_s, pv(PV_MU_R))
    k = token_shift(xk_ref, lk_s, pv(PV_MU_K))
    v = token_shift(xv_ref, lv_s, pv(PV_MU_V))
    lo = token_shift(xl_ref, ll_s, mul_ref[...])

    zw = pv(PV_W0) + _dot(jnp.tanh(lo).astype(BF16), w2_ref[...])
    w = -(jnp.maximum(-zw, 0.0) + jnp.log(1.0 + jnp.exp(-jnp.abs(zw)))) - 0.5
    d = -jnp.exp(w)
    iclr = jax.nn.sigmoid(pv(PV_A0) + _dot(lo.astype(BF16), a2_ref[...]))
    g = _dot(jax.nn.sigmoid(lo).astype(BF16), g2_ref[...])
    if has_vres:
        mix = jax.nn.sigmoid(pv(PV_V0) + _dot(_dot(v.astype(BF16), v1_ref[...]).astype(BF16), v2_ref[...]))
        v = v + (vf_ref[0] - v) * mix
    else:
        vf_out_ref[0] = v
    kks = k * pv(PV_KK)
    kh_all = k * (1.0 + (iclr - 1.0) * pv(PV_KA))

    row = lax.broadcasted_iota(jnp.int32, (L, L), 0)
    col = lax.broadcasted_iota(jnp.int32, (L, L), 1)
    incl = row >= col
    strict = row > col
    eye = (row == col).astype(F32)
    cs = _dot_hp(incl.astype(F32), d)
    cprev = cs - d
    c_last = cs[L - 1:L, :]
    c_mid = cs[L // 2:L // 2 + 1, :]
    e_start_prev = jnp.exp(cprev)
    e_start_incl = jnp.exp(cs)
    e_mid_prev = jnp.exp(cprev - c_mid)
    e_mid_incl = jnp.exp(cs - c_mid)
    e_from_mid = jnp.exp(c_mid - cs)
    e_to_end = jnp.exp(c_last - cs)
    e_chunk = jnp.exp(c_last)

    for h in range(n_heads):
        sl = slice(h * HEAD_DIM, (h + 1) * HEAD_DIM)
        kk = kks[:, sl]
        kk = kk / jnp.maximum(jnp.sqrt(jnp.sum(kk * kk, axis=-1, keepdims=True)), 1e-12)
        ic = iclr[:, sl]
        a_h = -kk
        b_h = kk * ic
        k_h = kh_all[:, sl]
        r_h = r[:, sl]
        v_h = v[:, sl]
        a0 = a_h * e_start_prev[:, sl]
        r0 = r_h * e_start_incl[:, sl]
        at = a_h * e_mid_prev[:, sl]
        rt = r_h * e_mid_incl[:, sl]
        bt = b_h * e_from_mid[:, sl]
        kt = k_h * e_from_mid[:, sl]
        bh = b_h * e_to_end[:, sl]
        kh = k_h * e_to_end[:, sl]
        n_ab = jnp.where(strict, _dot_nt_hp(at, bt), 0.0)
        n_ak = jnp.where(strict, _dot_nt_hp(at, kt), 0.0)
        n_rb = jnp.where(incl, _dot_nt_hp(rt, bt), 0.0)
        n_rk = jnp.where(incl, _dot_nt_hp(rt, kt), 0.0)
        x = eye + n_ab
        pw = n_ab
        for _ in range(int(math.log2(L)) - 1):
            pw = _dot_hp(pw, pw)
            x = x + _dot_hp(x, pw)
        w1 = _dot_hp(x, a0)
        w2 = _dot_hp(x, _dot_hp(n_ak, v_h))
        s_h = s_s[h]
        u = _dot_nt_hp(w1, s_h) + w2
        y = _dot_nt_hp(r0, s_h) + _dot_hp(n_rb, u) + _dot_hp(n_rk, v_h)
        s_new = s_h * e_chunk[:, sl] + _dot_tn_hp(u, bh) + _dot_tn_hp(v_h, kh)
        s_s[h] = s_new
        sout_ref[0, h] = s_new
        mean = jnp.mean(y, axis=-1, keepdims=True)
        yc = y - mean
        var = jnp.mean(yc * yc, axis=-1, keepdims=True)
        yn = yc * lax.rsqrt(var + LNX_EPS) * pv(PV_LNW)[:, sl] + pv(PV_LNB)[:, sl]
        bonus = jnp.sum(r_h * k_h * pv(PV_RK)[:, sl], axis=-1, keepdims=True)
        ob_ref[0, :, sl] = (yn + bonus * v_h) * g[:, sl]


def rwkv_mixer(p, shift0, s0, pvec, mu_l, w2p, a2p, g2p, vres, *, cols):
    b, t, _ = p.shape
    nc = t // CHUNK
    n_heads = s0.shape[1]
    cb = n_heads * HEAD_DIM
    rc, kc, vc, lc = cols
    has_vres = vres is not None
    tok = lambda col, w: pl.BlockSpec((1, CHUNK, w), lambda bi, c, col=col: (bi, c, col))
    const2 = lambda shape: pl.BlockSpec(shape, lambda bi, c: (0, 0))
    perb = lambda w: pl.BlockSpec((1, 1, w), lambda bi, c: (bi, 0, 0))
    state_spec = pl.BlockSpec((1, n_heads, HEAD_DIM, HEAD_DIM), lambda bi, c: (bi, 0, 0, 0))
    operands = [p, p, p, p, *shift0, s0, pvec, mu_l, w2p, a2p, g2p]
    in_specs = [tok(rc, cb), tok(kc, cb), tok(vc, cb), tok(lc, LORA_W),
                perb(cb), perb(cb), perb(cb), perb(LORA_W), state_spec,
                const2((PV_ROWS, cb)), const2((1, LORA_W)),
                const2((LORA_W, cb)), const2((LORA_W, cb)), const2((LORA_W, cb))]
    out_specs = [tok(0, cb), state_spec]
    out_shape = [jax.ShapeDtypeStruct((b, t, cb), F32), jax.ShapeDtypeStruct(s0.shape, F32)]
    if has_vres:
        v1p, v2p, v_first = vres
        operands += [v1p, v2p, v_first]
        in_specs += [const2(v1p.shape), const2(v2p.shape), tok(0, cb)]
    else:
        out_specs.append(tok(0, cb))
        out_shape.append(jax.ShapeDtypeStruct((b, t, cb), F32))
    outs = pl.pallas_call(
        functools.partial(_rwkv_kernel, n_heads=n_heads, has_vres=has_vres),
        grid=(b, nc),
        in_specs=in_specs,
        out_specs=out_specs,
        out_shape=out_shape,
        scratch_shapes=[pltpu.VMEM((n_heads, HEAD_DIM, HEAD_DIM), F32),
                        pltpu.VMEM((8, cb), F32), pltpu.VMEM((8, cb), F32), pltpu.VMEM((8, cb), F32),
                        pltpu.VMEM((8, LORA_W), F32)],
        compiler_params=_cparams(("parallel", "arbitrary")),
        name="rwkv_mixer",
    )(*operands)
    return (outs[0], outs[1], None) if has_vres else tuple(outs)


def _band_rel(n_left):
    nk = (n_left + 1) * CHUNK
    return np.arange(nk)[None, :] - n_left * CHUNK - np.arange(CHUNK)[:, None]


def _t5_bucket(rel):
    nb = T5_BUCKETS // 2
    exact = nb // 2
    n = np.abs(rel)
    nf = np.maximum(n, exact).astype(np.float32)
    large = exact + (np.log(nf / exact) / math.log(T5_MAX_DIST / exact) * (nb - exact)).astype(np.int32)
    return np.where(rel > 0, nb, 0) + np.where(n < exact, n, np.minimum(large, nb - 1))


def _bias_a(t5_table):
    idx = _t5_bucket(_band_rel(A_LEFT))
    return jnp.moveaxis(t5_table[jnp.asarray(idx)], -1, 0).astype(F32)


def _bias_c(rel_table):
    idx = np.clip(-_band_rel(C_LEFT), -REL_CLIP, REL_CLIP) + REL_CLIP
    return rel_table[:, jnp.asarray(idx)].astype(F32)


def _pad_rows(w, start, total):
    return jnp.zeros((total, w.shape[1]), w.dtype).at[start:start + w.shape[0]].set(w)


def kernel(x_prompt, x_sample, cache_a_k, cache_a_v, state_b_wkv, state_b_shift, cache_c_k, cache_c_v, state_d_conv, norm_mix_g, norm_ffn_g, norm_final_g, t5_table, w_in_e, w_out_e, a_sink, b_mu, b_w0, b_w2, b_a0, b_a2, b_g2, b_kk, b_ka, b_rk, b_lnx_w, b_lnx_b, b_v0, b_v1, b_v2, w_in_o, w_out_o, c_rel_table, d_dw_w, d_dw_b, d_ln_g, d_ln_b, ffn_w_gate, ffn_w_up, ffn_w_down):
    depth, d_model = norm_mix_g.shape
    n_even = w_in_e.shape[0]
    h_a = a_sink.shape[1]
    g_a = h_a // KVH_A
    qa = h_a * HEAD_DIM
    kva = KVH_A * HEAD_DIM
    h_b = state_b_wkv.shape[2]
    cb = h_b * HEAD_DIM
    h_c = c_rel_table.shape[1]
    qc = h_c * HEAD_DIM
    cd = d_dw_w.shape[2]
    d_ff = ffn_w_gate.shape[2]
    tf = d_ff // 2 if (d_ff // 2) % 128 == 0 else d_ff

    groups = [x_prompt, x_sample]
    dims = [x.shape[:2] for x in groups]
    xs = [x.reshape(-1, d_model) for x in groups]
    tms = [min(512, x.shape[0]) for x in xs]
    conv_tq = [min(512, t) for _, t in dims]

    o_q, o_k, o_v, o_pb = 0, qa, qa + kva, qa + 2 * kva
    o_r, o_wd, o_kb, o_vb = o_pb, o_pb + cb, o_pb + cb + DECAY_LORA, o_pb + 2 * cb + DECAY_LORA
    o_ad = o_vb + cb
    o_gd = o_ad + ICLR_LORA
    perm = np.concatenate([np.arange(o_q, o_q + qa), np.arange(o_r, o_r + cb), np.arange(o_kb, o_kb + cb),
                           np.arange(o_vb, o_vb + cb), np.arange(o_k, o_k + kva), np.arange(o_v, o_v + kva),
                           np.arange(o_wd, o_wd + DECAY_LORA), np.arange(o_ad, o_ad + ICLR_LORA),
                           np.arange(o_gd, o_gd + GATE_LORA)])
    n_r, n_kb, n_vb = qa, qa + cb, qa + 2 * cb
    n_ka = qa + 3 * cb
    n_va = n_ka + kva
    n_lo = n_va + kva
    sh_cols = [np.arange(c0, c0 + w) for c0, w in ((n_r, cb), (n_kb, cb), (n_vb, cb), (n_lo, LORA_W))]
    sh_src = [perm[cc] - o_pb for cc in sh_cols]
    inv_shift = np.argsort(np.concatenate(sh_src))

    bias_a = _bias_a(t5_table)
    st = [[[] for _ in range(7)] for _ in range(2)]
    v_first = [None, None]
    for i in range(depth):
        j = i // 2
        last = i == depth - 1
        if i % 2 == 0:
            w_in = w_in_e[j][:, perm].astype(BF16)
            wo = w_out_e[j].astype(BF16)
            mu = b_mu[j]
            pvec = jnp.zeros((PV_ROWS, cb), F32)
            rows = {PV_MU_R: mu[sh_src[0]], PV_MU_K: mu[sh_src[1]], PV_MU_V: mu[sh_src[2]], PV_W0: b_w0[j],
                    PV_A0: b_a0[j], PV_KK: b_kk[j], PV_KA: b_ka[j], PV_LNW: b_lnx_w[j], PV_LNB: b_lnx_b[j],
                    PV_RK: b_rk[j].reshape(cb)}
            if j > 0:
                rows[PV_V0] = b_v0[j - 1]
            for ri, val in rows.items():
                pvec = pvec.at[ri].set(val)
            mu_l = mu[sh_src[3]].reshape(1, LORA_W)
            w2p = _pad_rows(b_w2[j], 0, LORA_W).astype(BF16)
            a2p = _pad_rows(b_a2[j], DECAY_LORA, LORA_W).astype(BF16)
            g2p = _pad_rows(b_g2[j], DECAY_LORA + ICLR_LORA, LORA_W).astype(BF16)
            if j > 0:
                lora_v = b_v1.shape[2]
                v1p = jnp.zeros((cb, 128), F32).at[:, :lora_v].set(b_v1[j - 1]).astype(BF16)
                v2p = _pad_rows(b_v2[j - 1], 0, 128).astype(BF16)
            bias = bias_a
        else:
            w_in = w_in_o[j].astype(BF16)
            wo = w_out_o[j].astype(BF16)
            bias = _bias_c(c_rel_table[j])
        wg = ffn_w_gate[i].astype(BF16)
        wu = ffn_w_up[i].astype(BF16)
        wd = ffn_w_down[i].astype(BF16)
        for gi in range(2):
            b, t = dims[gi]
            p = proj_in(xs[gi], norm_mix_g[i], w_in, tms[gi]).reshape(b, t, -1)
            if i % 2 == 0:
                if gi == 0:
                    kv_srcs = None
                    shift0 = [jnp.zeros((b, 1, len(s)), F32) for s in sh_src]
                    s0 = jnp.zeros((b, h_b, HEAD_DIM, HEAD_DIM), F32)
                else:
                    kv_srcs = (cache_a_k[j].reshape(b, -1, kva), cache_a_v[j].reshape(b, -1, kva))
                    shift0 = [state_b_shift[j][:, s].reshape(b, 1, len(s)) for s in sh_src]
                    s0 = state_b_wkv[j]
                o1 = banded_attention(p, kv_srcs, bias, a_sink[j], n_left=A_LEFT, n_heads=h_a, group=g_a,
                                      kcol=n_ka // kva, vcol=n_va // kva, masked=gi == 0)
                vres = None if j == 0 else (v1p, v2p, v_first[gi])
                o2, wkv, vf = rwkv_mixer(p, shift0, s0, pvec, mu_l, w2p, a2p, g2p, vres,
                                         cols=(n_r // cb, n_kb // cb, n_vb // cb, n_lo // LORA_W))
                if j == 0:
                    v_first[gi] = vf
                keep = min(A_LEFT * CHUNK, t)
                nk = p[:, t - keep:, n_ka:n_ka + kva].reshape(b, keep, KVH_A, HEAD_DIM)
                nv = p[:, t - keep:, n_va:n_va + kva].reshape(b, keep, KVH_A, HEAD_DIM)
                nshift = jnp.concatenate([p[:, t - 1, cc[0]:cc[0] + len(cc)] for cc in sh_cols], axis=-1)[:, inv_shift]
                for s_list, val in zip(st[gi][0:4], (nk, nv, wkv, nshift)):
                    s_list.append(val)
            else:
                if gi == 0:
                    kv_srcs = None
                    conv_prev = None
                else:
                    kv_srcs = (cache_c_k[j].reshape(b, -1, qc), cache_c_v[j].reshape(b, -1, qc))
                    conv_prev = state_d_conv[j]
                o1 = banded_attention(p, kv_srcs, bias, None, n_left=C_LEFT, n_heads=h_c, group=1,
                                      kcol=1, vcol=2, masked=gi == 0)
                o2, nconv = conv_module(p, conv_prev, d_dw_w[j], d_dw_b[j], d_ln_g[j], d_ln_b[j],
                                        acol=3 * qc // cd, gcol=3 * qc // cd + 1, tq=conv_tq[gi])
                keep = min(C_LEFT * CHUNK, t)
                nk = p[:, t - keep:, qc:2 * qc].reshape(b, keep, h_c, HEAD_DIM)
                nv = p[:, t - keep:, 2 * qc:3 * qc].reshape(b, keep, h_c, HEAD_DIM)
                for s_list, val in zip(st[gi][4:7], (nk, nv, nconv)):
                    s_list.append(val)
            half = o1.shape[-1]
            xs[gi] = post(xs[gi], o1.reshape(-1, half), o2.reshape(-1, half), wo, norm_ffn_g[i], wg, wu, wd,
                          norm_final_g, tm=tms[gi], tf=tf, final_norm=last)
    y_prompt = xs[0].reshape(x_prompt.shape)
    y_sample = xs[1].reshape(x_sample.shape)
    (pak, pav, pbw, pbs, pck, pcv, pdc), (sak, sav, sbw, sbs, sck, scv, sdc) = [[jnp.stack(s) for s in g] for g in st]
    return (y_prompt, y_sample, pak, pav, pbw, pbs, pck, pcv, pdc, sak, sav, sbw, sbs, sck, scv, sdc)
```

```python
import functools
import math

import jax
import jax.numpy as jnp
import numpy as np
from jax import lax
from jax.experimental import pallas as pl
from jax.experimental.pallas import tpu as pltpu

F32 = jnp.float32
BF16 = jnp.bfloat16
HIGHEST = lax.Precision.HIGHEST

CHUNK = 64
HEAD_DIM = 64
RMS_EPS = 1e-6
NEG_INF = -1e30
LNX_EPS = 64e-5
LN_EPS = 1e-5
CONV_W = 31
T5_BUCKETS = 32
T5_MAX_DIST = 128
REL_CLIP = 128
A_LEFT = 2
C_LEFT = 8
KVH_A = 2
DECAY_LORA = 64
ICLR_LORA = 64
GATE_LORA = 128
LORA_W = DECAY_LORA + ICLR_LORA + GATE_LORA
CONV_HALO = 32
CONV_ROWS = 64

VMEM_LIMIT = 56 * 1024 * 1024


def _cparams(sem):
    return pltpu.CompilerParams(dimension_semantics=sem, vmem_limit_bytes=VMEM_LIMIT)


def _rms(x, g):
    return x * lax.rsqrt(jnp.mean(x * x, axis=-1, keepdims=True) + RMS_EPS) * g


def _dot(a, b):
    return jnp.dot(a, b, preferred_element_type=F32)


def _dot_hp(a, b):
    return jnp.dot(a, b, precision=HIGHEST, preferred_element_type=F32)


def _dot_nt_hp(a, b):
    return lax.dot_general(a, b, (((1,), (1,)), ((), ())), precision=HIGHEST, preferred_element_type=F32)


def _dot_tn_hp(a, b):
    return lax.dot_general(a, b, (((0,), (0,)), ((), ())), precision=HIGHEST, preferred_element_type=F32)


def _proj_in_kernel(x_ref, g_ref, w_ref, o_ref):
    h = _rms(x_ref[...], g_ref[...]).astype(BF16)
    o_ref[...] = _dot(h, w_ref[...])


def proj_in(x2d, g, w_bf16, tm):
    m, d = x2d.shape
    n = w_bf16.shape[1]
    return pl.pallas_call(
        _proj_in_kernel,
        grid=(m // tm,),
        in_specs=[pl.BlockSpec((tm, d), lambda i: (i, 0)),
                  pl.BlockSpec((1, d), lambda i: (0, 0)),
                  pl.BlockSpec((d, n), lambda i: (0, 0))],
        out_specs=pl.BlockSpec((tm, n), lambda i: (i, 0)),
        out_shape=jax.ShapeDtypeStruct((m, n), F32),
        compiler_params=_cparams(("parallel",)),
        name="proj_in",
    )(x2d, g.reshape(1, d), w_bf16)


def _post_kernel(x_ref, o1_ref, o2_ref, wo_ref, g_ref, wg_ref, wu_ref, wd_ref, gf_ref, out_ref,
                 xn_s, h_s, acc_s, *, final_norm):
    f = pl.program_id(1)
    half = o1_ref.shape[-1]

    @pl.when(f == 0)
    def _():
        xn = (x_ref[...] + _dot(o1_ref[...].astype(BF16), wo_ref[0:half, :])
              + _dot(o2_ref[...].astype(BF16), wo_ref[half:2 * half, :]))
        xn_s[...] = xn
        h_s[...] = _rms(xn, g_ref[...]).astype(BF16)
        acc_s[...] = jnp.zeros_like(acc_s)

    h = h_s[...]
    gate = _dot(h, wg_ref[...])
    up = _dot(h, wu_ref[...])
    act = gate * jax.nn.sigmoid(gate) * up
    acc_s[...] += _dot(act.astype(BF16), wd_ref[...])

    @pl.when(f == pl.num_programs(1) - 1)
    def _():
        y = xn_s[...] + acc_s[...]
        if final_norm:
            y = _rms(y, gf_ref[...])
        out_ref[...] = y


def post(x2d, o1, o2, wo, g, wg, wu, wd, gf, *, tm, tf, final_norm):
    m, d = x2d.shape
    half = o1.shape[-1]
    dff = wg.shape[1]
    return pl.pallas_call(
        functools.partial(_post_kernel, final_norm=final_norm),
        grid=(m // tm, dff // tf),
        in_specs=[pl.BlockSpec((tm, d), lambda i, f: (i, 0)),
                  pl.BlockSpec((tm, half), lambda i, f: (i, 0)),
                  pl.BlockSpec((tm, half), lambda i, f: (i, 0)),
                  pl.BlockSpec((2 * half, d), lambda i, f: (0, 0)),
                  pl.BlockSpec((1, d), lambda i, f: (0, 0)),
                  pl.BlockSpec((d, tf), lambda i, f: (0, f)),
                  pl.BlockSpec((d, tf), lambda i, f: (0, f)),
                  pl.BlockSpec((tf, d), lambda i, f: (f, 0)),
                  pl.BlockSpec((1, d), lambda i, f: (0, 0))],
        out_specs=pl.BlockSpec((tm, d), lambda i, f: (i, 0)),
        out_shape=jax.ShapeDtypeStruct((m, d), F32),
        scratch_shapes=[pltpu.VMEM((tm, d), F32), pltpu.VMEM((tm, d), BF16), pltpu.VMEM((tm, d), F32)],
        compiler_params=_cparams(("parallel", "arbitrary")),
        name="post",
    )(x2d, o1, o2, wo, g.reshape(1, d), wg, wu, wd, gf.reshape(1, d))


def _attn_kernel(*refs, n_blocks, n_heads, group, has_sink, masked):
    q_ref = refs[0]
    k_refs = refs[1:1 + n_blocks]
    v_refs = refs[1 + n_blocks:1 + 2 * n_blocks]
    bias_ref = refs[1 + 2 * n_blocks]
    sink_ref = refs[2 + 2 * n_blocks] if has_sink else None
    o_ref = refs[-1]
    nk = n_blocks * CHUNK

    q = (q_ref[0] * (HEAD_DIM ** -0.5)).astype(BF16)
    kband = jnp.concatenate([r[0] for r in k_refs], axis=0).astype(BF16)
    vband = jnp.concatenate([r[0] for r in v_refs], axis=0).astype(BF16)
    if masked:
        col = lax.broadcasted_iota(jnp.int32, (CHUNK, nk), 1)
        valid = col >= (n_blocks - 1 - pl.program_id(1)) * CHUNK
    for h in range(n_heads):
        kh = h // group
        qh = q[:, h * HEAD_DIM:(h + 1) * HEAD_DIM]
        s = lax.dot_general(qh, kband[:, kh * HEAD_DIM:(kh + 1) * HEAD_DIM], (((1,), (1,)), ((), ())),
                            preferred_element_type=F32) + bias_ref[h]
        if masked:
            s = jnp.where(valid, s, NEG_INF)
        m = jnp.max(s, axis=-1, keepdims=True)
        if has_sink:
            sk = sink_ref[h:h + 1, :]
            m = jnp.maximum(m, sk)
        e = jnp.exp(s - m)
        den = jnp.sum(e, axis=-1, keepdims=True)
        if has_sink:
            den = den + jnp.exp(sk - m)
        o = _dot(e.astype(BF16), vband[:, kh * HEAD_DIM:(kh + 1) * HEAD_DIM]) / den
        o_ref[0, :, h * HEAD_DIM:(h + 1) * HEAD_DIM] = o


def banded_attention(p, kv_srcs, bias, sink, *, n_left, n_heads, group, kcol, vcol, masked):
    b, t, _ = p.shape
    nc = t // CHUNK
    kvw = (n_heads // group) * HEAD_DIM
    qw = n_heads * HEAD_DIM
    n_blocks = n_left + 1
    operands = [p]
    in_specs = [pl.BlockSpec((1, CHUNK, qw), lambda bi, c: (bi, c, 0))]
    for col in (kcol, vcol):
        for j in range(n_blocks):
            if kv_srcs is None:
                operands.append(p)
                in_specs.append(pl.BlockSpec(
                    (1, CHUNK, kvw), lambda bi, c, j=j, col=col: (bi, jnp.maximum(c - n_left + j, 0), col)))
            elif j < n_left:
                operands.append(kv_srcs[0] if col == kcol else kv_srcs[1])
                in_specs.append(pl.BlockSpec((1, CHUNK, kvw), lambda bi, c, j=j: (bi, j, 0)))
            else:
                operands.append(p)
                in_specs.append(pl.BlockSpec((1, CHUNK, kvw), lambda bi, c, col=col: (bi, c, col)))
    operands.append(bias)
    in_specs.append(pl.BlockSpec(bias.shape, lambda bi, c: (0, 0, 0)))
    if sink is not None:
        operands.append(sink.reshape(n_heads, 1))
        in_specs.append(pl.BlockSpec((n_heads, 1), lambda bi, c: (0, 0)))
    return pl.pallas_call(
        functools.partial(_attn_kernel, n_blocks=n_blocks, n_heads=n_heads, group=group,
                          has_sink=sink is not None, masked=masked),
        grid=(b, nc),
        in_specs=in_specs,
        out_specs=pl.BlockSpec((1, CHUNK, qw), lambda bi, c: (bi, c, 0)),
        out_shape=jax.ShapeDtypeStruct((b, t, qw), F32),
        compiler_params=_cparams(("parallel", "parallel")),
        name="banded_attention",
    )(*operands)


def _conv_kernel(*refs, tq, from_state):
    if from_state:
        a_ref, gt_ref, st_in_ref, dw_ref, db_ref, lg_ref, lb_ref, o_ref, st_ref, buf = refs
    else:
        a_ref, gt_ref, pa_ref, pg_ref, dw_ref, db_ref, lg_ref, lb_ref, o_ref, st_ref, buf = refs
    keep = CONV_W - 1
    pad = CONV_HALO - keep
    if from_state:
        buf[0:pad, :] = jnp.zeros((pad, buf.shape[1]), F32)
        buf[pad:CONV_HALO, :] = st_in_ref[0]
    else:
        prev = pa_ref[0] * jax.nn.sigmoid(pg_ref[0])
        buf[0:CONV_HALO, :] = jnp.where(pl.program_id(1) > 0, prev, 0.0)
    buf[CONV_HALO:CONV_HALO + tq, :] = a_ref[0] * jax.nn.sigmoid(gt_ref[0])
    rs = min(tq, CONV_ROWS)
    for r0 in range(0, tq, rs):
        z = jnp.zeros((rs, buf.shape[1]), F32) + db_ref[...]
        for w in range(CONV_W):
            z = z + buf[r0 + pad + w:r0 + pad + w + rs, :] * dw_ref[w:w + 1, :]
        mean = jnp.mean(z, axis=-1, keepdims=True)
        zc = z - mean
        var = jnp.mean(zc * zc, axis=-1, keepdims=True)
        zn = zc * lax.rsqrt(var + LN_EPS) * lg_ref[...] + lb_ref[...]
        o_ref[0, r0:r0 + rs, :] = zn * jax.nn.sigmoid(zn)
    st_ref[0] = buf[tq + pad:tq + CONV_HALO, :]


def conv_module(p, state, dw_w, dw_b, ln_g, ln_b, *, acol, gcol, tq):
    b, t, _ = p.shape
    cd = dw_w.shape[1]
    keep = CONV_W - 1
    from_state = state is not None
    operands = [p, p]
    in_specs = [pl.BlockSpec((1, tq, cd), lambda bi, i: (bi, i, acol)),
                pl.BlockSpec((1, tq, cd), lambda bi, i: (bi, i, gcol))]
    if from_state:
        operands.append(state)
        in_specs.append(pl.BlockSpec((1, keep, cd), lambda bi, i: (bi, 0, 0)))
    else:
        r = tq // CONV_HALO
        for col in (acol, gcol):
            operands.append(p)
            in_specs.append(pl.BlockSpec((1, CONV_HALO, cd),
                                         lambda bi, i, col=col: (bi, jnp.maximum(i * r - 1, 0), col)))
    operands += [dw_w, dw_b.reshape(1, cd), ln_g.reshape(1, cd), ln_b.reshape(1, cd)]
    in_specs += [pl.BlockSpec((CONV_W, cd), lambda bi, i: (0, 0))] + [pl.BlockSpec((1, cd), lambda bi, i: (0, 0))] * 3
    return pl.pallas_call(
        functools.partial(_conv_kernel, tq=tq, from_state=from_state),
        grid=(b, t // tq),
        in_specs=in_specs,
        out_specs=[pl.BlockSpec((1, tq, cd), lambda bi, i: (bi, i, 0)),
                   pl.BlockSpec((1, keep, cd), lambda bi, i: (bi, 0, 0))],
        out_shape=[jax.ShapeDtypeStruct((b, t, cd), F32), jax.ShapeDtypeStruct((b, keep, cd), F32)],
        scratch_shapes=[pltpu.VMEM((CONV_HALO + tq, cd), F32)],
        compiler_params=_cparams(("parallel", "arbitrary")),
        name="conv_module",
    )(*operands)


PV_MU_R, PV_MU_K, PV_MU_V, PV_W0, PV_A0, PV_KK, PV_KA, PV_V0, PV_LNW, PV_LNB, PV_RK = range(11)
PV_ROWS = 16
RWKV_BATCH_BLOCK = 2


def _shift_rows(x, first_row):
    rolled = pltpu.roll(x, 1, axis=0)
    row = lax.broadcasted_iota(jnp.int32, x.shape, 0)
    return jnp.where(row == 0, first_row, rolled)


def _rwkv_kernel(*refs, n_heads, has_vres, bb):
    (xr_ref, xk_ref, xv_ref, xl_ref, shr_ref, shk_ref, shv_ref, shl_ref, s0_ref, pv_ref, mul_ref,
     w2_ref, a2_ref, g2_ref) = refs[:14]
    pos = 14
    if has_vres:
        v1_ref, v2_ref, vf_ref = refs[pos:pos + 3]
        pos += 3
    ob_ref, sout_ref = refs[pos:pos + 2]
    pos += 2
    if not has_vres:
        vf_out_ref = refs[pos]
        pos += 1
    s_s, lr_s, lk_s, lv_s, ll_s = refs[pos:pos + 5]
    L = CHUNK
    c = pl.program_id(1)

    n_pairs = n_heads // 2
    PW = 2 * HEAD_DIM

    @pl.when(c == 0)
    def _():
        zero = jnp.zeros((HEAD_DIM, HEAD_DIM), F32)
        for bi in range(bb):
            for p in range(n_pairs):
                top = jnp.concatenate([s0_ref[bi, 2 * p], zero], axis=1)
                bottom = jnp.concatenate([zero, s0_ref[bi, 2 * p + 1]], axis=1)
                s_s[bi * n_pairs + p] = jnp.concatenate([top, bottom], axis=0)
            lr_s[bi, 0:1, :] = shr_ref[bi]
            lk_s[bi, 0:1, :] = shk_ref[bi]
            lv_s[bi, 0:1, :] = shv_ref[bi]
            ll_s[bi, 0:1, :] = shl_ref[bi]

    def token_shift(x_ref, last_s, mu, bi):
        x = x_ref[bi]
        prev = _shift_rows(x, last_s[bi, 0:1, :])
        last_s[bi, 0:1, :] = x[L - 1:L, :]
        return x + (prev - x) * mu

    pv = lambda i: pv_ref[i:i + 1, :]
    row = lax.broadcasted_iota(jnp.int32, (L, L), 0)
    col = lax.broadcasted_iota(jnp.int32, (L, L), 1)
    tri_incl = (row >= col).astype(F32)

    full = []
    for bi in range(bb):
        r = token_shift(xr_ref, lr_s, pv(PV_MU_R), bi)
        k = token_shift(xk_ref, lk_s, pv(PV_MU_K), bi)
        v = token_shift(xv_ref, lv_s, pv(PV_MU_V), bi)
        lo = token_shift(xl_ref, ll_s, mul_ref[...], bi)

        zw = pv(PV_W0) + _dot(jnp.tanh(lo).astype(BF16), w2_ref[...])
        w = -(jnp.maximum(-zw, 0.0) + jnp.log(1.0 + jnp.exp(-jnp.abs(zw)))) - 0.5
        d = -jnp.exp(w)
        iclr = jax.nn.sigmoid(pv(PV_A0) + _dot(lo.astype(BF16), a2_ref[...]))
        g = _dot(jax.nn.sigmoid(lo).astype(BF16), g2_ref[...])
        if has_vres:
            mix = jax.nn.sigmoid(pv(PV_V0) + _dot(_dot(v.astype(BF16), v1_ref[...]).astype(BF16), v2_ref[...]))
            v = v + (vf_ref[bi] - v) * mix
        else:
            vf_out_ref[bi] = v
        cs = _dot_hp(tri_incl, d)
        cprev = cs - d
        c_last = cs[L - 1:L, :]
        c_mid = cs[L // 2:L // 2 + 1, :]
        full.append(dict(
            r=r, v=v, g=g, iclr=iclr, kks=k * pv(PV_KK), kh=k * (1.0 + (iclr - 1.0) * pv(PV_KA)),
            e_start_prev=jnp.exp(cprev),
            e_start_incl=jnp.exp(cs),
            e_mid_prev=jnp.exp(cprev - c_mid), e_mid_incl=jnp.exp(cs - c_mid), e_from_mid=jnp.exp(c_mid - cs),
            e_to_end=jnp.exp(c_last - cs),
            e_chunk=jnp.exp(c_last)))

    lane_lo = lax.broadcasted_iota(jnp.int32, (1, PW), 1) < HEAD_DIM
    r2 = lax.broadcasted_iota(jnp.int32, (2 * L, 2 * L), 0)
    c2 = lax.broadcasted_iota(jnp.int32, (2 * L, 2 * L), 1)
    same_head = (r2 >= L) == (c2 >= L)
    strict_bd = same_head & (jnp.bitwise_and(r2, L - 1) > jnp.bitwise_and(c2, L - 1))
    eye_bd = (r2 == c2).astype(F32)
    incl_c = (lax.broadcasted_iota(jnp.int32, (L, 2 * L), 0)
              >= jnp.bitwise_and(lax.broadcasted_iota(jnp.int32, (L, 2 * L), 1), L - 1))

    def bd(x):
        return jnp.concatenate([jnp.where(lane_lo, x, 0.0), jnp.where(lane_lo, 0.0, x)], axis=0)

    def tile2(x):
        return jnp.concatenate([x, x], axis=0)

    def head_sum(x):
        lo_sum = jnp.sum(jnp.where(lane_lo, x, 0.0), axis=-1, keepdims=True)
        hi_sum = jnp.sum(jnp.where(lane_lo, 0.0, x), axis=-1, keepdims=True)
        return jnp.where(lane_lo, lo_sum, hi_sum)

    bf = lambda z: z.astype(BF16)
    dot_nt = lambda p_, q_: lax.dot_general(p_, q_, (((1,), (1,)), ((), ())), preferred_element_type=F32)
    dot_tn = lambda p_, q_: lax.dot_general(p_, q_, (((0,), (0,)), ((), ())), preferred_element_type=F32)

    units = [(bi, p) for bi in range(bb) for p in range(n_pairs)]
    sls = [slice(p * PW, (p + 1) * PW) for _, p in units]
    get = lambda name: [full[bi][name][:, sl] for (bi, _), sl in zip(units, sls)]
    mul = lambda xs_, ys_: [x_ * y_ for x_, y_ in zip(xs_, ys_)]

    kk = get("kks")
    kk = [z / jnp.maximum(jnp.sqrt(head_sum(z * z)), 1e-12) for z in kk]
    a_p = [-z for z in kk]
    b_p = mul(kk, get("iclr"))
    k_p, r_p, v_p = get("kh"), get("r"), get("v")
    a0 = mul(a_p, get("e_start_prev"))
    r0 = mul(r_p, get("e_start_incl"))
    at = mul(a_p, get("e_mid_prev"))
    rt = mul(r_p, get("e_mid_incl"))
    e_from_mid, e_to_end = get("e_from_mid"), get("e_to_end")
    bt, kt = mul(b_p, e_from_mid), mul(k_p, e_from_mid)
    bh, kh = mul(b_p, e_to_end), mul(k_p, e_to_end)

    at_bd = [bf(bd(z)) for z in at]
    rt_b = [bf(z) for z in rt]
    n_ab = [jnp.where(strict_bd, dot_nt(x_, bf(tile2(y_))), 0.0) for x_, y_ in zip(at_bd, bt)]
    n_ak = [jnp.where(strict_bd, dot_nt(x_, bf(tile2(y_))), 0.0) for x_, y_ in zip(at_bd, kt)]
    n_rb = [jnp.where(incl_c, dot_nt(x_, bf(bd(y_))), 0.0) for x_, y_ in zip(rt_b, bt)]
    n_rk = [jnp.where(incl_c, dot_nt(x_, bf(bd(y_))), 0.0) for x_, y_ in zip(rt_b, kt)]
    x = [eye_bd + z for z in n_ab]
    pw = n_ab
    for _ in range(int(math.log2(L)) - 1):
        pw_b = [bf(z) for z in pw]
        pw = [_dot(z, z) for z in pw_b]
        x = [x_ + _dot(bf(x_), bf(p_)) for x_, p_ in zip(x, pw)]
    x_b = [bf(z) for z in x]
    v_bd = [bf(bd(z)) for z in v_p]
    w1 = [_dot(x_, bf(bd(y_))) for x_, y_ in zip(x_b, a0)]
    t2 = [bf(_dot(bf(x_), y_)) for x_, y_ in zip(n_ak, v_bd)]
    w2 = [_dot(x_, y_) for x_, y_ in zip(x_b, t2)]
    s_old = [s_s[i] for i in range(len(units))]
    s_b = [bf(z) for z in s_old]
    u = [dot_nt(bf(x_), y_) + z_ for x_, y_, z_ in zip(w1, s_b, w2)]
    y = [dot_nt(bf(r_), s_) + _dot(bf(nb_), bf(u_)) + _dot(bf(nk_), v_)
         for r_, s_, nb_, u_, nk_, v_ in zip(r0, s_b, n_rb, u, n_rk, v_bd)]
    upd = [dot_tn(bf(jnp.concatenate([u_[0:L] + u_[L:2 * L], v_], axis=0)), bf(jnp.concatenate([b_, k_], axis=0)))
           for u_, v_, b_, k_ in zip(u, v_p, bh, kh)]
    for i, (s_, e_, d_) in enumerate(zip(s_old, get("e_chunk"), upd)):
        s_s[i] = s_ * e_ + jnp.where(same_head, d_, 0.0)
    g_p = get("g")
    for i, ((bi, _), sl) in enumerate(zip(units, sls)):
        mean = head_sum(y[i]) * (1.0 / HEAD_DIM)
        yc = y[i] - mean
        var = head_sum(yc * yc) * (1.0 / HEAD_DIM)
        yn = yc * lax.rsqrt(var + LNX_EPS) * pv(PV_LNW)[:, sl] + pv(PV_LNB)[:, sl]
        bonus = head_sum(r_p[i] * k_p[i] * pv(PV_RK)[:, sl])
        ob_ref[bi, :, sl] = (yn + bonus * v_p[i]) * g_p[i]

    @pl.when(c == pl.num_programs(1) - 1)
    def _():
        for i, (bi, p) in enumerate(units):
            s_p = s_s[i]
            sout_ref[bi, 2 * p] = s_p[0:HEAD_DIM, 0:HEAD_DIM]
            sout_ref[bi, 2 * p + 1] = s_p[HEAD_DIM:PW, HEAD_DIM:PW]


def rwkv_mixer(p, shift0, s0, pvec, mu_l, w2p, a2p, g2p, vres, *, cols):
    b, t, _ = p.shape
    nc = t // CHUNK
    n_heads = s0.shape[1]
    cb = n_heads * HEAD_DIM
    rc, kc, vc, lc = cols
    has_vres = vres is not None
    bb = RWKV_BATCH_BLOCK if b % RWKV_BATCH_BLOCK == 0 else 1
    tok = lambda col, w: pl.BlockSpec((bb, CHUNK, w), lambda bi, c, col=col: (bi, c, col))
    const2 = lambda shape: pl.BlockSpec(shape, lambda bi, c: (0, 0))
    perb = lambda w: pl.BlockSpec((bb, 1, w), lambda bi, c: (bi, 0, 0))
    state_spec = pl.BlockSpec((bb, n_heads, HEAD_DIM, HEAD_DIM), lambda bi, c: (bi, 0, 0, 0))
    operands = [p, p, p, p, *shift0, s0, pvec, mu_l, w2p, a2p, g2p]
    in_specs = [tok(rc, cb), tok(kc, cb), tok(vc, cb), tok(lc, LORA_W),
                perb(cb), perb(cb), perb(cb), perb(LORA_W), state_spec,
                const2((PV_ROWS, cb)), const2((1, LORA_W)),
                const2((LORA_W, cb)), const2((LORA_W, cb)), const2((LORA_W, cb))]
    out_specs = [tok(0, cb), state_spec]
    out_shape = [jax.ShapeDtypeStruct((b, t, cb), F32), jax.ShapeDtypeStruct(s0.shape, F32)]
    if has_vres:
        v1p, v2p, v_first = vres
        operands += [v1p, v2p, v_first]
        in_specs += [const2(v1p.shape), const2(v2p.shape), tok(0, cb)]
    else:
        out_specs.append(tok(0, cb))
        out_shape.append(jax.ShapeDtypeStruct((b, t, cb), F32))
    outs = pl.pallas_call(
        functools.partial(_rwkv_kernel, n_heads=n_heads, has_vres=has_vres, bb=bb),
        grid=(b // bb, nc),
        in_specs=in_specs,
        out_specs=out_specs,
        out_shape=out_shape,
        scratch_shapes=[pltpu.VMEM((bb * n_heads // 2, 2 * HEAD_DIM, 2 * HEAD_DIM), F32),
                        pltpu.VMEM((bb, 8, cb), F32), pltpu.VMEM((bb, 8, cb), F32), pltpu.VMEM((bb, 8, cb), F32),
                        pltpu.VMEM((bb, 8, LORA_W), F32)],
        compiler_params=_cparams(("parallel", "arbitrary")),
        name="rwkv_mixer",
    )(*operands)
    return (outs[0], outs[1], None) if has_vres else tuple(outs)


def _band_rel(n_left):
    nk = (n_left + 1) * CHUNK
    return np.arange(nk)[None, :] - n_left * CHUNK - np.arange(CHUNK)[:, None]


def _t5_bucket(rel):
    nb = T5_BUCKETS // 2
    exact = nb // 2
    n = np.abs(rel)
    nf = np.maximum(n, exact).astype(np.float32)
    large = exact + (np.log(nf / exact) / math.log(T5_MAX_DIST / exact) * (nb - exact)).astype(np.int32)
    return np.where(rel > 0, nb, 0) + np.where(n < exact, n, np.minimum(large, nb - 1))


def _bias_a(t5_table):
    idx = _t5_bucket(_band_rel(A_LEFT))
    return jnp.moveaxis(t5_table[jnp.asarray(idx)], -1, 0).astype(F32)


def _bias_c(rel_table):
    idx = np.clip(-_band_rel(C_LEFT), -REL_CLIP, REL_CLIP) + REL_CLIP
    return rel_table[:, jnp.asarray(idx)].astype(F32)


def _pad_rows(w, start, total):
    return jnp.zeros((total, w.shape[1]), w.dtype).at[start:start + w.shape[0]].set(w)


def kernel(x_prompt, x_sample, cache_a_k, cache_a_v, state_b_wkv, state_b_shift, cache_c_k, cache_c_v, state_d_conv, norm_mix_g, norm_ffn_g, norm_final_g, t5_table, w_in_e, w_out_e, a_sink, b_mu, b_w0, b_w2, b_a0, b_a2, b_g2, b_kk, b_ka, b_rk, b_lnx_w, b_lnx_b, b_v0, b_v1, b_v2, w_in_o, w_out_o, c_rel_table, d_dw_w, d_dw_b, d_ln_g, d_ln_b, ffn_w_gate, ffn_w_up, ffn_w_down):
    depth, d_model = norm_mix_g.shape
    n_even = w_in_e.shape[0]
    h_a = a_sink.shape[1]
    g_a = h_a // KVH_A
    qa = h_a * HEAD_DIM
    kva = KVH_A * HEAD_DIM
    h_b = state_b_wkv.shape[2]
    cb = h_b * HEAD_DIM
    h_c = c_rel_table.shape[1]
    qc = h_c * HEAD_DIM
    cd = d_dw_w.shape[2]
    d_ff = ffn_w_gate.shape[2]
    tf = d_ff // 2 if (d_ff // 2) % 128 == 0 else d_ff

    groups = [x_prompt, x_sample]
    dims = [x.shape[:2] for x in groups]
    xs = [x.reshape(-1, d_model) for x in groups]
    tms = [min(512, x.shape[0]) for x in xs]
    conv_tq = [min(512, t) for _, t in dims]

    o_q, o_k, o_v, o_pb = 0, qa, qa + kva, qa + 2 * kva
    o_r, o_wd, o_kb, o_vb = o_pb, o_pb + cb, o_pb + cb + DECAY_LORA, o_pb + 2 * cb + DECAY_LORA
    o_ad = o_vb + cb
    o_gd = o_ad + ICLR_LORA
    perm_ranges = [(o_q, qa), (o_r, cb), (o_kb, cb), (o_vb, cb), (o_k, kva), (o_v, kva),
                   (o_wd, DECAY_LORA), (o_ad, ICLR_LORA), (o_gd, GATE_LORA)]
    perm = np.concatenate([np.arange(a, a + n) for a, n in perm_ranges])
    n_r, n_kb, n_vb = qa, qa + cb, qa + 2 * cb
    n_ka = qa + 3 * cb
    n_va = n_ka + kva
    n_lo = n_va + kva
    sh_cols = [np.arange(c0, c0 + w) for c0, w in ((n_r, cb), (n_kb, cb), (n_vb, cb), (n_lo, LORA_W))]
    sh_src = [perm[cc] - o_pb for cc in sh_cols]
    inv_shift = np.argsort(np.concatenate(sh_src))

    bias_a = _bias_a(t5_table)
    st = [[[] for _ in range(7)] for _ in range(2)]
    v_first = [None, None]
    for i in range(depth):
        j = i // 2
        last = i == depth - 1
        if i % 2 == 0:
            w_in = jnp.concatenate([w_in_e[j][:, a:a + n] for a, n in perm_ranges], axis=1).astype(BF16)
            wo = w_out_e[j].astype(BF16)
            mu = b_mu[j]
            pvec = jnp.zeros((PV_ROWS, cb), F32)
            rows = {PV_MU_R: mu[sh_src[0]], PV_MU_K: mu[sh_src[1]], PV_MU_V: mu[sh_src[2]], PV_W0: b_w0[j],
                    PV_A0: b_a0[j], PV_KK: b_kk[j], PV_KA: b_ka[j], PV_LNW: b_lnx_w[j], PV_LNB: b_lnx_b[j],
                    PV_RK: b_rk[j].reshape(cb)}
            if j > 0:
                rows[PV_V0] = b_v0[j - 1]
            for ri, val in rows.items():
                pvec = pvec.at[ri].set(val)
            mu_l = mu[sh_src[3]].reshape(1, LORA_W)
            w2p = _pad_rows(b_w2[j], 0, LORA_W).astype(BF16)
            a2p = _pad_rows(b_a2[j], DECAY_LORA, LORA_W).astype(BF16)
            g2p = _pad_rows(b_g2[j], DECAY_LORA + ICLR_LORA, LORA_W).astype(BF16)
            if j > 0:
                lora_v = b_v1.shape[2]
                v1p = jnp.zeros((cb, 128), F32).at[:, :lora_v].set(b_v1[j - 1]).astype(BF16)
                v2p = _pad_rows(b_v2[j - 1], 0, 128).astype(BF16)
            bias = bias_a
        else:
            w_in = w_in_o[j].astype(BF16)
            wo = w_out_o[j].astype(BF16)
            bias = _bias_c(c_rel_table[j])
        wg = ffn_w_gate[i].astype(BF16)
        wu = ffn_w_up[i].astype(BF16)
        wd = ffn_w_down[i].astype(BF16)
        for gi in range(2):
            b, t = dims[gi]
            p = proj_in(xs[gi], norm_mix_g[i], w_in, tms[gi]).reshape(b, t, -1)
            if i % 2 == 0:
                if gi == 0:
                    kv_srcs = None
                    shift0 = [jnp.zeros((b, 1, len(s)), F32) for s in sh_src]
                    s0 = jnp.zeros((b, h_b, HEAD_DIM, HEAD_DIM), F32)
                else:
                    kv_srcs = (cache_a_k[j].reshape(b, -1, kva), cache_a_v[j].reshape(b, -1, kva))
                    shift0 = [state_b_shift[j][:, s].reshape(b, 1, len(s)) for s in sh_src]
                    s0 = state_b_wkv[j]
                o1 = banded_attention(p, kv_srcs, bias, a_sink[j], n_left=A_LEFT, n_heads=h_a, group=g_a,
                                      kcol=n_ka // kva, vcol=n_va // kva, masked=gi == 0)
                vres = None if j == 0 else (v1p, v2p, v_first[gi])
                o2, wkv, vf = rwkv_mixer(p, shift0, s0, pvec, mu_l, w2p, a2p, g2p, vres,
                                         cols=(n_r // cb, n_kb // cb, n_vb // cb, n_lo // LORA_W))
                if j == 0:
                    v_first[gi] = vf
                keep = min(A_LEFT * CHUNK, t)
                nk = p[:, t - keep:, n_ka:n_ka + kva].reshape(b, keep, KVH_A, HEAD_DIM)
                nv = p[:, t - keep:, n_va:n_va + kva].reshape(b, keep, KVH_A, HEAD_DIM)
                nshift = jnp.concatenate([p[:, t - 1, cc[0]:cc[0] + len(cc)] for cc in sh_cols], axis=-1)[:, inv_shift]
                for s_list, val in zip(st[gi][0:4], (nk, nv, wkv, nshift)):
                    s_list.append(val)
            else:
                if gi == 0:
                    kv_srcs = None
                    conv_prev = None
                else:
                    kv_srcs = (cache_c_k[j].reshape(b, -1, qc), cache_c_v[j].reshape(b, -1, qc))
                    conv_prev = state_d_conv[j]
                o1 = banded_attention(p, kv_srcs, bias, None, n_left=C_LEFT, n_heads=h_c, group=1,
                                      kcol=1, vcol=2, masked=gi == 0)
                o2, nconv = conv_module(p, conv_prev, d_dw_w[j], d_dw_b[j], d_ln_g[j], d_ln_b[j],
                                        acol=3 * qc // cd, gcol=3 * qc // cd + 1, tq=conv_tq[gi])
                keep = min(C_LEFT * CHUNK, t)
                nk = p[:, t - keep:, qc:2 * qc].reshape(b, keep, h_c, HEAD_DIM)
                nv = p[:, t - keep:, 2 * qc:3 * qc].reshape(b, keep, h_c, HEAD_DIM)
                for s_list, val in zip(st[gi][4:7], (nk, nv, nconv)):
                    s_list.append(val)
            half = o1.shape[-1]
            xs[gi] = post(xs[gi], o1.reshape(-1, half), o2.reshape(-1, half), wo, norm_ffn_g[i], wg, wu, wd,
                          norm_final_g, tm=tms[gi], tf=tf, final_norm=last)
    y_prompt = xs[0].reshape(x_prompt.shape)
    y_sample = xs[1].reshape(x_sample.shape)
    (pak, pav, pbw, pbs, pck, pcv, pdc), (sak, sav, sbw, sbs, sck, scv, sdc) = [[jnp.stack(s) for s in g] for g in st]
    return (y_prompt, y_sample, pak, pav, pbw, pbs, pck, pcv, pdc, sak, sav, sbw, sbs, sck, scv, sdc)
```

```python
import functools
import math

import jax
import jax.numpy as jnp
import numpy as np
from jax import lax
from jax.experimental import pallas as pl
from jax.experimental.pallas import tpu as pltpu

F32 = jnp.float32
BF16 = jnp.bfloat16
HIGHEST = lax.Precision.HIGHEST

CHUNK = 64
HEAD_DIM = 64
RMS_EPS = 1e-6
NEG_INF = -1e30
LNX_EPS = 64e-5
LN_EPS = 1e-5
CONV_W = 31
T5_BUCKETS = 32
T5_MAX_DIST = 128
REL_CLIP = 128
A_LEFT = 2
C_LEFT = 8
KVH_A = 2
DECAY_LORA = 64
ICLR_LORA = 64
GATE_LORA = 128
LORA_W = DECAY_LORA + ICLR_LORA + GATE_LORA
CONV_HALO = 32
CONV_ROWS = 64
PAIR_W = 2 * HEAD_DIM
ATTN_TQ = 256
ATTN_SAMPLE_BB = 2
ATTN_PROMPT_HEAD_GROUP = 4

VMEM_LIMIT = 56 * 1024 * 1024


def _cparams(sem):
    return pltpu.CompilerParams(dimension_semantics=sem, vmem_limit_bytes=VMEM_LIMIT)


def _rms(x, g):
    return x * lax.rsqrt(jnp.mean(x * x, axis=-1, keepdims=True) + RMS_EPS) * g


def _dot(a, b):
    return jnp.dot(a, b, preferred_element_type=F32)


def _dot_hp(a, b):
    return jnp.dot(a, b, precision=HIGHEST, preferred_element_type=F32)


def _dot_nt_hp(a, b):
    return lax.dot_general(a, b, (((1,), (1,)), ((), ())), precision=HIGHEST, preferred_element_type=F32)


def _dot_tn_hp(a, b):
    return lax.dot_general(a, b, (((0,), (0,)), ((), ())), precision=HIGHEST, preferred_element_type=F32)


def _proj_in_kernel(x_ref, g_ref, w_ref, o_ref):
    h = _rms(x_ref[...], g_ref[...]).astype(BF16)
    o_ref[...] = _dot(h, w_ref[...])


def proj_in(x2d, g, w_bf16, tm):
    m, d = x2d.shape
    n = w_bf16.shape[1]
    return pl.pallas_call(
        _proj_in_kernel,
        grid=(m // tm,),
        in_specs=[pl.BlockSpec((tm, d), lambda i: (i, 0)),
                  pl.BlockSpec((1, d), lambda i: (0, 0)),
                  pl.BlockSpec((d, n), lambda i: (0, 0))],
        out_specs=pl.BlockSpec((tm, n), lambda i: (i, 0)),
        out_shape=jax.ShapeDtypeStruct((m, n), F32),
        compiler_params=_cparams(("parallel",)),
        name="proj_in",
    )(x2d, g.reshape(1, d), w_bf16)


def _post_kernel(x_ref, o1_ref, o2_ref, wo_ref, g_ref, wg_ref, wu_ref, wd_ref, gf_ref, out_ref,
                 xn_s, h_s, acc_s, *, final_norm):
    f = pl.program_id(1)
    half = o1_ref.shape[-1]

    @pl.when(f == 0)
    def _():
        xn = (x_ref[...] + _dot(o1_ref[...].astype(BF16), wo_ref[0:half, :])
              + _dot(o2_ref[...].astype(BF16), wo_ref[half:2 * half, :]))
        xn_s[...] = xn
        h_s[...] = _rms(xn, g_ref[...]).astype(BF16)
        acc_s[...] = jnp.zeros_like(acc_s)

    h = h_s[...]
    gate = _dot(h, wg_ref[...])
    up = _dot(h, wu_ref[...])
    act = gate * jax.nn.sigmoid(gate) * up
    acc_s[...] += _dot(act.astype(BF16), wd_ref[...])

    @pl.when(f == pl.num_programs(1) - 1)
    def _():
        y = xn_s[...] + acc_s[...]
        if final_norm:
            y = _rms(y, gf_ref[...])
        out_ref[...] = y


def post(x2d, o1, o2, wo, g, wg, wu, wd, gf, *, tm, tf, final_norm):
    m, d = x2d.shape
    half = o1.shape[-1]
    dff = wg.shape[1]
    return pl.pallas_call(
        functools.partial(_post_kernel, final_norm=final_norm),
        grid=(m // tm, dff // tf),
        in_specs=[pl.BlockSpec((tm, d), lambda i, f: (i, 0)),
                  pl.BlockSpec((tm, half), lambda i, f: (i, 0)),
                  pl.BlockSpec((tm, half), lambda i, f: (i, 0)),
                  pl.BlockSpec((2 * half, d), lambda i, f: (0, 0)),
                  pl.BlockSpec((1, d), lambda i, f: (0, 0)),
                  pl.BlockSpec((d, tf), lambda i, f: (0, f)),
                  pl.BlockSpec((d, tf), lambda i, f: (0, f)),
                  pl.BlockSpec((tf, d), lambda i, f: (f, 0)),
                  pl.BlockSpec((1, d), lambda i, f: (0, 0))],
        out_specs=pl.BlockSpec((tm, d), lambda i, f: (i, 0)),
        out_shape=jax.ShapeDtypeStruct((m, d), F32),
        scratch_shapes=[pltpu.VMEM((tm, d), F32), pltpu.VMEM((tm, d), BF16), pltpu.VMEM((tm, d), F32)],
        compiler_params=_cparams(("parallel", "arbitrary")),
        name="post",
    )(x2d, o1, o2, wo, g.reshape(1, d), wg, wu, wd, gf.reshape(1, d))


def _attend_heads(q, k_tile, v_tile, bias_ref, sink_ref, valid, n_heads, write, group_heads):
    lane_lo = lax.broadcasted_iota(jnp.int32, (1, PAIR_W), 1) < HEAD_DIM
    lane_hi = jnp.logical_not(lane_lo)
    for h0 in range(0, n_heads, group_heads):
        hs = list(range(h0, min(h0 + group_heads, n_heads)))
        qh = [jnp.where(lane_lo if h % 2 == 0 else lane_hi, q[:, (h // 2) * PAIR_W:(h // 2 + 1) * PAIR_W], 0.0)
              .astype(BF16) for h in hs]
        s = [lax.dot_general(z, k_tile(h), (((1,), (1,)), ((), ())), preferred_element_type=F32) + bias_ref[h]
             for z, h in zip(qh, hs)]
        if valid is not None:
            s = [jnp.where(valid, z, NEG_INF) for z in s]
        m = [jnp.max(z, axis=-1, keepdims=True) for z in s]
        if sink_ref is not None:
            sk = [sink_ref[h:h + 1, :] for h in hs]
            m = [jnp.maximum(z, k_) for z, k_ in zip(m, sk)]
        e = [jnp.exp(z - m_) for z, m_ in zip(s, m)]
        den = [jnp.sum(z, axis=-1, keepdims=True) for z in e]
        if sink_ref is not None:
            den = [z + jnp.exp(k_ - m_) for z, k_, m_ in zip(den, sk, m)]
        o = [_dot(z.astype(BF16), v_tile(h)) / d_ for z, h, d_ in zip(e, hs, den)]
        for i in range(0, len(hs), 2):
            write(hs[i] // 2, jnp.where(lane_lo, o[i], o[i + 1]))


def _kv_tile_fns(k_plain, v_plain, k_swap, v_swap, group):
    if group == 1:
        tile = lambda src, h: src[:, (h // 2) * PAIR_W:(h // 2 + 1) * PAIR_W]
        return (lambda h: tile(k_plain, h)), (lambda h: tile(v_plain, h))
    pick = lambda plain, swap, h: (plain if h // group == h % 2 else swap)[...]
    return (lambda h: pick(k_plain, k_swap, h)), (lambda h: pick(v_plain, v_swap, h))


def _attn_prompt_kernel(*refs, n_left, tq, n_heads, group, has_sink):
    q_ref, k_ref, v_ref, bias_ref = refs[:4]
    sink_ref = refs[4] if has_sink else None
    o_ref = refs[5] if has_sink else refs[4]
    wins = refs[6:] if has_sink else refs[5:]
    hist = n_left * CHUNK
    nk = hist + tq
    i = pl.program_id(1)

    @pl.when(i == 0)
    def _():
        for w in wins:
            w[0:hist, :] = jnp.zeros((hist, w.shape[1]), BF16)

    k_new = k_ref[0]
    v_new = v_ref[0]
    wins[0][hist:nk, :] = k_new.astype(BF16)
    wins[1][hist:nk, :] = v_new.astype(BF16)
    if group > 1:
        wins[2][hist:nk, :] = pltpu.roll(k_new, HEAD_DIM, axis=1).astype(BF16)
        wins[3][hist:nk, :] = pltpu.roll(v_new, HEAD_DIM, axis=1).astype(BF16)
    valid = lax.broadcasted_iota(jnp.int32, (tq, nk), 1) >= hist - i * tq
    k_tile, v_tile = _kv_tile_fns(wins[0], wins[1], wins[2] if group > 1 else None, wins[3] if group > 1 else None,
                                  group)

    def write(pair, val):
        o_ref[0, :, pair * PAIR_W:(pair + 1) * PAIR_W] = val

    _attend_heads(q_ref[0] * (HEAD_DIM ** -0.5), k_tile, v_tile, bias_ref, sink_ref, valid, n_heads, write,
                  ATTN_PROMPT_HEAD_GROUP if group == 1 else 2)
    for w in wins:
        w[0:hist, :] = w[tq:nk, :]


def _attn_sample_kernel(*refs, n_heads, group, has_sink, bb):
    q_ref, kc_ref, vc_ref, kn_ref, vn_ref, bias_ref = refs[:6]
    sink_ref = refs[6] if has_sink else None
    o_ref = refs[-1]
    for bi in range(bb):
        k_all = jnp.concatenate([kc_ref[bi], kn_ref[bi]], axis=0)
        v_all = jnp.concatenate([vc_ref[bi], vn_ref[bi]], axis=0)
        k_sw = v_sw = None
        if group > 1:
            k_sw = pltpu.roll(k_all, HEAD_DIM, axis=1).astype(BF16)
            v_sw = pltpu.roll(v_all, HEAD_DIM, axis=1).astype(BF16)
        k_tile, v_tile = _kv_tile_fns(k_all.astype(BF16), v_all.astype(BF16), k_sw, v_sw, group)

        def write(pair, val, bi=bi):
            o_ref[bi, :, pair * PAIR_W:(pair + 1) * PAIR_W] = val

        _attend_heads(q_ref[bi] * (HEAD_DIM ** -0.5), k_tile, v_tile, bias_ref, sink_ref, None, n_heads, write,
                      n_heads)


def banded_attention(p, kv_srcs, bias, sink, *, n_left, n_heads, group, kcol, vcol):
    b, t, _ = p.shape
    kvw = (n_heads // group) * HEAD_DIM
    qw = n_heads * HEAD_DIM
    hist = n_left * CHUNK
    has_sink = sink is not None
    assert group == 1 or kvw == PAIR_W, "grouped-query path assumes two kv heads sharing one lane tile"
    sink_ops = [sink.reshape(n_heads, 1)] if has_sink else []
    if kv_srcs is None:
        tq = min(ATTN_TQ, t)
        nk = hist + tq
        bias_full = jnp.concatenate(
            [jnp.pad(bias, ((0, 0), (0, 0), (qc * CHUNK, nk - qc * CHUNK - bias.shape[2])), constant_values=NEG_INF)
             for qc in range(tq // CHUNK)], axis=1)
        sink_specs = [pl.BlockSpec((n_heads, 1), lambda bi, i: (0, 0))] if has_sink else []
        return pl.pallas_call(
            functools.partial(_attn_prompt_kernel, n_left=n_left, tq=tq, n_heads=n_heads, group=group,
                              has_sink=has_sink),
            grid=(b, t // tq),
            in_specs=[pl.BlockSpec((1, tq, qw), lambda bi, i: (bi, i, 0)),
                      pl.BlockSpec((1, tq, kvw), lambda bi, i: (bi, i, kcol)),
                      pl.BlockSpec((1, tq, kvw), lambda bi, i: (bi, i, vcol)),
                      pl.BlockSpec(bias_full.shape, lambda bi, i: (0, 0, 0))] + sink_specs,
            out_specs=pl.BlockSpec((1, tq, qw), lambda bi, i: (bi, i, 0)),
            out_shape=jax.ShapeDtypeStruct((b, t, qw), F32),
            scratch_shapes=[pltpu.VMEM((nk, kvw), BF16)] * (4 if group > 1 else 2),
            compiler_params=_cparams(("parallel", "arbitrary")),
            name="attention_prompt",
        )(p, p, p, bias_full, *sink_ops)
    bb = ATTN_SAMPLE_BB if b % ATTN_SAMPLE_BB == 0 else 1
    sink_specs = [pl.BlockSpec((n_heads, 1), lambda i: (0, 0))] if has_sink else []
    return pl.pallas_call(
        functools.partial(_attn_sample_kernel, n_heads=n_heads, group=group, has_sink=has_sink, bb=bb),
        grid=(b // bb,),
        in_specs=[pl.BlockSpec((bb, t, qw), lambda i: (i, 0, 0)),
                  pl.BlockSpec((bb, hist, kvw), lambda i: (i, 0, 0)),
                  pl.BlockSpec((bb, hist, kvw), lambda i: (i, 0, 0)),
                  pl.BlockSpec((bb, t, kvw), lambda i: (i, 0, kcol)),
                  pl.BlockSpec((bb, t, kvw), lambda i: (i, 0, vcol)),
                  pl.BlockSpec(bias.shape, lambda i: (0, 0, 0))] + sink_specs,
        out_specs=pl.BlockSpec((bb, t, qw), lambda i: (i, 0, 0)),
        out_shape=jax.ShapeDtypeStruct((b, t, qw), F32),
        compiler_params=_cparams(("parallel",)),
        name="attention_sample",
    )(p, kv_srcs[0], kv_srcs[1], p, p, bias, *sink_ops)


def _conv_kernel(*refs, tq, from_state):
    if from_state:
        a_ref, gt_ref, st_in_ref, dw_ref, db_ref, lg_ref, lb_ref, o_ref, st_ref, buf = refs
    else:
        a_ref, gt_ref, pa_ref, pg_ref, dw_ref, db_ref, lg_ref, lb_ref, o_ref, st_ref, buf = refs
    keep = CONV_W - 1
    pad = CONV_HALO - keep
    if from_state:
        buf[0:pad, :] = jnp.zeros((pad, buf.shape[1]), F32)
        buf[pad:CONV_HALO, :] = st_in_ref[0]
    else:
        prev = pa_ref[0] * jax.nn.sigmoid(pg_ref[0])
        buf[0:CONV_HALO, :] = jnp.where(pl.program_id(1) > 0, prev, 0.0)
    buf[CONV_HALO:CONV_HALO + tq, :] = a_ref[0] * jax.nn.sigmoid(gt_ref[0])
    rs = min(tq, CONV_ROWS)
    for r0 in range(0, tq, rs):
        z = jnp.zeros((rs, buf.shape[1]), F32) + db_ref[...]
        base = buf[r0:r0 + rs + CONV_HALO, :]
        for s in range(8):
            rolled = pltpu.roll(base, rs + CONV_HALO - (pad + s), axis=0)
            for j, w in enumerate(range(s, CONV_W, 8)):
                z = z + rolled[8 * j:8 * j + rs, :] * dw_ref[w:w + 1, :]
        mean = jnp.mean(z, axis=-1, keepdims=True)
        zc = z - mean
        var = jnp.mean(zc * zc, axis=-1, keepdims=True)
        zn = zc * lax.rsqrt(var + LN_EPS) * lg_ref[...] + lb_ref[...]
        o_ref[0, r0:r0 + rs, :] = zn * jax.nn.sigmoid(zn)
    st_ref[0] = buf[tq + pad:tq + CONV_HALO, :]


def conv_module(p, state, dw_w, dw_b, ln_g, ln_b, *, acol, gcol, tq):
    b, t, _ = p.shape
    cd = dw_w.shape[1]
    keep = CONV_W - 1
    from_state = state is not None
    operands = [p, p]
    in_specs = [pl.BlockSpec((1, tq, cd), lambda bi, i: (bi, i, acol)),
                pl.BlockSpec((1, tq, cd), lambda bi, i: (bi, i, gcol))]
    if from_state:
        operands.append(state)
        in_specs.append(pl.BlockSpec((1, keep, cd), lambda bi, i: (bi, 0, 0)))
    else:
        r = tq // CONV_HALO
        for col in (acol, gcol):
            operands.append(p)
            in_specs.append(pl.BlockSpec((1, CONV_HALO, cd),
                                         lambda bi, i, col=col: (bi, jnp.maximum(i * r - 1, 0), col)))
    operands += [dw_w, dw_b.reshape(1, cd), ln_g.reshape(1, cd), ln_b.reshape(1, cd)]
    in_specs += [pl.BlockSpec((CONV_W, cd), lambda bi, i: (0, 0))] + [pl.BlockSpec((1, cd), lambda bi, i: (0, 0))] * 3
    return pl.pallas_call(
        functools.partial(_conv_kernel, tq=tq, from_state=from_state),
        grid=(b, t // tq),
        in_specs=in_specs,
        out_specs=[pl.BlockSpec((1, tq, cd), lambda bi, i: (bi, i, 0)),
                   pl.BlockSpec((1, keep, cd), lambda bi, i: (bi, 0, 0))],
        out_shape=[jax.ShapeDtypeStruct((b, t, cd), F32), jax.ShapeDtypeStruct((b, keep, cd), F32)],
        scratch_shapes=[pltpu.VMEM((CONV_HALO + tq, cd), F32)],
        compiler_params=_cparams(("parallel", "arbitrary")),
        name="conv_module",
    )(*operands)


PV_MU_R, PV_MU_K, PV_MU_V, PV_W0, PV_A0, PV_KK, PV_KA, PV_V0, PV_LNW, PV_LNB, PV_RK = range(11)
PV_ROWS = 16
RWKV_BATCH_BLOCK = 2


def _shift_rows(x, first_row):
    rolled = pltpu.roll(x, 1, axis=0)
    row = lax.broadcasted_iota(jnp.int32, x.shape, 0)
    return jnp.where(row == 0, first_row, rolled)


def _rwkv_kernel(*refs, n_heads, has_vres, bb):
    (xr_ref, xk_ref, xv_ref, xl_ref, shr_ref, shk_ref, shv_ref, shl_ref, s0_ref, pv_ref, mul_ref,
     w2_ref, a2_ref, g2_ref) = refs[:14]
    pos = 14
    if has_vres:
        v1_ref, v2_ref, vf_ref = refs[pos:pos + 3]
        pos += 3
    ob_ref, sout_ref = refs[pos:pos + 2]
    pos += 2
    if not has_vres:
        vf_out_ref = refs[pos]
        pos += 1
    s_s, lr_s, lk_s, lv_s, ll_s = refs[pos:pos + 5]
    L = CHUNK
    c = pl.program_id(1)

    n_pairs = n_heads // 2
    PW = 2 * HEAD_DIM

    @pl.when(c == 0)
    def _():
        zero = jnp.zeros((HEAD_DIM, HEAD_DIM), F32)
        for bi in range(bb):
            for p in range(n_pairs):
                top = jnp.concatenate([s0_ref[bi, 2 * p], zero], axis=1)
                bottom = jnp.concatenate([zero, s0_ref[bi, 2 * p + 1]], axis=1)
                s_s[bi * n_pairs + p] = jnp.concatenate([top, bottom], axis=0)
            lr_s[bi, 0:1, :] = shr_ref[bi]
            lk_s[bi, 0:1, :] = shk_ref[bi]
            lv_s[bi, 0:1, :] = shv_ref[bi]
            ll_s[bi, 0:1, :] = shl_ref[bi]

    def token_shift(x_ref, last_s, mu, bi):
        x = x_ref[bi]
        prev = _shift_rows(x, last_s[bi, 0:1, :])
        last_s[bi, 0:1, :] = x[L - 1:L, :]
        return x + (prev - x) * mu

    pv = lambda i: pv_ref[i:i + 1, :]
    row = lax.broadcasted_iota(jnp.int32, (L, L), 0)
    col = lax.broadcasted_iota(jnp.int32, (L, L), 1)
    tri_incl = (row >= col).astype(F32)

    full = []
    for bi in range(bb):
        r = token_shift(xr_ref, lr_s, pv(PV_MU_R), bi)
        k = token_shift(xk_ref, lk_s, pv(PV_MU_K), bi)
        v = token_shift(xv_ref, lv_s, pv(PV_MU_V), bi)
        lo = token_shift(xl_ref, ll_s, mul_ref[...], bi)

        zw = pv(PV_W0) + _dot(jnp.tanh(lo).astype(BF16), w2_ref[...])
        w = -(jnp.maximum(-zw, 0.0) + jnp.log(1.0 + jnp.exp(-jnp.abs(zw)))) - 0.5
        d = -jnp.exp(w)
        iclr = jax.nn.sigmoid(pv(PV_A0) + _dot(lo.astype(BF16), a2_ref[...]))
        g = _dot(jax.nn.sigmoid(lo).astype(BF16), g2_ref[...])
        if has_vres:
            mix = jax.nn.sigmoid(pv(PV_V0) + _dot(_dot(v.astype(BF16), v1_ref[...]).astype(BF16), v2_ref[...]))
            v = v + (vf_ref[bi] - v) * mix
        else:
            vf_out_ref[bi] = v
        cs = _dot_hp(tri_incl, d)
        cprev = cs - d
        c_last = cs[L - 1:L, :]
        c_mid = cs[L // 2:L // 2 + 1, :]
        full.append(dict(
            r=r, v=v, g=g, iclr=iclr, kks=k * pv(PV_KK), kh=k * (1.0 + (iclr - 1.0) * pv(PV_KA)),
            e_start_prev=jnp.exp(cprev),
            e_start_incl=jnp.exp(cs),
            e_mid_prev=jnp.exp(cprev - c_mid), e_mid_incl=jnp.exp(cs - c_mid), e_from_mid=jnp.exp(c_mid - cs),
            e_to_end=jnp.exp(c_last - cs),
            e_chunk=jnp.exp(c_last)))

    lane_lo = lax.broadcasted_iota(jnp.int32, (1, PW), 1) < HEAD_DIM
    r2 = lax.broadcasted_iota(jnp.int32, (2 * L, 2 * L), 0)
    c2 = lax.broadcasted_iota(jnp.int32, (2 * L, 2 * L), 1)
    same_head = (r2 >= L) == (c2 >= L)
    strict_bd = same_head & (jnp.bitwise_and(r2, L - 1) > jnp.bitwise_and(c2, L - 1))
    eye_bd = (r2 == c2).astype(F32)
    incl_c = (lax.broadcasted_iota(jnp.int32, (L, 2 * L), 0)
              >= jnp.bitwise_and(lax.broadcasted_iota(jnp.int32, (L, 2 * L), 1), L - 1))

    def bd(x):
        return jnp.concatenate([jnp.where(lane_lo, x, 0.0), jnp.where(lane_lo, 0.0, x)], axis=0)

    def tile2(x):
        return jnp.concatenate([x, x], axis=0)

    def head_sum(x):
        lo_sum = jnp.sum(jnp.where(lane_lo, x, 0.0), axis=-1, keepdims=True)
        hi_sum = jnp.sum(jnp.where(lane_lo, 0.0, x), axis=-1, keepdims=True)
        return jnp.where(lane_lo, lo_sum, hi_sum)

    bf = lambda z: z.astype(BF16)
    dot_nt = lambda p_, q_: lax.dot_general(p_, q_, (((1,), (1,)), ((), ())), preferred_element_type=F32)
    dot_tn = lambda p_, q_: lax.dot_general(p_, q_, (((0,), (0,)), ((), ())), preferred_element_type=F32)

    units = [(bi, p) for bi in range(bb) for p in range(n_pairs)]
    sls = [slice(p * PW, (p + 1) * PW) for _, p in units]
    get = lambda name: [full[bi][name][:, sl] for (bi, _), sl in zip(units, sls)]
    mul = lambda xs_, ys_: [x_ * y_ for x_, y_ in zip(xs_, ys_)]

    kk = get("kks")
    kk = [z / jnp.maximum(jnp.sqrt(head_sum(z * z)), 1e-12) for z in kk]
    a_p = [-z for z in kk]
    b_p = mul(kk, get("iclr"))
    k_p, r_p, v_p = get("kh"), get("r"), get("v")
    a0 = mul(a_p, get("e_start_prev"))
    r0 = mul(r_p, get("e_start_incl"))
    at = mul(a_p, get("e_mid_prev"))
    rt = mul(r_p, get("e_mid_incl"))
    e_from_mid, e_to_end = get("e_from_mid"), get("e_to_end")
    bt, kt = mul(b_p, e_from_mid), mul(k_p, e_from_mid)
    bh, kh = mul(b_p, e_to_end), mul(k_p, e_to_end)

    at_bd = [bf(bd(z)) for z in at]
    rt_b = [bf(z) for z in rt]
    n_ab = [jnp.where(strict_bd, dot_nt(x_, bf(tile2(y_))), 0.0) for x_, y_ in zip(at_bd, bt)]
    n_ak = [jnp.where(strict_bd, dot_nt(x_, bf(tile2(y_))), 0.0) for x_, y_ in zip(at_bd, kt)]
    n_rb = [jnp.where(incl_c, dot_nt(x_, bf(bd(y_))), 0.0) for x_, y_ in zip(rt_b, bt)]
    n_rk = [jnp.where(incl_c, dot_nt(x_, bf(bd(y_))), 0.0) for x_, y_ in zip(rt_b, kt)]
    x = [eye_bd + z for z in n_ab]
    pw = n_ab
    for _ in range(int(math.log2(L)) - 1):
        pw_b = [bf(z) for z in pw]
        pw = [_dot(z, z) for z in pw_b]
        x = [x_ + _dot(bf(x_), bf(p_)) for x_, p_ in zip(x, pw)]
    x_b = [bf(z) for z in x]
    v_bd = [bf(bd(z)) for z in v_p]
    w1 = [_dot(x_, bf(bd(y_))) for x_, y_ in zip(x_b, a0)]
    t2 = [bf(_dot(bf(x_), y_)) for x_, y_ in zip(n_ak, v_bd)]
    w2 = [_dot(x_, y_) for x_, y_ in zip(x_b, t2)]
    s_old = [s_s[i] for i in range(len(units))]
    s_b = [bf(z) for z in s_old]
    u = [dot_nt(bf(x_), y_) + z_ for x_, y_, z_ in zip(w1, s_b, w2)]
    y = [dot_nt(bf(r_), s_) + _dot(bf(nb_), bf(u_)) + _dot(bf(nk_), v_)
         for r_, s_, nb_, u_, nk_, v_ in zip(r0, s_b, n_rb, u, n_rk, v_bd)]
    upd = [dot_tn(bf(jnp.concatenate([u_[0:L] + u_[L:2 * L], v_], axis=0)), bf(jnp.concatenate([b_, k_], axis=0)))
           for u_, v_, b_, k_ in zip(u, v_p, bh, kh)]
    for i, (s_, e_, d_) in enumerate(zip(s_old, get("e_chunk"), upd)):
        s_s[i] = s_ * e_ + jnp.where(same_head, d_, 0.0)
    g_p = get("g")
    for i, ((bi, _), sl) in enumerate(zip(units, sls)):
        mean = head_sum(y[i]) * (1.0 / HEAD_DIM)
        yc = y[i] - mean
        var = head_sum(yc * yc) * (1.0 / HEAD_DIM)
        yn = yc * lax.rsqrt(var + LNX_EPS) * pv(PV_LNW)[:, sl] + pv(PV_LNB)[:, sl]
        bonus = head_sum(r_p[i] * k_p[i] * pv(PV_RK)[:, sl])
        ob_ref[bi, :, sl] = (yn + bonus * v_p[i]) * g_p[i]

    @pl.when(c == pl.num_programs(1) - 1)
    def _():
        for i, (bi, p) in enumerate(units):
            s_p = s_s[i]
            sout_ref[bi, 2 * p] = s_p[0:HEAD_DIM, 0:HEAD_DIM]
            sout_ref[bi, 2 * p + 1] = s_p[HEAD_DIM:PW, HEAD_DIM:PW]


def rwkv_mixer(p, shift0, s0, pvec, mu_l, w2p, a2p, g2p, vres, *, cols):
    b, t, _ = p.shape
    nc = t // CHUNK
    n_heads = s0.shape[1]
    cb = n_heads * HEAD_DIM
    rc, kc, vc, lc = cols
    has_vres = vres is not None
    bb = RWKV_BATCH_BLOCK if b % RWKV_BATCH_BLOCK == 0 else 1
    tok = lambda col, w: pl.BlockSpec((bb, CHUNK, w), lambda bi, c, col=col: (bi, c, col))
    const2 = lambda shape: pl.BlockSpec(shape, lambda bi, c: (0, 0))
    perb = lambda w: pl.BlockSpec((bb, 1, w), lambda bi, c: (bi, 0, 0))
    state_spec = pl.BlockSpec((bb, n_heads, HEAD_DIM, HEAD_DIM), lambda bi, c: (bi, 0, 0, 0))
    operands = [p, p, p, p, *shift0, s0, pvec, mu_l, w2p, a2p, g2p]
    in_specs = [tok(rc, cb), tok(kc, cb), tok(vc, cb), tok(lc, LORA_W),
                perb(cb), perb(cb), perb(cb), perb(LORA_W), state_spec,
                const2((PV_ROWS, cb)), const2((1, LORA_W)),
                const2((LORA_W, cb)), const2((LORA_W, cb)), const2((LORA_W, cb))]
    out_specs = [tok(0, cb), state_spec]
    out_shape = [jax.ShapeDtypeStruct((b, t, cb), F32), jax.ShapeDtypeStruct(s0.shape, F32)]
    if has_vres:
        v1p, v2p, v_first = vres
        operands += [v1p, v2p, v_first]
        in_specs += [const2(v1p.shape), const2(v2p.shape), tok(0, cb)]
    else:
        out_specs.append(tok(0, cb))
        out_shape.append(jax.ShapeDtypeStruct((b, t, cb), F32))
    outs = pl.pallas_call(
        functools.partial(_rwkv_kernel, n_heads=n_heads, has_vres=has_vres, bb=bb),
        grid=(b // bb, nc),
        in_specs=in_specs,
        out_specs=out_specs,
        out_shape=out_shape,
        scratch_shapes=[pltpu.VMEM((bb * n_heads // 2, 2 * HEAD_DIM, 2 * HEAD_DIM), F32),
                        pltpu.VMEM((bb, 8, cb), F32), pltpu.VMEM((bb, 8, cb), F32), pltpu.VMEM((bb, 8, cb), F32),
                        pltpu.VMEM((bb, 8, LORA_W), F32)],
        compiler_params=_cparams(("parallel", "arbitrary")),
        name="rwkv_mixer",
    )(*operands)
    return (outs[0], outs[1], None) if has_vres else tuple(outs)


def _band_offsets(n_left):
    return np.arange(-(CHUNK - 1), (n_left + 1) * CHUNK) - n_left * CHUNK


def _lookup_static(table, idx):
    h = table.shape[0]
    pieces = []
    i, n = 0, len(idx)
    while i < n:
        j = i + 1
        step = int(idx[j] - idx[i]) if j < n else 0
        if step in (-1, 0, 1):
            while j < n and idx[j] - idx[j - 1] == step:
                j += 1
        first, last = int(idx[i]), int(idx[j - 1])
        if first == last:
            pieces.append(jnp.broadcast_to(table[:, first:first + 1], (h, j - i)))
        elif first < last:
            pieces.append(table[:, first:last + 1])
        else:
            pieces.append(jnp.flip(table[:, last:first + 1], axis=1))
        i = j
    return jnp.concatenate(pieces, axis=1)


def _toeplitz(e, n_rows, n_cols):
    h, n = e.shape
    z = jnp.concatenate([e, jnp.zeros((h, 1), e.dtype)], axis=1)
    shifted = jnp.tile(z, (1, n_rows))[:, :n_rows * n].reshape(h, n_rows, n)
    return shifted[:, :, n_rows - 1:n_rows - 1 + n_cols]


def _t5_bucket(rel):
    nb = T5_BUCKETS // 2
    exact = nb // 2
    n = np.abs(rel)
    nf = np.maximum(n, exact).astype(np.float32)
    large = exact + (np.log(nf / exact) / math.log(T5_MAX_DIST / exact) * (nb - exact)).astype(np.int32)
    return np.where(rel > 0, nb, 0) + np.where(n < exact, n, np.minimum(large, nb - 1))


def _bias_a(t5_table):
    e = _lookup_static(t5_table.T.astype(F32), _t5_bucket(_band_offsets(A_LEFT)))
    return _toeplitz(e, CHUNK, (A_LEFT + 1) * CHUNK)


def _bias_c(rel_table):
    idx = np.clip(-_band_offsets(C_LEFT), -REL_CLIP, REL_CLIP) + REL_CLIP
    return _toeplitz(_lookup_static(rel_table.astype(F32), idx), CHUNK, (C_LEFT + 1) * CHUNK)


def _pad_rows(w, start, total):
    return jnp.zeros((total, w.shape[1]), w.dtype).at[start:start + w.shape[0]].set(w)


def kernel(x_prompt, x_sample, cache_a_k, cache_a_v, state_b_wkv, state_b_shift, cache_c_k, cache_c_v, state_d_conv, norm_mix_g, norm_ffn_g, norm_final_g, t5_table, w_in_e, w_out_e, a_sink, b_mu, b_w0, b_w2, b_a0, b_a2, b_g2, b_kk, b_ka, b_rk, b_lnx_w, b_lnx_b, b_v0, b_v1, b_v2, w_in_o, w_out_o, c_rel_table, d_dw_w, d_dw_b, d_ln_g, d_ln_b, ffn_w_gate, ffn_w_up, ffn_w_down):
    depth, d_model = norm_mix_g.shape
    n_even = w_in_e.shape[0]
    h_a = a_sink.shape[1]
    g_a = h_a // KVH_A
    qa = h_a * HEAD_DIM
    kva = KVH_A * HEAD_DIM
    h_b = state_b_wkv.shape[2]
    cb = h_b * HEAD_DIM
    h_c = c_rel_table.shape[1]
    qc = h_c * HEAD_DIM
    cd = d_dw_w.shape[2]
    d_ff = ffn_w_gate.shape[2]
    tf = d_ff // 2 if (d_ff // 2) % 128 == 0 else d_ff

    groups = [x_prompt, x_sample]
    dims = [x.shape[:2] for x in groups]
    xs = [x.reshape(-1, d_model) for x in groups]
    tms = [min(512, x.shape[0]) for x in xs]
    conv_tq = [min(512, t) for _, t in dims]

    o_q, o_k, o_v, o_pb = 0, qa, qa + kva, qa + 2 * kva
    o_r, o_wd, o_kb, o_vb = o_pb, o_pb + cb, o_pb + cb + DECAY_LORA, o_pb + 2 * cb + DECAY_LORA
    o_ad = o_vb + cb
    o_gd = o_ad + ICLR_LORA
    perm_ranges = [(o_q, qa), (o_r, cb), (o_kb, cb), (o_vb, cb), (o_k, kva), (o_v, kva),
                   (o_wd, DECAY_LORA), (o_ad, ICLR_LORA), (o_gd, GATE_LORA)]
    n_r, n_kb, n_vb = qa, qa + cb, qa + 2 * cb
    n_ka = qa + 3 * cb
    n_va = n_ka + kva
    n_lo = n_va + kva
    pb_pieces = [[(o_r, cb)], [(o_kb, cb)], [(o_vb, cb)], [(o_wd, DECAY_LORA), (o_ad, ICLR_LORA), (o_gd, GATE_LORA)]]

    def split_pb(z):
        return [jnp.concatenate([z[..., a - o_pb:a - o_pb + n] for a, n in piece], axis=-1) for piece in pb_pieces]

    def join_pb(r_, k_, v_, lo_):
        return jnp.concatenate([r_, lo_[..., :DECAY_LORA], k_, v_, lo_[..., DECAY_LORA:]], axis=-1)

    bias_a = _bias_a(t5_table)
    st = [[[] for _ in range(7)] for _ in range(2)]
    v_first = [None, None]
    for i in range(depth):
        j = i // 2
        last = i == depth - 1
        if i % 2 == 0:
            w_in = jnp.concatenate([w_in_e[j][:, a:a + n] for a, n in perm_ranges], axis=1).astype(BF16)
            wo = w_out_e[j].astype(BF16)
            mu_r, mu_k, mu_v, mu_lo = split_pb(b_mu[j])
            pvec = jnp.zeros((PV_ROWS, cb), F32)
            rows = {PV_MU_R: mu_r, PV_MU_K: mu_k, PV_MU_V: mu_v, PV_W0: b_w0[j],
                    PV_A0: b_a0[j], PV_KK: b_kk[j], PV_KA: b_ka[j], PV_LNW: b_lnx_w[j], PV_LNB: b_lnx_b[j],
                    PV_RK: b_rk[j].reshape(cb)}
            if j > 0:
                rows[PV_V0] = b_v0[j - 1]
            for ri, val in rows.items():
                pvec = pvec.at[ri].set(val)
            mu_l = mu_lo.reshape(1, LORA_W)
            w2p = _pad_rows(b_w2[j], 0, LORA_W).astype(BF16)
            a2p = _pad_rows(b_a2[j], DECAY_LORA, LORA_W).astype(BF16)
            g2p = _pad_rows(b_g2[j], DECAY_LORA + ICLR_LORA, LORA_W).astype(BF16)
            if j > 0:
                lora_v = b_v1.shape[2]
                v1p = jnp.zeros((cb, 128), F32).at[:, :lora_v].set(b_v1[j - 1]).astype(BF16)
                v2p = _pad_rows(b_v2[j - 1], 0, 128).astype(BF16)
            bias = bias_a
        else:
            w_in = w_in_o[j].astype(BF16)
            wo = w_out_o[j].astype(BF16)
            bias = _bias_c(c_rel_table[j])
        wg = ffn_w_gate[i].astype(BF16)
        wu = ffn_w_up[i].astype(BF16)
        wd = ffn_w_down[i].astype(BF16)
        for gi in range(2):
            b, t = dims[gi]
            p = proj_in(xs[gi], norm_mix_g[i], w_in, tms[gi]).reshape(b, t, -1)
            if i % 2 == 0:
                if gi == 0:
                    kv_srcs = None
                    shift0 = [jnp.zeros((b, 1, w_), F32) for w_ in (cb, cb, cb, LORA_W)]
                    s0 = jnp.zeros((b, h_b, HEAD_DIM, HEAD_DIM), F32)
                else:
                    kv_srcs = (cache_a_k[j].reshape(b, -1, kva), cache_a_v[j].reshape(b, -1, kva))
                    shift0 = [z[:, None, :] for z in split_pb(state_b_shift[j])]
                    s0 = state_b_wkv[j]
                o1 = banded_attention(p, kv_srcs, bias, a_sink[j], n_left=A_LEFT, n_heads=h_a, group=g_a,
                                      kcol=n_ka // kva, vcol=n_va // kva)
                vres = None if j == 0 else (v1p, v2p, v_first[gi])
                o2, wkv, vf = rwkv_mixer(p, shift0, s0, pvec, mu_l, w2p, a2p, g2p, vres,
                                         cols=(n_r // cb, n_kb // cb, n_vb // cb, n_lo // LORA_W))
                if j == 0:
                    v_first[gi] = vf
                keep = min(A_LEFT * CHUNK, t)
                nk = p[:, t - keep:, n_ka:n_ka + kva].reshape(b, keep, KVH_A, HEAD_DIM)
                nv = p[:, t - keep:, n_va:n_va + kva].reshape(b, keep, KVH_A, HEAD_DIM)
                nshift = join_pb(*[p[:, t - 1, c0:c0 + w_] for c0, w_ in ((n_r, cb), (n_kb, cb), (n_vb, cb), (n_lo, LORA_W))])
                for s_list, val in zip(st[gi][0:4], (nk, nv, wkv, nshift)):
                    s_list.append(val)
            else:
                if gi == 0:
                    kv_srcs = None
                    conv_prev = None
                else:
                    kv_srcs = (cache_c_k[j].reshape(b, -1, qc), cache_c_v[j].reshape(b, -1, qc))
                    conv_prev = state_d_conv[j]
                o1 = banded_attention(p, kv_srcs, bias, None, n_left=C_LEFT, n_heads=h_c, group=1,
                                      kcol=1, vcol=2)
                o2, nconv = conv_module(p, conv_prev, d_dw_w[j], d_dw_b[j], d_ln_g[j], d_ln_b[j],
                                        acol=3 * qc // cd, gcol=3 * qc // cd + 1, tq=conv_tq[gi])
                keep = min(C_LEFT * CHUNK, t)
                nk = p[:, t - keep:, qc:2 * qc].reshape(b, keep, h_c, HEAD_DIM)
                nv = p[:, t - keep:, 2 * qc:3 * qc].reshape(b, keep, h_c, HEAD_DIM)
                for s_list, val in zip(st[gi][4:7], (nk, nv, nconv)):
                    s_list.append(val)
            half = o1.shape[-1]
            xs[gi] = post(xs[gi], o1.reshape(-1, half), o2.reshape(-1, half), wo, norm_ffn_g[i], wg, wu, wd,
                          norm_final_g, tm=tms[gi], tf=tf, final_norm=last)
    y_prompt = xs[0].reshape(x_prompt.shape)
    y_sample = xs[1].reshape(x_sample.shape)
    (pak, pav, pbw, pbs, pck, pcv, pdc), (sak, sav, sbw, sbs, sck, scv, sdc) = [[jnp.stack(s) for s in g] for g in st]
    return (y_prompt, y_sample, pak, pav, pbw, pbs, pck, pcv, pdc, sak, sav, sbw, sbs, sck, scv, sdc)
```

```python
import functools
import math

import jax
import jax.numpy as jnp
import numpy as np
from jax import lax
from jax.experimental import pallas as pl
from jax.experimental.pallas import tpu as pltpu

F32 = jnp.float32
BF16 = jnp.bfloat16
HIGHEST = lax.Precision.HIGHEST

CHUNK = 64
HEAD_DIM = 64
RMS_EPS = 1e-6
NEG_INF = -1e30
LNX_EPS = 64e-5
LN_EPS = 1e-5
CONV_W = 31
T5_BUCKETS = 32
T5_MAX_DIST = 128
REL_CLIP = 128
A_LEFT = 2
C_LEFT = 8
KVH_A = 2
DECAY_LORA = 64
ICLR_LORA = 64
GATE_LORA = 128
LORA_W = DECAY_LORA + ICLR_LORA + GATE_LORA
CONV_HALO = 32
CONV_ROWS = 64
POST_ROWS = 256
PAIR_W = 2 * HEAD_DIM
ATTN_TQ = 256
ATTN_SAMPLE_BB = 2
ATTN_PROMPT_HEAD_GROUP = 4

VMEM_LIMIT = 56 * 1024 * 1024


def _cparams(sem):
    return pltpu.CompilerParams(dimension_semantics=sem, vmem_limit_bytes=VMEM_LIMIT)


def _rms(x, g):
    return x * lax.rsqrt(jnp.mean(x * x, axis=-1, keepdims=True) + RMS_EPS) * g


def _dot(a, b):
    return jnp.dot(a, b, preferred_element_type=F32)


def _dot_hp(a, b):
    return jnp.dot(a, b, precision=HIGHEST, preferred_element_type=F32)


def _dot_nt_hp(a, b):
    return lax.dot_general(a, b, (((1,), (1,)), ((), ())), precision=HIGHEST, preferred_element_type=F32)


def _dot_tn_hp(a, b):
    return lax.dot_general(a, b, (((0,), (0,)), ((), ())), precision=HIGHEST, preferred_element_type=F32)


def _proj_in_kernel(x_ref, g_ref, w_ref, o_ref):
    h = _rms(x_ref[...], g_ref[...]).astype(BF16)
    o_ref[...] = _dot(h, w_ref[...])


def proj_in(x2d, g, w_bf16, tm):
    m, d = x2d.shape
    n = w_bf16.shape[1]
    return pl.pallas_call(
        _proj_in_kernel,
        grid=(m // tm,),
        in_specs=[pl.BlockSpec((tm, d), lambda i: (i, 0)),
                  pl.BlockSpec((1, d), lambda i: (0, 0)),
                  pl.BlockSpec((d, n), lambda i: (0, 0))],
        out_specs=pl.BlockSpec((tm, n), lambda i: (i, 0)),
        out_shape=jax.ShapeDtypeStruct((m, n), F32),
        compiler_params=_cparams(("parallel",)),
        name="proj_in",
    )(x2d, g.reshape(1, d), w_bf16)


def _post_kernel(x_ref, o1_ref, o2_ref, wo_ref, g_ref, wg_ref, wu_ref, wd_ref, gf_ref, out_ref, h_s, *, final_norm):
    f = pl.program_id(1)
    half = o1_ref.shape[-1]
    tm = h_s.shape[0]
    rs = min(tm, POST_ROWS)

    @pl.when(f == 0)
    def _():
        for r0 in range(0, tm, rs):
            rows = slice(r0, r0 + rs)
            xn = (x_ref[rows, :] + _dot(o1_ref[rows, :], wo_ref[0:half, :])
                  + _dot(o2_ref[rows, :], wo_ref[half:2 * half, :]))
            out_ref[rows, :] = xn
            h_s[rows, :] = _rms(xn, g_ref[...]).astype(BF16)

    for r0 in range(0, tm, rs):
        rows = slice(r0, r0 + rs)
        h = h_s[rows, :]
        gate = _dot(h, wg_ref[...])
        up = _dot(h, wu_ref[...])
        act = gate * jax.nn.sigmoid(gate) * up
        out_ref[rows, :] += _dot(act.astype(BF16), wd_ref[...])

    if final_norm:
        @pl.when(f == pl.num_programs(1) - 1)
        def _():
            for r0 in range(0, tm, rs):
                rows = slice(r0, r0 + rs)
                out_ref[rows, :] = _rms(out_ref[rows, :], gf_ref[...])


def post(x2d, o1, o2, wo, g, wg, wu, wd, gf, layer, *, tm, tf, final_norm):
    m, d = x2d.shape
    half = o1.shape[-1]
    dff = wg.shape[2]
    return pl.pallas_call(
        functools.partial(_post_kernel, final_norm=final_norm),
        grid=(m // tm, dff // tf),
        in_specs=[pl.BlockSpec((tm, d), lambda i, f: (i, 0)),
                  pl.BlockSpec((tm, half), lambda i, f: (i, 0)),
                  pl.BlockSpec((tm, half), lambda i, f: (i, 0)),
                  pl.BlockSpec((2 * half, d), lambda i, f: (0, 0)),
                  pl.BlockSpec((1, d), lambda i, f: (0, 0)),
                  pl.BlockSpec((None, d, tf), lambda i, f: (layer, 0, f)),
                  pl.BlockSpec((None, d, tf), lambda i, f: (layer, 0, f)),
                  pl.BlockSpec((None, tf, d), lambda i, f: (layer, f, 0)),
                  pl.BlockSpec((1, d), lambda i, f: (0, 0))],
        out_specs=pl.BlockSpec((tm, d), lambda i, f: (i, 0)),
        out_shape=jax.ShapeDtypeStruct((m, d), F32),
        scratch_shapes=[pltpu.VMEM((tm, d), BF16)],
        compiler_params=_cparams(("parallel", "arbitrary")),
        name="post",
    )(x2d, o1, o2, wo, g.reshape(1, d), wg, wu, wd, gf.reshape(1, d))


def _attend_heads(q, k_tile, v_tile, bias_ref, sink_ref, valid, n_heads, write, group_heads):
    lane_lo = lax.broadcasted_iota(jnp.int32, (1, PAIR_W), 1) < HEAD_DIM
    lane_hi = jnp.logical_not(lane_lo)
    for h0 in range(0, n_heads, group_heads):
        hs = list(range(h0, min(h0 + group_heads, n_heads)))
        qh = [jnp.where(lane_lo if h % 2 == 0 else lane_hi, q[:, (h // 2) * PAIR_W:(h // 2 + 1) * PAIR_W], 0.0)
              .astype(BF16) for h in hs]
        s = [lax.dot_general(z, k_tile(h), (((1,), (1,)), ((), ())), preferred_element_type=F32) + bias_ref[h]
             for z, h in zip(qh, hs)]
        if valid is not None:
            s = [jnp.where(valid, z, NEG_INF) for z in s]
        m = [jnp.max(z, axis=-1, keepdims=True) for z in s]
        if sink_ref is not None:
            sk = [sink_ref[h:h + 1, :] for h in hs]
            m = [jnp.maximum(z, k_) for z, k_ in zip(m, sk)]
        e = [jnp.exp(z - m_) for z, m_ in zip(s, m)]
        den = [jnp.sum(z, axis=-1, keepdims=True) for z in e]
        if sink_ref is not None:
            den = [z + jnp.exp(k_ - m_) for z, k_, m_ in zip(den, sk, m)]
        o = [_dot(z.astype(BF16), v_tile(h)) / d_ for z, h, d_ in zip(e, hs, den)]
        for i in range(0, len(hs), 2):
            write(hs[i] // 2, jnp.where(lane_lo, o[i], o[i + 1]))


def _kv_tile_fns(k_plain, v_plain, k_swap, v_swap, group):
    if group == 1:
        tile = lambda src, h: src[:, (h // 2) * PAIR_W:(h // 2 + 1) * PAIR_W]
        return (lambda h: tile(k_plain, h)), (lambda h: tile(v_plain, h))
    pick = lambda plain, swap, h: (plain if h // group == h % 2 else swap)[...]
    return (lambda h: pick(k_plain, k_swap, h)), (lambda h: pick(v_plain, v_swap, h))


def _attn_prompt_kernel(*refs, n_left, tq, n_heads, group, has_sink):
    q_ref, k_ref, v_ref, bias_ref = refs[:4]
    sink_ref = refs[4] if has_sink else None
    o_ref = refs[5] if has_sink else refs[4]
    wins = refs[6:] if has_sink else refs[5:]
    hist = n_left * CHUNK
    nk = hist + tq
    i = pl.program_id(1)

    @pl.when(i == 0)
    def _():
        for w in wins:
            w[0:hist, :] = jnp.zeros((hist, w.shape[1]), BF16)

    k_new = k_ref[0]
    v_new = v_ref[0]
    wins[0][hist:nk, :] = k_new.astype(BF16)
    wins[1][hist:nk, :] = v_new.astype(BF16)
    if group > 1:
        wins[2][hist:nk, :] = pltpu.roll(k_new, HEAD_DIM, axis=1).astype(BF16)
        wins[3][hist:nk, :] = pltpu.roll(v_new, HEAD_DIM, axis=1).astype(BF16)
    valid = lax.broadcasted_iota(jnp.int32, (tq, nk), 1) >= hist - i * tq
    k_tile, v_tile = _kv_tile_fns(wins[0], wins[1], wins[2] if group > 1 else None, wins[3] if group > 1 else None,
                                  group)

    def write(pair, val):
        o_ref[0, :, pair * PAIR_W:(pair + 1) * PAIR_W] = val.astype(o_ref.dtype)

    _attend_heads(q_ref[0] * (HEAD_DIM ** -0.5), k_tile, v_tile, bias_ref, sink_ref, valid, n_heads, write,
                  ATTN_PROMPT_HEAD_GROUP if group == 1 else 8)
    for w in wins:
        w[0:hist, :] = w[tq:nk, :]


def _attn_sample_kernel(*refs, n_heads, group, has_sink, bb):
    q_ref, kc_ref, vc_ref, kn_ref, vn_ref, bias_ref = refs[:6]
    sink_ref = refs[6] if has_sink else None
    o_ref = refs[-1]
    for bi in range(bb):
        k_all = jnp.concatenate([kc_ref[bi], kn_ref[bi]], axis=0)
        v_all = jnp.concatenate([vc_ref[bi], vn_ref[bi]], axis=0)
        k_sw = v_sw = None
        if group > 1:
            k_sw = pltpu.roll(k_all, HEAD_DIM, axis=1).astype(BF16)
            v_sw = pltpu.roll(v_all, HEAD_DIM, axis=1).astype(BF16)
        k_tile, v_tile = _kv_tile_fns(k_all.astype(BF16), v_all.astype(BF16), k_sw, v_sw, group)

        def write(pair, val, bi=bi):
            o_ref[bi, :, pair * PAIR_W:(pair + 1) * PAIR_W] = val.astype(o_ref.dtype)

        _attend_heads(q_ref[bi] * (HEAD_DIM ** -0.5), k_tile, v_tile, bias_ref, sink_ref, None, n_heads, write,
                      n_heads)


def banded_attention(p, kv_srcs, bias, sink, *, n_left, n_heads, group, kcol, vcol):
    b, t, _ = p.shape
    kvw = (n_heads // group) * HEAD_DIM
    qw = n_heads * HEAD_DIM
    hist = n_left * CHUNK
    has_sink = sink is not None
    assert group == 1 or kvw == PAIR_W, "grouped-query path assumes two kv heads sharing one lane tile"
    sink_ops = [sink.reshape(n_heads, 1)] if has_sink else []
    if kv_srcs is None:
        tq = min(ATTN_TQ, t)
        nk = hist + tq
        bias_full = jnp.concatenate(
            [jnp.pad(bias, ((0, 0), (0, 0), (qc * CHUNK, nk - qc * CHUNK - bias.shape[2])), constant_values=NEG_INF)
             for qc in range(tq // CHUNK)], axis=1)
        sink_specs = [pl.BlockSpec((n_heads, 1), lambda bi, i: (0, 0))] if has_sink else []
        return pl.pallas_call(
            functools.partial(_attn_prompt_kernel, n_left=n_left, tq=tq, n_heads=n_heads, group=group,
                              has_sink=has_sink),
            grid=(b, t // tq),
            in_specs=[pl.BlockSpec((1, tq, qw), lambda bi, i: (bi, i, 0)),
                      pl.BlockSpec((1, tq, kvw), lambda bi, i: (bi, i, kcol)),
                      pl.BlockSpec((1, tq, kvw), lambda bi, i: (bi, i, vcol)),
                      pl.BlockSpec(bias_full.shape, lambda bi, i: (0, 0, 0))] + sink_specs,
            out_specs=pl.BlockSpec((1, tq, qw), lambda bi, i: (bi, i, 0)),
            out_shape=jax.ShapeDtypeStruct((b, t, qw), BF16),
            scratch_shapes=[pltpu.VMEM((nk, kvw), BF16)] * (4 if group > 1 else 2),
            compiler_params=_cparams(("parallel", "arbitrary")),
            name="attention_prompt",
        )(p, p, p, bias_full, *sink_ops)
    bb = ATTN_SAMPLE_BB if b % ATTN_SAMPLE_BB == 0 else 1
    sink_specs = [pl.BlockSpec((n_heads, 1), lambda i: (0, 0))] if has_sink else []
    return pl.pallas_call(
        functools.partial(_attn_sample_kernel, n_heads=n_heads, group=group, has_sink=has_sink, bb=bb),
        grid=(b // bb,),
        in_specs=[pl.BlockSpec((bb, t, qw), lambda i: (i, 0, 0)),
                  pl.BlockSpec((bb, hist, kvw), lambda i: (i, 0, 0)),
                  pl.BlockSpec((bb, hist, kvw), lambda i: (i, 0, 0)),
                  pl.BlockSpec((bb, t, kvw), lambda i: (i, 0, kcol)),
                  pl.BlockSpec((bb, t, kvw), lambda i: (i, 0, vcol)),
                  pl.BlockSpec(bias.shape, lambda i: (0, 0, 0))] + sink_specs,
        out_specs=pl.BlockSpec((bb, t, qw), lambda i: (i, 0, 0)),
        out_shape=jax.ShapeDtypeStruct((b, t, qw), BF16),
        compiler_params=_cparams(("parallel",)),
        name="attention_sample",
    )(p, kv_srcs[0], kv_srcs[1], p, p, bias, *sink_ops)


def _conv_kernel(*refs, tq, from_state):
    if from_state:
        a_ref, gt_ref, st_in_ref, dw_ref, db_ref, lg_ref, lb_ref, o_ref, st_ref, buf = refs
    else:
        a_ref, gt_ref, pa_ref, pg_ref, dw_ref, db_ref, lg_ref, lb_ref, o_ref, st_ref, buf = refs
    keep = CONV_W - 1
    pad = CONV_HALO - keep
    if from_state:
        buf[0:pad, :] = jnp.zeros((pad, buf.shape[1]), F32)
        buf[pad:CONV_HALO, :] = st_in_ref[0]
    else:
        prev = pa_ref[0] * jax.nn.sigmoid(pg_ref[0])
        buf[0:CONV_HALO, :] = jnp.where(pl.program_id(1) > 0, prev, 0.0)
    buf[CONV_HALO:CONV_HALO + tq, :] = a_ref[0] * jax.nn.sigmoid(gt_ref[0])
    rs = min(tq, CONV_ROWS)
    for r0 in range(0, tq, rs):
        z = jnp.zeros((rs, buf.shape[1]), F32) + db_ref[...]
        base = buf[r0:r0 + rs + CONV_HALO, :]
        for s in range(8):
            rolled = pltpu.roll(base, rs + CONV_HALO - (pad + s), axis=0)
            for j, w in enumerate(range(s, CONV_W, 8)):
                z = z + rolled[8 * j:8 * j + rs, :] * dw_ref[w:w + 1, :]
        mean = jnp.mean(z, axis=-1, keepdims=True)
        zc = z - mean
        var = jnp.mean(zc * zc, axis=-1, keepdims=True)
        zn = zc * lax.rsqrt(var + LN_EPS) * lg_ref[...] + lb_ref[...]
        o_ref[0, r0:r0 + rs, :] = (zn * jax.nn.sigmoid(zn)).astype(o_ref.dtype)
    st_ref[0] = buf[tq + pad:tq + CONV_HALO, :]


def conv_module(p, state, dw_w, dw_b, ln_g, ln_b, *, acol, gcol, tq):
    b, t, _ = p.shape
    cd = dw_w.shape[1]
    keep = CONV_W - 1
    from_state = state is not None
    operands = [p, p]
    in_specs = [pl.BlockSpec((1, tq, cd), lambda bi, i: (bi, i, acol)),
                pl.BlockSpec((1, tq, cd), lambda bi, i: (bi, i, gcol))]
    if from_state:
        operands.append(state)
        in_specs.append(pl.BlockSpec((1, keep, cd), lambda bi, i: (bi, 0, 0)))
    else:
        r = tq // CONV_HALO
        for col in (acol, gcol):
            operands.append(p)
            in_specs.append(pl.BlockSpec((1, CONV_HALO, cd),
                                         lambda bi, i, col=col: (bi, jnp.maximum(i * r - 1, 0), col)))
    operands += [dw_w, dw_b.reshape(1, cd), ln_g.reshape(1, cd), ln_b.reshape(1, cd)]
    in_specs += [pl.BlockSpec((CONV_W, cd), lambda bi, i: (0, 0))] + [pl.BlockSpec((1, cd), lambda bi, i: (0, 0))] * 3
    return pl.pallas_call(
        functools.partial(_conv_kernel, tq=tq, from_state=from_state),
        grid=(b, t // tq),
        in_specs=in_specs,
        out_specs=[pl.BlockSpec((1, tq, cd), lambda bi, i: (bi, i, 0)),
                   pl.BlockSpec((1, keep, cd), lambda bi, i: (bi, 0, 0))],
        out_shape=[jax.ShapeDtypeStruct((b, t, cd), BF16), jax.ShapeDtypeStruct((b, keep, cd), F32)],
        scratch_shapes=[pltpu.VMEM((CONV_HALO + tq, cd), F32)],
        compiler_params=_cparams(("parallel", "arbitrary")),
        name="conv_module",
    )(*operands)


PV_MU_R, PV_MU_K, PV_MU_V, PV_W0, PV_A0, PV_KK, PV_KA, PV_V0, PV_LNW, PV_LNB, PV_RK = range(11)
PV_ROWS = 16
RWKV_BATCH_BLOCK = 2


def _shift_rows(x, first_row):
    rolled = pltpu.roll(x, 1, axis=0)
    row = lax.broadcasted_iota(jnp.int32, x.shape, 0)
    return jnp.where(row == 0, first_row, rolled)


def _rwkv_kernel(*refs, n_heads, has_vres, bb):
    (xr_ref, xk_ref, xv_ref, xl_ref, shr_ref, shk_ref, shv_ref, shl_ref, s0_ref, pv_ref, mul_ref,
     w2_ref, a2_ref, g2_ref) = refs[:14]
    pos = 14
    if has_vres:
        v1_ref, v2_ref, vf_ref = refs[pos:pos + 3]
        pos += 3
    ob_ref, sout_ref = refs[pos:pos + 2]
    pos += 2
    if not has_vres:
        vf_out_ref = refs[pos]
        pos += 1
    s_s, lr_s, lk_s, lv_s, ll_s = refs[pos:pos + 5]
    L = CHUNK
    c = pl.program_id(1)

    n_pairs = n_heads // 2
    PW = 2 * HEAD_DIM

    @pl.when(c == 0)
    def _():
        zero = jnp.zeros((HEAD_DIM, HEAD_DIM), F32)
        for bi in range(bb):
            for p in range(n_pairs):
                top = jnp.concatenate([s0_ref[bi, 2 * p], zero], axis=1)
                bottom = jnp.concatenate([zero, s0_ref[bi, 2 * p + 1]], axis=1)
                s_s[bi * n_pairs + p] = jnp.concatenate([top, bottom], axis=0)
            lr_s[bi, 0:1, :] = shr_ref[bi]
            lk_s[bi, 0:1, :] = shk_ref[bi]
            lv_s[bi, 0:1, :] = shv_ref[bi]
            ll_s[bi, 0:1, :] = shl_ref[bi]

    def token_shift(x_ref, last_s, mu, bi):
        x = x_ref[bi]
        prev = _shift_rows(x, last_s[bi, 0:1, :])
        last_s[bi, 0:1, :] = x[L - 1:L, :]
        return x + (prev - x) * mu

    pv = lambda i: pv_ref[i:i + 1, :]
    row = lax.broadcasted_iota(jnp.int32, (L, L), 0)
    col = lax.broadcasted_iota(jnp.int32, (L, L), 1)
    tri_incl = (row >= col).astype(F32)

    full = []
    for bi in range(bb):
        r = token_shift(xr_ref, lr_s, pv(PV_MU_R), bi)
        k = token_shift(xk_ref, lk_s, pv(PV_MU_K), bi)
        v = token_shift(xv_ref, lv_s, pv(PV_MU_V), bi)
        lo = token_shift(xl_ref, ll_s, mul_ref[...], bi)

        zw = pv(PV_W0) + _dot(jnp.tanh(lo).astype(BF16), w2_ref[...])
        w = -(jnp.maximum(-zw, 0.0) + jnp.log(1.0 + jnp.exp(-jnp.abs(zw)))) - 0.5
        d = -jnp.exp(w)
        iclr = jax.nn.sigmoid(pv(PV_A0) + _dot(lo.astype(BF16), a2_ref[...]))
        g = _dot(jax.nn.sigmoid(lo).astype(BF16), g2_ref[...])
        if has_vres:
            mix = jax.nn.sigmoid(pv(PV_V0) + _dot(_dot(v.astype(BF16), v1_ref[...]).astype(BF16), v2_ref[...]))
            v = v + (vf_ref[bi] - v) * mix
        else:
            vf_out_ref[bi] = v
        cs = _dot_hp(tri_incl, d)
        cprev = cs - d
        c_last = cs[L - 1:L, :]
        c_mid = cs[L // 2:L // 2 + 1, :]
        full.append(dict(
            r=r, v=v, g=g, iclr=iclr, kks=k * pv(PV_KK), kh=k * (1.0 + (iclr - 1.0) * pv(PV_KA)),
            e_start_prev=jnp.exp(cprev),
            e_start_incl=jnp.exp(cs),
            e_mid_prev=jnp.exp(cprev - c_mid), e_mid_incl=jnp.exp(cs - c_mid), e_from_mid=jnp.exp(c_mid - cs),
            e_to_end=jnp.exp(c_last - cs),
            e_chunk=jnp.exp(c_last)))

    lane_lo = lax.broadcasted_iota(jnp.int32, (1, PW), 1) < HEAD_DIM
    r2 = lax.broadcasted_iota(jnp.int32, (2 * L, 2 * L), 0)
    c2 = lax.broadcasted_iota(jnp.int32, (2 * L, 2 * L), 1)
    same_head = (r2 >= L) == (c2 >= L)
    strict_bd = same_head & (jnp.bitwise_and(r2, L - 1) > jnp.bitwise_and(c2, L - 1))
    eye_bd = (r2 == c2).astype(F32)
    incl_c = (lax.broadcasted_iota(jnp.int32, (L, 2 * L), 0)
              >= jnp.bitwise_and(lax.broadcasted_iota(jnp.int32, (L, 2 * L), 1), L - 1))

    def bd(x):
        return jnp.concatenate([jnp.where(lane_lo, x, 0.0), jnp.where(lane_lo, 0.0, x)], axis=0)

    def tile2(x):
        return jnp.concatenate([x, x], axis=0)

    def head_sum(x):
        lo_sum = jnp.sum(jnp.where(lane_lo, x, 0.0), axis=-1, keepdims=True)
        hi_sum = jnp.sum(jnp.where(lane_lo, 0.0, x), axis=-1, keepdims=True)
        return jnp.where(lane_lo, lo_sum, hi_sum)

    bf = lambda z: z.astype(BF16)
    dot_nt = lambda p_, q_: lax.dot_general(p_, q_, (((1,), (1,)), ((), ())), preferred_element_type=F32)
    dot_tn = lambda p_, q_: lax.dot_general(p_, q_, (((0,), (0,)), ((), ())), preferred_element_type=F32)

    units = [(bi, p) for bi in range(bb) for p in range(n_pairs)]
    sls = [slice(p * PW, (p + 1) * PW) for _, p in units]
    get = lambda name: [full[bi][name][:, sl] for (bi, _), sl in zip(units, sls)]
    mul = lambda xs_, ys_: [x_ * y_ for x_, y_ in zip(xs_, ys_)]

    kk = get("kks")
    kk = [z / jnp.maximum(jnp.sqrt(head_sum(z * z)), 1e-12) for z in kk]
    a_p = [-z for z in kk]
    b_p = mul(kk, get("iclr"))
    k_p, r_p, v_p = get("kh"), get("r"), get("v")
    a0 = mul(a_p, get("e_start_prev"))
    r0 = mul(r_p, get("e_start_incl"))
    at = mul(a_p, get("e_mid_prev"))
    rt = mul(r_p, get("e_mid_incl"))
    e_from_mid, e_to_end = get("e_from_mid"), get("e_to_end")
    bt, kt = mul(b_p, e_from_mid), mul(k_p, e_from_mid)
    bh, kh = mul(b_p, e_to_end), mul(k_p, e_to_end)

    cat0 = lambda x_, y_: jnp.concatenate([x_, y_], axis=0)
    cat1 = lambda x_, y_: jnp.concatenate([x_, y_], axis=1)
    at_bd = [bf(bd(z)) for z in at]
    g_a = [dot_nt(x_, bf(cat0(tile2(b_), tile2(k_)))) for x_, b_, k_ in zip(at_bd, bt, kt)]
    n_ab = [jnp.where(strict_bd, z[:, 0:PW], 0.0) for z in g_a]
    n_ak = [jnp.where(strict_bd, z[:, PW:2 * PW], 0.0) for z in g_a]
    n_r = [jnp.where(cat1(incl_c, incl_c), dot_nt(bf(x_), bf(cat0(bd(b_), bd(k_)))), 0.0)
           for x_, b_, k_ in zip(rt, bt, kt)]
    y_inv = [eye_bd + z for z in n_ab]
    pw = [_dot(bf(z), bf(z)) for z in n_ab]
    for _ in range(int(math.log2(L)) - 2):
        lvl = [_dot(bf(p_), bf(cat1(y_, p_))) for p_, y_ in zip(pw, y_inv)]
        y_inv = [y_ + z[:, 0:PW] for y_, z in zip(y_inv, lvl)]
        pw = [z[:, PW:2 * PW] for z in lvl]
    y_inv = [y_ + _dot(bf(p_), bf(y_)) for p_, y_ in zip(pw, y_inv)]
    v_bd = [bf(bd(z)) for z in v_p]
    t2 = [_dot(bf(x_), y_) for x_, y_ in zip(n_ak, v_bd)]
    w12 = [_dot(bf(x_), bf(cat1(bd(a_), t_))) for x_, a_, t_ in zip(y_inv, a0, t2)]
    s_old = [s_s[i] for i in range(len(units))]
    s_b = [bf(z) for z in s_old]
    u = [dot_nt(bf(z[:, 0:PW]), s_) + z[:, PW:2 * PW] for z, s_ in zip(w12, s_b)]
    y = [dot_nt(bf(r_), s_) + _dot(bf(n_), cat0(bf(u_), v_))
         for r_, s_, n_, u_, v_ in zip(r0, s_b, n_r, u, v_bd)]
    upd = [dot_tn(bf(jnp.concatenate([u_[0:L] + u_[L:2 * L], v_], axis=0)), bf(jnp.concatenate([b_, k_], axis=0)))
           for u_, v_, b_, k_ in zip(u, v_p, bh, kh)]
    for i, (s_, e_, d_) in enumerate(zip(s_old, get("e_chunk"), upd)):
        s_s[i] = s_ * e_ + jnp.where(same_head, d_, 0.0)
    g_p = get("g")
    for i, ((bi, _), sl) in enumerate(zip(units, sls)):
        mean = head_sum(y[i]) * (1.0 / HEAD_DIM)
        yc = y[i] - mean
        var = head_sum(yc * yc) * (1.0 / HEAD_DIM)
        yn = yc * lax.rsqrt(var + LNX_EPS) * pv(PV_LNW)[:, sl] + pv(PV_LNB)[:, sl]
        bonus = head_sum(r_p[i] * k_p[i] * pv(PV_RK)[:, sl])
        ob_ref[bi, :, sl] = ((yn + bonus * v_p[i]) * g_p[i]).astype(ob_ref.dtype)

    @pl.when(c == pl.num_programs(1) - 1)
    def _():
        for i, (bi, p) in enumerate(units):
            s_p = s_s[i]
            sout_ref[bi, 2 * p] = s_p[0:HEAD_DIM, 0:HEAD_DIM]
            sout_ref[bi, 2 * p + 1] = s_p[HEAD_DIM:PW, HEAD_DIM:PW]


def rwkv_mixer(p, shift0, s0, pvec, mu_l, w2p, a2p, g2p, vres, *, cols):
    b, t, _ = p.shape
    nc = t // CHUNK
    n_heads = s0.shape[1]
    cb = n_heads * HEAD_DIM
    rc, kc, vc, lc = cols
    has_vres = vres is not None
    bb = RWKV_BATCH_BLOCK if b % RWKV_BATCH_BLOCK == 0 else 1
    tok = lambda col, w: pl.BlockSpec((bb, CHUNK, w), lambda bi, c, col=col: (bi, c, col))
    const2 = lambda shape: pl.BlockSpec(shape, lambda bi, c: (0, 0))
    perb = lambda w: pl.BlockSpec((bb, 1, w), lambda bi, c: (bi, 0, 0))
    state_spec = pl.BlockSpec((bb, n_heads, HEAD_DIM, HEAD_DIM), lambda bi, c: (bi, 0, 0, 0))
    operands = [p, p, p, p, *shift0, s0, pvec, mu_l, w2p, a2p, g2p]
    in_specs = [tok(rc, cb), tok(kc, cb), tok(vc, cb), tok(lc, LORA_W),
                perb(cb), perb(cb), perb(cb), perb(LORA_W), state_spec,
                const2((PV_ROWS, cb)), const2((1, LORA_W)),
                const2((LORA_W, cb)), const2((LORA_W, cb)), const2((LORA_W, cb))]
    out_specs = [tok(0, cb), state_spec]
    out_shape = [jax.ShapeDtypeStruct((b, t, cb), BF16), jax.ShapeDtypeStruct(s0.shape, F32)]
    if has_vres:
        v1p, v2p, v_first = vres
        operands += [v1p, v2p, v_first]
        in_specs += [const2(v1p.shape), const2(v2p.shape), tok(0, cb)]
    else:
        out_specs.append(tok(0, cb))
        out_shape.append(jax.ShapeDtypeStruct((b, t, cb), F32))
    outs = pl.pallas_call(
        functools.partial(_rwkv_kernel, n_heads=n_heads, has_vres=has_vres, bb=bb),
        grid=(b // bb, nc),
        in_specs=in_specs,
        out_specs=out_specs,
        out_shape=out_shape,
        scratch_shapes=[pltpu.VMEM((bb * n_heads // 2, 2 * HEAD_DIM, 2 * HEAD_DIM), F32),
                        pltpu.VMEM((bb, 8, cb), F32), pltpu.VMEM((bb, 8, cb), F32), pltpu.VMEM((bb, 8, cb), F32),
                        pltpu.VMEM((bb, 8, LORA_W), F32)],
        compiler_params=_cparams(("parallel", "arbitrary")),
        name="rwkv_mixer",
    )(*operands)
    return (outs[0], outs[1], None) if has_vres else tuple(outs)


def _band_offsets(n_left):
    return np.arange(-(CHUNK - 1), (n_left + 1) * CHUNK) - n_left * CHUNK


def _lookup_static(table, idx):
    h = table.shape[0]
    pieces = []
    i, n = 0, len(idx)
    while i < n:
        j = i + 1
        step = int(idx[j] - idx[i]) if j < n else 0
        if step in (-1, 0, 1):
            while j < n and idx[j] - idx[j - 1] == step:
                j += 1
        first, last = int(idx[i]), int(idx[j - 1])
        if first == last:
            pieces.append(jnp.broadcast_to(table[:, first:first + 1], (h, j - i)))
        elif first < last:
            pieces.append(table[:, first:last + 1])
        else:
            pieces.append(jnp.flip(table[:, last:first + 1], axis=1))
        i = j
    return jnp.concatenate(pieces, axis=1)


def _toeplitz(e, n_rows, n_cols):
    h, n = e.shape
    z = jnp.concatenate([e, jnp.zeros((h, 1), e.dtype)], axis=1)
    shifted = jnp.tile(z, (1, n_rows))[:, :n_rows * n].reshape(h, n_rows, n)
    return shifted[:, :, n_rows - 1:n_rows - 1 + n_cols]


def _t5_bucket(rel):
    nb = T5_BUCKETS // 2
    exact = nb // 2
    n = np.abs(rel)
    nf = np.maximum(n, exact).astype(np.float32)
    large = exact + (np.log(nf / exact) / math.log(T5_MAX_DIST / exact) * (nb - exact)).astype(np.int32)
    return np.where(rel > 0, nb, 0) + np.where(n < exact, n, np.minimum(large, nb - 1))


def _bias_a(t5_table):
    e = _lookup_static(t5_table.T.astype(F32), _t5_bucket(_band_offsets(A_LEFT)))
    return _toeplitz(e, CHUNK, (A_LEFT + 1) * CHUNK)


def _bias_c(rel_table):
    idx = np.clip(-_band_offsets(C_LEFT), -REL_CLIP, REL_CLIP) + REL_CLIP
    return _toeplitz(_lookup_static(rel_table.astype(F32), idx), CHUNK, (C_LEFT + 1) * CHUNK)


def _pad_rows(w, start, total):
    return jnp.zeros((total, w.shape[1]), w.dtype).at[start:start + w.shape[0]].set(w)


def kernel(x_prompt, x_sample, cache_a_k, cache_a_v, state_b_wkv, state_b_shift, cache_c_k, cache_c_v, state_d_conv, norm_mix_g, norm_ffn_g, norm_final_g, t5_table, w_in_e, w_out_e, a_sink, b_mu, b_w0, b_w2, b_a0, b_a2, b_g2, b_kk, b_ka, b_rk, b_lnx_w, b_lnx_b, b_v0, b_v1, b_v2, w_in_o, w_out_o, c_rel_table, d_dw_w, d_dw_b, d_ln_g, d_ln_b, ffn_w_gate, ffn_w_up, ffn_w_down):
    depth, d_model = norm_mix_g.shape
    n_even = w_in_e.shape[0]
    h_a = a_sink.shape[1]
    g_a = h_a // KVH_A
    qa = h_a * HEAD_DIM
    kva = KVH_A * HEAD_DIM
    h_b = state_b_wkv.shape[2]
    cb = h_b * HEAD_DIM
    h_c = c_rel_table.shape[1]
    qc = h_c * HEAD_DIM
    cd = d_dw_w.shape[2]
    d_ff = ffn_w_gate.shape[2]
    tf = d_ff // 2 if (d_ff // 2) % 128 == 0 else d_ff

    groups = [x_prompt, x_sample]
    dims = [x.shape[:2] for x in groups]
    xs = [x.reshape(-1, d_model) for x in groups]
    tms = [min(512, x.shape[0]) for x in xs]
    post_tms = [min(1024, x.shape[0]) for x in xs]
    wg, wu, wd = ffn_w_gate.astype(BF16), ffn_w_up.astype(BF16), ffn_w_down.astype(BF16)
    conv_tq = [min(512, t) for _, t in dims]

    o_q, o_k, o_v, o_pb = 0, qa, qa + kva, qa + 2 * kva
    o_r, o_wd, o_kb, o_vb = o_pb, o_pb + cb, o_pb + cb + DECAY_LORA, o_pb + 2 * cb + DECAY_LORA
    o_ad = o_vb + cb
    o_gd = o_ad + ICLR_LORA
    perm_ranges = [(o_q, qa), (o_r, cb), (o_kb, cb), (o_vb, cb), (o_k, kva), (o_v, kva),
                   (o_wd, DECAY_LORA), (o_ad, ICLR_LORA), (o_gd, GATE_LORA)]
    n_r, n_kb, n_vb = qa, qa + cb, qa + 2 * cb
    n_ka = qa + 3 * cb
    n_va = n_ka + kva
    n_lo = n_va + kva
    pb_pieces = [[(o_r, cb)], [(o_kb, cb)], [(o_vb, cb)], [(o_wd, DECAY_LORA), (o_ad, ICLR_LORA), (o_gd, GATE_LORA)]]

    def split_pb(z):
        return [jnp.concatenate([z[..., a - o_pb:a - o_pb + n] for a, n in piece], axis=-1) for piece in pb_pieces]

    def join_pb(r_, k_, v_, lo_):
        return jnp.concatenate([r_, lo_[..., :DECAY_LORA], k_, v_, lo_[..., DECAY_LORA:]], axis=-1)

    bias_a = _bias_a(t5_table)
    st = [[[] for _ in range(7)] for _ in range(2)]
    v_first = [None, None]
    for i in range(depth):
        j = i // 2
        last = i == depth - 1
        if i % 2 == 0:
            w_in = jnp.concatenate([w_in_e[j][:, a:a + n] for a, n in perm_ranges], axis=1).astype(BF16)
            wo = w_out_e[j].astype(BF16)
            mu_r, mu_k, mu_v, mu_lo = split_pb(b_mu[j])
            pvec = jnp.zeros((PV_ROWS, cb), F32)
            rows = {PV_MU_R: mu_r, PV_MU_K: mu_k, PV_MU_V: mu_v, PV_W0: b_w0[j],
                    PV_A0: b_a0[j], PV_KK: b_kk[j], PV_KA: b_ka[j], PV_LNW: b_lnx_w[j], PV_LNB: b_lnx_b[j],
                    PV_RK: b_rk[j].reshape(cb)}
            if j > 0:
                rows[PV_V0] = b_v0[j - 1]
            for ri, val in rows.items():
                pvec = pvec.at[ri].set(val)
            mu_l = mu_lo.reshape(1, LORA_W)
            w2p = _pad_rows(b_w2[j], 0, LORA_W).astype(BF16)
            a2p = _pad_rows(b_a2[j], DECAY_LORA, LORA_W).astype(BF16)
            g2p = _pad_rows(b_g2[j], DECAY_LORA + ICLR_LORA, LORA_W).astype(BF16)
            if j > 0:
                lora_v = b_v1.shape[2]
                v1p = jnp.zeros((cb, 128), F32).at[:, :lora_v].set(b_v1[j - 1]).astype(BF16)
                v2p = _pad_rows(b_v2[j - 1], 0, 128).astype(BF16)
            bias = bias_a
        else:
            w_in = w_in_o[j].astype(BF16)
            wo = w_out_o[j].astype(BF16)
            bias = _bias_c(c_rel_table[j])
        for gi in range(2):
            b, t = dims[gi]
            p = proj_in(xs[gi], norm_mix_g[i], w_in, tms[gi]).reshape(b, t, -1)
            if i % 2 == 0:
                if gi == 0:
                    kv_srcs = None
                    shift0 = [jnp.zeros((b, 1, w_), F32) for w_ in (cb, cb, cb, LORA_W)]
                    s0 = jnp.zeros((b, h_b, HEAD_DIM, HEAD_DIM), F32)
                else:
                    kv_srcs = (cache_a_k[j].reshape(b, -1, kva), cache_a_v[j].reshape(b, -1, kva))
                    shift0 = [z[:, None, :] for z in split_pb(state_b_shift[j])]
                    s0 = state_b_wkv[j]
                o1 = banded_attention(p, kv_srcs, bias, a_sink[j], n_left=A_LEFT, n_heads=h_a, group=g_a,
                                      kcol=n_ka // kva, vcol=n_va // kva)
                vres = None if j == 0 else (v1p, v2p, v_first[gi])
                o2, wkv, vf = rwkv_mixer(p, shift0, s0, pvec, mu_l, w2p, a2p, g2p, vres,
                                         cols=(n_r // cb, n_kb // cb, n_vb // cb, n_lo // LORA_W))
                if j == 0:
                    v_first[gi] = vf
                keep = min(A_LEFT * CHUNK, t)
                nk = p[:, t - keep:, n_ka:n_ka + kva].reshape(b, keep, KVH_A, HEAD_DIM)
                nv = p[:, t - keep:, n_va:n_va + kva].reshape(b, keep, KVH_A, HEAD_DIM)
                nshift = join_pb(*[p[:, t - 1, c0:c0 + w_] for c0, w_ in ((n_r, cb), (n_kb, cb), (n_vb, cb), (n_lo, LORA_W))])
                for s_list, val in zip(st[gi][0:4], (nk, nv, wkv, nshift)):
                    s_list.append(val)
            else:
                if gi == 0:
                    kv_srcs = None
                    conv_prev = None
                else:
                    kv_srcs = (cache_c_k[j].reshape(b, -1, qc), cache_c_v[j].reshape(b, -1, qc))
                    conv_prev = state_d_conv[j]
                o1 = banded_attention(p, kv_srcs, bias, None, n_left=C_LEFT, n_heads=h_c, group=1,
                                      kcol=1, vcol=2)
                o2, nconv = conv_module(p, conv_prev, d_dw_w[j], d_dw_b[j], d_ln_g[j], d_ln_b[j],
                                        acol=3 * qc // cd, gcol=3 * qc // cd + 1, tq=conv_tq[gi])
                keep = min(C_LEFT * CHUNK, t)
                nk = p[:, t - keep:, qc:2 * qc].reshape(b, keep, h_c, HEAD_DIM)
                nv = p[:, t - keep:, 2 * qc:3 * qc].reshape(b, keep, h_c, HEAD_DIM)
                for s_list, val in zip(st[gi][4:7], (nk, nv, nconv)):
                    s_list.append(val)
            half = o1.shape[-1]
            xs[gi] = post(xs[gi], o1.reshape(-1, half), o2.reshape(-1, half), wo, norm_ffn_g[i], wg, wu, wd,
                          norm_final_g, i, tm=post_tms[gi], tf=tf, final_norm=last)
    y_prompt = xs[0].reshape(x_prompt.shape)
    y_sample = xs[1].reshape(x_sample.shape)
    (pak, pav, pbw, pbs, pck, pcv, pdc), (sak, sav, sbw, sbs, sck, scv, sdc) = [[jnp.stack(s) for s in g] for g in st]
    return (y_prompt, y_sample, pak, pav, pbw, pbs, pck, pcv, pdc, sak, sav, sbw, sbs, sck, scv, sdc)
```

```python
import functools
import math

import jax
import jax.numpy as jnp
import numpy as np
from jax import lax
from jax.experimental import pallas as pl
from jax.experimental.pallas import tpu as pltpu

F32 = jnp.float32
BF16 = jnp.bfloat16
HIGHEST = lax.Precision.HIGHEST

CHUNK = 64
HEAD_DIM = 64
RMS_EPS = 1e-6
NEG_INF = -1e30
LNX_EPS = 64e-5
LN_EPS = 1e-5
CONV_W = 31
T5_BUCKETS = 32
T5_MAX_DIST = 128
REL_CLIP = 128
A_LEFT = 2
C_LEFT = 8
KVH_A = 2
DECAY_LORA = 64
ICLR_LORA = 64
GATE_LORA = 128
LORA_W = DECAY_LORA + ICLR_LORA + GATE_LORA
CONV_HALO = 32
CONV_ROWS = 64
POST_ROWS = 256
PAIR_W = 2 * HEAD_DIM
ATTN_TQ = 256
ATTN_SAMPLE_BB = 4
ATTN_PROMPT_HEAD_GROUP = 4

VMEM_LIMIT = 56 * 1024 * 1024


def _cparams(sem):
    return pltpu.CompilerParams(dimension_semantics=sem, vmem_limit_bytes=VMEM_LIMIT)


def _batch_block(b, limit):
    return max(d for d in range(1, limit + 1) if b % d == 0)


def _rms(x, g):
    return x * lax.rsqrt(jnp.mean(x * x, axis=-1, keepdims=True) + RMS_EPS) * g


def _dot(a, b):
    return jnp.dot(a, b, preferred_element_type=F32)


def _dot_hp(a, b):
    return jnp.dot(a, b, precision=HIGHEST, preferred_element_type=F32)


def _dot_nt_hp(a, b):
    return lax.dot_general(a, b, (((1,), (1,)), ((), ())), precision=HIGHEST, preferred_element_type=F32)


def _dot_tn_hp(a, b):
    return lax.dot_general(a, b, (((0,), (0,)), ((), ())), precision=HIGHEST, preferred_element_type=F32)


def _proj_in_kernel(x_ref, g_ref, w_ref, o_ref):
    h = _rms(x_ref[...], g_ref[...]).astype(BF16)
    o_ref[...] = _dot(h, w_ref[...])


def proj_in(x2d, g, w_bf16, tm):
    m, d = x2d.shape
    n = w_bf16.shape[1]
    return pl.pallas_call(
        _proj_in_kernel,
        grid=(m // tm,),
        in_specs=[pl.BlockSpec((tm, d), lambda i: (i, 0)),
                  pl.BlockSpec((1, d), lambda i: (0, 0)),
                  pl.BlockSpec((d, n), lambda i: (0, 0))],
        out_specs=pl.BlockSpec((tm, n), lambda i: (i, 0)),
        out_shape=jax.ShapeDtypeStruct((m, n), F32),
        compiler_params=_cparams(("parallel",)),
        name="proj_in",
    )(x2d, g.reshape(1, d), w_bf16)


def _post_kernel(x_ref, o1_ref, o2_ref, wo_ref, g_ref, wg_ref, wu_ref, wd_ref, gf_ref, out_ref, h_s, *, final_norm):
    f = pl.program_id(1)
    half = o1_ref.shape[-1]
    tm = h_s.shape[0]
    rs = min(tm, POST_ROWS)

    @pl.when(f == 0)
    def _():
        for r0 in range(0, tm, rs):
            rows = slice(r0, r0 + rs)
            xn = (x_ref[rows, :] + _dot(o1_ref[rows, :], wo_ref[0:half, :])
                  + _dot(o2_ref[rows, :], wo_ref[half:2 * half, :]))
            out_ref[rows, :] = xn
            h_s[rows, :] = _rms(xn, g_ref[...]).astype(BF16)

    for r0 in range(0, tm, rs):
        rows = slice(r0, r0 + rs)
        h = h_s[rows, :]
        gate = _dot(h, wg_ref[...])
        up = _dot(h, wu_ref[...])
        act = gate * jax.nn.sigmoid(gate) * up
        out_ref[rows, :] += _dot(act.astype(BF16), wd_ref[...])

    if final_norm:
        @pl.when(f == pl.num_programs(1) - 1)
        def _():
            for r0 in range(0, tm, rs):
                rows = slice(r0, r0 + rs)
                out_ref[rows, :] = _rms(out_ref[rows, :], gf_ref[...])


def post(x2d, o1, o2, wo, g, wg, wu, wd, gf, layer, *, tm, tf, final_norm):
    m, d = x2d.shape
    half = o1.shape[-1]
    dff = wg.shape[2]
    return pl.pallas_call(
        functools.partial(_post_kernel, final_norm=final_norm),
        grid=(m // tm, dff // tf),
        in_specs=[pl.BlockSpec((tm, d), lambda i, f: (i, 0)),
                  pl.BlockSpec((tm, half), lambda i, f: (i, 0)),
                  pl.BlockSpec((tm, half), lambda i, f: (i, 0)),
                  pl.BlockSpec((2 * half, d), lambda i, f: (0, 0)),
                  pl.BlockSpec((1, d), lambda i, f: (0, 0)),
                  pl.BlockSpec((None, d, tf), lambda i, f: (layer, 0, f)),
                  pl.BlockSpec((None, d, tf), lambda i, f: (layer, 0, f)),
                  pl.BlockSpec((None, tf, d), lambda i, f: (layer, f, 0)),
                  pl.BlockSpec((1, d), lambda i, f: (0, 0))],
        out_specs=pl.BlockSpec((tm, d), lambda i, f: (i, 0)),
        out_shape=jax.ShapeDtypeStruct((m, d), F32),
        scratch_shapes=[pltpu.VMEM((tm, d), BF16)],
        compiler_params=_cparams(("parallel", "arbitrary")),
        name="post",
    )(x2d, o1, o2, wo, g.reshape(1, d), wg, wu, wd, gf.reshape(1, d))


def _attend_heads(q, k_tile, v_tile, bias_ref, sink_ref, valid, n_heads, write, group_heads):
    lane_lo = lax.broadcasted_iota(jnp.int32, (1, PAIR_W), 1) < HEAD_DIM
    lane_hi = jnp.logical_not(lane_lo)
    for h0 in range(0, n_heads, group_heads):
        hs = list(range(h0, min(h0 + group_heads, n_heads)))
        qh = [jnp.where(lane_lo if h % 2 == 0 else lane_hi, q[:, (h // 2) * PAIR_W:(h // 2 + 1) * PAIR_W], 0.0)
              .astype(BF16) for h in hs]
        s = [lax.dot_general(z, k_tile(h), (((1,), (1,)), ((), ())), preferred_element_type=F32) + bias_ref[h]
             for z, h in zip(qh, hs)]
        if valid is not None:
            s = [jnp.where(valid, z, NEG_INF) for z in s]
        m = [jnp.max(z, axis=-1, keepdims=True) for z in s]
        if sink_ref is not None:
            sk = [sink_ref[h:h + 1, :] for h in hs]
            m = [jnp.maximum(z, k_) for z, k_ in zip(m, sk)]
        e = [jnp.exp(z - m_) for z, m_ in zip(s, m)]
        den = [jnp.sum(z, axis=-1, keepdims=True) for z in e]
        if sink_ref is not None:
            den = [z + jnp.exp(k_ - m_) for z, k_, m_ in zip(den, sk, m)]
        o = [_dot(z.astype(BF16), v_tile(h)) / d_ for z, h, d_ in zip(e, hs, den)]
        for i in range(0, len(hs), 2):
            write(hs[i] // 2, jnp.where(lane_lo, o[i], o[i + 1]))


def _kv_tile_fns(k_plain, v_plain, k_swap, v_swap, group):
    if group == 1:
        tile = lambda src, h: src[:, (h // 2) * PAIR_W:(h // 2 + 1) * PAIR_W]
        return (lambda h: tile(k_plain, h)), (lambda h: tile(v_plain, h))
    pick = lambda plain, swap, h: (plain if h // group == h % 2 else swap)[...]
    return (lambda h: pick(k_plain, k_swap, h)), (lambda h: pick(v_plain, v_swap, h))


def _attn_prompt_kernel(*refs, n_left, tq, n_heads, group, has_sink):
    q_ref, k_ref, v_ref, bias_ref = refs[:4]
    sink_ref = refs[4] if has_sink else None
    o_ref = refs[5] if has_sink else refs[4]
    wins = refs[6:] if has_sink else refs[5:]
    hist = n_left * CHUNK
    nk = hist + tq
    i = pl.program_id(1)

    @pl.when(i == 0)
    def _():
        for w in wins:
            w[0:hist, :] = jnp.zeros((hist, w.shape[1]), BF16)

    k_new = k_ref[0]
    v_new = v_ref[0]
    wins[0][hist:nk, :] = k_new.astype(BF16)
    wins[1][hist:nk, :] = v_new.astype(BF16)
    if group > 1:
        wins[2][hist:nk, :] = pltpu.roll(k_new, HEAD_DIM, axis=1).astype(BF16)
        wins[3][hist:nk, :] = pltpu.roll(v_new, HEAD_DIM, axis=1).astype(BF16)
    valid = lax.broadcasted_iota(jnp.int32, (tq, nk), 1) >= hist - i * tq
    k_tile, v_tile = _kv_tile_fns(wins[0], wins[1], wins[2] if group > 1 else None, wins[3] if group > 1 else None,
                                  group)

    def write(pair, val):
        o_ref[0, :, pair * PAIR_W:(pair + 1) * PAIR_W] = val.astype(o_ref.dtype)

    _attend_heads(q_ref[0] * (HEAD_DIM ** -0.5), k_tile, v_tile, bias_ref, sink_ref, valid, n_heads, write,
                  ATTN_PROMPT_HEAD_GROUP if group == 1 else 8)
    for w in wins:
        w[0:hist, :] = w[tq:nk, :]


def attention_prompt(p, bias, sink, *, n_left, n_heads, group, kcol, vcol):
    b, t, _ = p.shape
    kvw = (n_heads // group) * HEAD_DIM
    qw = n_heads * HEAD_DIM
    has_sink = sink is not None
    assert group == 1 or kvw == PAIR_W, "grouped-query path assumes two kv heads sharing one lane tile"
    tq = min(ATTN_TQ, t)
    nk = n_left * CHUNK + tq
    bias_full = jnp.concatenate(
        [jnp.pad(bias, ((0, 0), (0, 0), (qc * CHUNK, nk - qc * CHUNK - bias.shape[2])), constant_values=NEG_INF)
         for qc in range(tq // CHUNK)], axis=1)
    sink_ops = [sink.reshape(n_heads, 1)] if has_sink else []
    sink_specs = [pl.BlockSpec((n_heads, 1), lambda bi, i: (0, 0))] if has_sink else []
    return pl.pallas_call(
        functools.partial(_attn_prompt_kernel, n_left=n_left, tq=tq, n_heads=n_heads, group=group, has_sink=has_sink),
        grid=(b, t // tq),
        in_specs=[pl.BlockSpec((1, tq, qw), lambda bi, i: (bi, i, 0)),
                  pl.BlockSpec((1, tq, kvw), lambda bi, i: (bi, i, kcol)),
                  pl.BlockSpec((1, tq, kvw), lambda bi, i: (bi, i, vcol)),
                  pl.BlockSpec(bias_full.shape, lambda bi, i: (0, 0, 0))] + sink_specs,
        out_specs=pl.BlockSpec((1, tq, qw), lambda bi, i: (bi, i, 0)),
        out_shape=jax.ShapeDtypeStruct((b, t, qw), BF16),
        scratch_shapes=[pltpu.VMEM((nk, kvw), BF16)] * (4 if group > 1 else 2),
        compiler_params=_cparams(("parallel", "arbitrary")),
        name="attention_prompt",
    )(p, p, p, bias_full, *sink_ops)


def _attn_sample_kernel(*refs, n_heads, group, has_sink, bb):
    q_ref, kc_ref, vc_ref, kn_ref, vn_ref, bias_ref = refs[:6]
    sink_ref = refs[6] if has_sink else None
    o_ref = refs[-1]
    hist = kc_ref.shape[-1]
    nt = (((1,), (1,)), ((), ()))
    heads = range(n_heads)
    q_sl = [slice(h * HEAD_DIM, (h + 1) * HEAD_DIM) for h in heads]
    kv_sl = [slice((h // group) * HEAD_DIM, (h // group + 1) * HEAD_DIM) for h in heads]
    for bi in range(bb):
        q = (q_ref[bi] * (HEAD_DIM ** -0.5)).astype(BF16)
        kn = kn_ref[bi].astype(BF16)
        vn = vn_ref[bi].astype(BF16)
        qh = [q[:, sl] for sl in q_sl]
        s = [jnp.concatenate([_dot(qh[h], kc_ref[bi, h // group].astype(BF16)),
                              lax.dot_general(qh[h], kn[:, kv_sl[h]], nt, preferred_element_type=F32)], axis=1)
             + bias_ref[h] for h in heads]
        m = [jnp.max(z, axis=-1, keepdims=True) for z in s]
        if has_sink:
            sk = [sink_ref[h:h + 1, :] for h in heads]
            m = [jnp.maximum(z, k_) for z, k_ in zip(m, sk)]
        e = [jnp.exp(z - m_) for z, m_ in zip(s, m)]
        den = [jnp.sum(z, axis=-1, keepdims=True) for z in e]
        if has_sink:
            den = [z + jnp.exp(k_ - m_) for z, k_, m_ in zip(den, sk, m)]
        for h in heads:
            eb = e[h].astype(BF16)
            o = (lax.dot_general(eb[:, 0:hist], vc_ref[bi, h // group].astype(BF16), nt, preferred_element_type=F32)
                 + _dot(eb[:, hist:], vn[:, kv_sl[h]]))
            o_ref[bi, :, q_sl[h]] = (o / den[h]).astype(o_ref.dtype)


def attention_sample(p, k_cache, v_cache, layer, bias, sink, *, n_heads, group, kcol, vcol):
    b, t, _ = p.shape
    qw = n_heads * HEAD_DIM
    kv_heads = n_heads // group
    kvw = kv_heads * HEAD_DIM
    hist = k_cache.shape[2]
    has_sink = sink is not None
    bb = _batch_block(b, ATTN_SAMPLE_BB)
    transposed = lambda z: jnp.transpose(z, (0, 1, 3, 4, 2))
    cache_spec = pl.BlockSpec((None, bb, kv_heads, HEAD_DIM, hist), lambda i: (layer, i, 0, 0, 0))
    sink_ops = [sink.reshape(n_heads, 1)] if has_sink else []
    sink_specs = [pl.BlockSpec((n_heads, 1), lambda i: (0, 0))] if has_sink else []
    return pl.pallas_call(
        functools.partial(_attn_sample_kernel, n_heads=n_heads, group=group, has_sink=has_sink, bb=bb),
        grid=(b // bb,),
        in_specs=[pl.BlockSpec((bb, t, qw), lambda i: (i, 0, 0)), cache_spec, cache_spec,
                  pl.BlockSpec((bb, t, kvw), lambda i: (i, 0, kcol)),
                  pl.BlockSpec((bb, t, kvw), lambda i: (i, 0, vcol)),
                  pl.BlockSpec(bias.shape, lambda i: (0, 0, 0))] + sink_specs,
        out_specs=pl.BlockSpec((bb, t, qw), lambda i: (i, 0, 0)),
        out_shape=jax.ShapeDtypeStruct((b, t, qw), BF16),
        compiler_params=_cparams(("parallel",)),
        name="attention_sample",
    )(p, transposed(k_cache), transposed(v_cache), p, p, bias, *sink_ops)


def _conv_kernel(*refs, tq, from_state):
    if from_state:
        a_ref, gt_ref, st_in_ref, dw_ref, db_ref, lg_ref, lb_ref, o_ref, st_ref, buf = refs
    else:
        a_ref, gt_ref, pa_ref, pg_ref, dw_ref, db_ref, lg_ref, lb_ref, o_ref, st_ref, buf = refs
    keep = CONV_W - 1
    pad = CONV_HALO - keep
    if from_state:
        buf[0:pad, :] = jnp.zeros((pad, buf.shape[1]), F32)
        buf[pad:CONV_HALO, :] = st_in_ref[0]
    else:
        prev = pa_ref[0] * jax.nn.sigmoid(pg_ref[0])
        buf[0:CONV_HALO, :] = jnp.where(pl.program_id(1) > 0, prev, 0.0)
    buf[CONV_HALO:CONV_HALO + tq, :] = a_ref[0] * jax.nn.sigmoid(gt_ref[0])
    rs = min(tq, CONV_ROWS)
    for r0 in range(0, tq, rs):
        z = jnp.zeros((rs, buf.shape[1]), F32) + db_ref[...]
        base = buf[r0:r0 + rs + CONV_HALO, :]
        for s in range(8):
            rolled = pltpu.roll(base, rs + CONV_HALO - (pad + s), axis=0)
            for j, w in enumerate(range(s, CONV_W, 8)):
                z = z + rolled[8 * j:8 * j + rs, :] * dw_ref[w:w + 1, :]
        mean = jnp.mean(z, axis=-1, keepdims=True)
        zc = z - mean
        var = jnp.mean(zc * zc, axis=-1, keepdims=True)
        zn = zc * lax.rsqrt(var + LN_EPS) * lg_ref[...] + lb_ref[...]
        o_ref[0, r0:r0 + rs, :] = (zn * jax.nn.sigmoid(zn)).astype(o_ref.dtype)
    st_ref[0] = buf[tq + pad:tq + CONV_HALO, :]


def conv_module(p, state, dw_w, dw_b, ln_g, ln_b, *, acol, gcol, tq):
    b, t, _ = p.shape
    cd = dw_w.shape[1]
    keep = CONV_W - 1
    from_state = state is not None
    operands = [p, p]
    in_specs = [pl.BlockSpec((1, tq, cd), lambda bi, i: (bi, i, acol)),
                pl.BlockSpec((1, tq, cd), lambda bi, i: (bi, i, gcol))]
    if from_state:
        operands.append(state)
        in_specs.append(pl.BlockSpec((1, keep, cd), lambda bi, i: (bi, 0, 0)))
    else:
        r = tq // CONV_HALO
        for col in (acol, gcol):
            operands.append(p)
            in_specs.append(pl.BlockSpec((1, CONV_HALO, cd),
                                         lambda bi, i, col=col: (bi, jnp.maximum(i * r - 1, 0), col)))
    operands += [dw_w, dw_b.reshape(1, cd), ln_g.reshape(1, cd), ln_b.reshape(1, cd)]
    in_specs += [pl.BlockSpec((CONV_W, cd), lambda bi, i: (0, 0))] + [pl.BlockSpec((1, cd), lambda bi, i: (0, 0))] * 3
    return pl.pallas_call(
        functools.partial(_conv_kernel, tq=tq, from_state=from_state),
        grid=(b, t // tq),
        in_specs=in_specs,
        out_specs=[pl.BlockSpec((1, tq, cd), lambda bi, i: (bi, i, 0)),
                   pl.BlockSpec((1, keep, cd), lambda bi, i: (bi, 0, 0))],
        out_shape=[jax.ShapeDtypeStruct((b, t, cd), BF16), jax.ShapeDtypeStruct((b, keep, cd), F32)],
        scratch_shapes=[pltpu.VMEM((CONV_HALO + tq, cd), F32)],
        compiler_params=_cparams(("parallel", "arbitrary")),
        name="conv_module",
    )(*operands)


PV_MU_R, PV_MU_K, PV_MU_V, PV_W0, PV_A0, PV_KK, PV_KA, PV_V0, PV_LNW, PV_LNB, PV_RK = range(11)
PV_ROWS = 16
RWKV_BATCH_BLOCK = 4


def _shift_rows(x, first_row):
    rolled = pltpu.roll(x, 1, axis=0)
    row = lax.broadcasted_iota(jnp.int32, x.shape, 0)
    return jnp.where(row == 0, first_row, rolled)


def _rwkv_kernel(*refs, n_heads, has_vres, bb):
    (xr_ref, xk_ref, xv_ref, xl_ref, shr_ref, shk_ref, shv_ref, shl_ref, s0_ref, pv_ref, mul_ref,
     w2_ref, a2_ref, g2_ref) = refs[:14]
    pos = 14
    if has_vres:
        v1_ref, v2_ref, vf_ref = refs[pos:pos + 3]
        pos += 3
    ob_ref, sout_ref = refs[pos:pos + 2]
    pos += 2
    if not has_vres:
        vf_out_ref = refs[pos]
        pos += 1
    s_s, lr_s, lk_s, lv_s, ll_s = refs[pos:pos + 5]
    L = CHUNK
    c = pl.program_id(1)

    n_pairs = n_heads // 2
    PW = 2 * HEAD_DIM

    @pl.when(c == 0)
    def _():
        zero = jnp.zeros((HEAD_DIM, HEAD_DIM), F32)
        for bi in range(bb):
            for p in range(n_pairs):
                top = jnp.concatenate([s0_ref[bi, 2 * p], zero], axis=1)
                bottom = jnp.concatenate([zero, s0_ref[bi, 2 * p + 1]], axis=1)
                s_s[bi * n_pairs + p] = jnp.concatenate([top, bottom], axis=0)
            lr_s[bi, 0:1, :] = shr_ref[bi]
            lk_s[bi, 0:1, :] = shk_ref[bi]
            lv_s[bi, 0:1, :] = shv_ref[bi]
            ll_s[bi, 0:1, :] = shl_ref[bi]

    def token_shift(x_ref, last_s, mu, bi):
        x = x_ref[bi]
        prev = _shift_rows(x, last_s[bi, 0:1, :])
        last_s[bi, 0:1, :] = x[L - 1:L, :]
        return x + (prev - x) * mu

    pv = lambda i: pv_ref[i:i + 1, :]
    row = lax.broadcasted_iota(jnp.int32, (L, L), 0)
    col = lax.broadcasted_iota(jnp.int32, (L, L), 1)
    tri_incl = (row >= col).astype(F32)

    full = []
    for bi in range(bb):
        r = token_shift(xr_ref, lr_s, pv(PV_MU_R), bi)
        k = token_shift(xk_ref, lk_s, pv(PV_MU_K), bi)
        v = token_shift(xv_ref, lv_s, pv(PV_MU_V), bi)
        lo = token_shift(xl_ref, ll_s, mul_ref[...], bi)

        zw = pv(PV_W0) + _dot(jnp.tanh(lo).astype(BF16), w2_ref[...])
        w = -(jnp.maximum(-zw, 0.0) + jnp.log(1.0 + jnp.exp(-jnp.abs(zw)))) - 0.5
        d = -jnp.exp(w)
        iclr = jax.nn.sigmoid(pv(PV_A0) + _dot(lo.astype(BF16), a2_ref[...]))
        g = _dot(jax.nn.sigmoid(lo).astype(BF16), g2_ref[...])
        if has_vres:
            mix = jax.nn.sigmoid(pv(PV_V0) + _dot(_dot(v.astype(BF16), v1_ref[...]).astype(BF16), v2_ref[...]))
            v = v + (vf_ref[bi] - v) * mix
        else:
            vf_out_ref[bi] = v
        cs = _dot_hp(tri_incl, d)
        cprev = cs - d
        c_last = cs[L - 1:L, :]
        c_mid = cs[L // 2:L // 2 + 1, :]
        full.append(dict(
            r=r, v=v, g=g, iclr=iclr, kks=k * pv(PV_KK), kh=k * (1.0 + (iclr - 1.0) * pv(PV_KA)),
            e_start_prev=jnp.exp(cprev),
            e_start_incl=jnp.exp(cs),
            e_mid_prev=jnp.exp(cprev - c_mid), e_mid_incl=jnp.exp(cs - c_mid), e_from_mid=jnp.exp(c_mid - cs),
            e_to_end=jnp.exp(c_last - cs),
            e_chunk=jnp.exp(c_last)))

    lane_lo = lax.broadcasted_iota(jnp.int32, (1, PW), 1) < HEAD_DIM
    r2 = lax.broadcasted_iota(jnp.int32, (2 * L, 2 * L), 0)
    c2 = lax.broadcasted_iota(jnp.int32, (2 * L, 2 * L), 1)
    same_head = (r2 >= L) == (c2 >= L)
    strict_bd = same_head & (jnp.bitwise_and(r2, L - 1) > jnp.bitwise_and(c2, L - 1))
    eye_bd = (r2 == c2).astype(F32)
    incl_c = (lax.broadcasted_iota(jnp.int32, (L, 2 * L), 0)
              >= jnp.bitwise_and(lax.broadcasted_iota(jnp.int32, (L, 2 * L), 1), L - 1))

    def bd(x):
        return jnp.concatenate([jnp.where(lane_lo, x, 0.0), jnp.where(lane_lo, 0.0, x)], axis=0)

    def tile2(x):
        return jnp.concatenate([x, x], axis=0)

    def head_sum(x):
        lo_sum = jnp.sum(jnp.where(lane_lo, x, 0.0), axis=-1, keepdims=True)
        hi_sum = jnp.sum(jnp.where(lane_lo, 0.0, x), axis=-1, keepdims=True)
        return jnp.where(lane_lo, lo_sum, hi_sum)

    bf = lambda z: z.astype(BF16)
    dot_nt = lambda p_, q_: lax.dot_general(p_, q_, (((1,), (1,)), ((), ())), preferred_element_type=F32)
    dot_tn = lambda p_, q_: lax.dot_general(p_, q_, (((0,), (0,)), ((), ())), preferred_element_type=F32)

    units = [(bi, p) for bi in range(bb) for p in range(n_pairs)]
    sls = [slice(p * PW, (p + 1) * PW) for _, p in units]
    get = lambda name: [full[bi][name][:, sl] for (bi, _), sl in zip(units, sls)]
    mul = lambda xs_, ys_: [x_ * y_ for x_, y_ in zip(xs_, ys_)]

    kk = get("kks")
    kk = [z / jnp.maximum(jnp.sqrt(head_sum(z * z)), 1e-12) for z in kk]
    a_p = [-z for z in kk]
    b_p = mul(kk, get("iclr"))
    k_p, r_p, v_p = get("kh"), get("r"), get("v")
    a0 = mul(a_p, get("e_start_prev"))
    r0 = mul(r_p, get("e_start_incl"))
    at = mul(a_p, get("e_mid_prev"))
    rt = mul(r_p, get("e_mid_incl"))
    e_from_mid, e_to_end = get("e_from_mid"), get("e_to_end")
    bt, kt = mul(b_p, e_from_mid), mul(k_p, e_from_mid)
    bh, kh = mul(b_p, e_to_end), mul(k_p, e_to_end)

    cat0 = lambda x_, y_: jnp.concatenate([x_, y_], axis=0)
    cat1 = lambda x_, y_: jnp.concatenate([x_, y_], axis=1)
    at_bd = [bf(bd(z)) for z in at]
    g_a = [dot_nt(x_, bf(cat0(tile2(b_), tile2(k_)))) for x_, b_, k_ in zip(at_bd, bt, kt)]
    n_ab = [jnp.where(strict_bd, z[:, 0:PW], 0.0) for z in g_a]
    n_ak = [jnp.where(strict_bd, z[:, PW:2 * PW], 0.0) for z in g_a]
    n_r = [jnp.where(cat1(incl_c, incl_c), dot_nt(bf(x_), bf(cat0(bd(b_), bd(k_)))), 0.0)
           for x_, b_, k_ in zip(rt, bt, kt)]
    y_inv = [eye_bd + z for z in n_ab]
    pw = [_dot(bf(z), bf(z)) for z in n_ab]
    for _ in range(int(math.log2(L)) - 2):
        lvl = [_dot(bf(p_), bf(cat1(y_, p_))) for p_, y_ in zip(pw, y_inv)]
        y_inv = [y_ + z[:, 0:PW] for y_, z in zip(y_inv, lvl)]
        pw = [z[:, PW:2 * PW] for z in lvl]
    y_inv = [y_ + _dot(bf(p_), bf(y_)) for p_, y_ in zip(pw, y_inv)]
    v_bd = [bf(bd(z)) for z in v_p]
    t2 = [_dot(bf(x_), y_) for x_, y_ in zip(n_ak, v_bd)]
    w12 = [_dot(bf(x_), bf(cat1(bd(a_), t_))) for x_, a_, t_ in zip(y_inv, a0, t2)]
    s_old = [s_s[i] for i in range(len(units))]
    s_b = [bf(z) for z in s_old]
    u = [dot_nt(bf(z[:, 0:PW]), s_) + z[:, PW:2 * PW] for z, s_ in zip(w12, s_b)]
    y = [dot_nt(bf(r_), s_) + _dot(bf(n_), cat0(bf(u_), v_))
         for r_, s_, n_, u_, v_ in zip(r0, s_b, n_r, u, v_bd)]
    upd = [dot_tn(bf(jnp.concatenate([u_[0:L] + u_[L:2 * L], v_], axis=0)), bf(jnp.concatenate([b_, k_], axis=0)))
           for u_, v_, b_, k_ in zip(u, v_p, bh, kh)]
    for i, (s_, e_, d_) in enumerate(zip(s_old, get("e_chunk"), upd)):
        s_s[i] = s_ * e_ + jnp.where(same_head, d_, 0.0)
    g_p = get("g")
    for i, ((bi, _), sl) in enumerate(zip(units, sls)):
        mean = head_sum(y[i]) * (1.0 / HEAD_DIM)
        yc = y[i] - mean
        var = head_sum(yc * yc) * (1.0 / HEAD_DIM)
        yn = yc * lax.rsqrt(var + LNX_EPS) * pv(PV_LNW)[:, sl] + pv(PV_LNB)[:, sl]
        bonus = head_sum(r_p[i] * k_p[i] * pv(PV_RK)[:, sl])
        ob_ref[bi, :, sl] = ((yn + bonus * v_p[i]) * g_p[i]).astype(ob_ref.dtype)

    @pl.when(c == pl.num_programs(1) - 1)
    def _():
        for i, (bi, p) in enumerate(units):
            s_p = s_s[i]
            sout_ref[bi, 2 * p] = s_p[0:HEAD_DIM, 0:HEAD_DIM]
            sout_ref[bi, 2 * p + 1] = s_p[HEAD_DIM:PW, HEAD_DIM:PW]


def rwkv_mixer(p, shift0, s0, pvec, mu_l, w2p, a2p, g2p, vres, *, cols):
    b, t, _ = p.shape
    nc = t // CHUNK
    n_heads = s0.shape[1]
    cb = n_heads * HEAD_DIM
    rc, kc, vc, lc = cols
    has_vres = vres is not None
    bb = _batch_block(b, RWKV_BATCH_BLOCK)
    tok = lambda col, w: pl.BlockSpec((bb, CHUNK, w), lambda bi, c, col=col: (bi, c, col))
    const2 = lambda shape: pl.BlockSpec(shape, lambda bi, c: (0, 0))
    perb = lambda w: pl.BlockSpec((bb, 1, w), lambda bi, c: (bi, 0, 0))
    state_spec = pl.BlockSpec((bb, n_heads, HEAD_DIM, HEAD_DIM), lambda bi, c: (bi, 0, 0, 0))
    operands = [p, p, p, p, *shift0, s0, pvec, mu_l, w2p, a2p, g2p]
    in_specs = [tok(rc, cb), tok(kc, cb), tok(vc, cb), tok(lc, LORA_W),
                perb(cb), perb(cb), perb(cb), perb(LORA_W), state_spec,
                const2((PV_ROWS, cb)), const2((1, LORA_W)),
                const2((LORA_W, cb)), const2((LORA_W, cb)), const2((LORA_W, cb))]
    out_specs = [tok(0, cb), state_spec]
    out_shape = [jax.ShapeDtypeStruct((b, t, cb), BF16), jax.ShapeDtypeStruct(s0.shape, F32)]
    if has_vres:
        v1p, v2p, v_first = vres
        operands += [v1p, v2p, v_first]
        in_specs += [const2(v1p.shape), const2(v2p.shape), tok(0, cb)]
    else:
        out_specs.append(tok(0, cb))
        out_shape.append(jax.ShapeDtypeStruct((b, t, cb), F32))
    outs = pl.pallas_call(
        functools.partial(_rwkv_kernel, n_heads=n_heads, has_vres=has_vres, bb=bb),
        grid=(b // bb, nc),
        in_specs=in_specs,
        out_specs=out_specs,
        out_shape=out_shape,
        scratch_shapes=[pltpu.VMEM((bb * n_heads // 2, 2 * HEAD_DIM, 2 * HEAD_DIM), F32),
                        pltpu.VMEM((bb, 8, cb), F32), pltpu.VMEM((bb, 8, cb), F32), pltpu.VMEM((bb, 8, cb), F32),
                        pltpu.VMEM((bb, 8, LORA_W), F32)],
        compiler_params=_cparams(("parallel", "arbitrary")),
        name="rwkv_mixer",
    )(*operands)
    return (outs[0], outs[1], None) if has_vres else tuple(outs)


def _band_offsets(n_left):
    return np.arange(-(CHUNK - 1), (n_left + 1) * CHUNK) - n_left * CHUNK


def _lookup_static(table, idx):
    h = table.shape[0]
    pieces = []
    i, n = 0, len(idx)
    while i < n:
        j = i + 1
        step = int(idx[j] - idx[i]) if j < n else 0
        if step in (-1, 0, 1):
            while j < n and idx[j] - idx[j - 1] == step:
                j += 1
        first, last = int(idx[i]), int(idx[j - 1])
        if first == last:
            pieces.append(jnp.broadcast_to(table[:, first:first + 1], (h, j - i)))
        elif first < last:
            pieces.append(table[:, first:last + 1])
        else:
            pieces.append(jnp.flip(table[:, last:first + 1], axis=1))
        i = j
    return jnp.concatenate(pieces, axis=1)


def _toeplitz(e, n_rows, n_cols):
    h, n = e.shape
    z = jnp.concatenate([e, jnp.zeros((h, 1), e.dtype)], axis=1)
    shifted = jnp.tile(z, (1, n_rows))[:, :n_rows * n].reshape(h, n_rows, n)
    return shifted[:, :, n_rows - 1:n_rows - 1 + n_cols]


def _t5_bucket(rel):
    nb = T5_BUCKETS // 2
    exact = nb // 2
    n = np.abs(rel)
    nf = np.maximum(n, exact).astype(np.float32)
    large = exact + (np.log(nf / exact) / math.log(T5_MAX_DIST / exact) * (nb - exact)).astype(np.int32)
    return np.where(rel > 0, nb, 0) + np.where(n < exact, n, np.minimum(large, nb - 1))


def _bias_a(t5_table):
    e = _lookup_static(t5_table.T.astype(F32), _t5_bucket(_band_offsets(A_LEFT)))
    return _toeplitz(e, CHUNK, (A_LEFT + 1) * CHUNK)


def _bias_c(rel_table):
    idx = np.clip(-_band_offsets(C_LEFT), -REL_CLIP, REL_CLIP) + REL_CLIP
    return _toeplitz(_lookup_static(rel_table.astype(F32), idx), CHUNK, (C_LEFT + 1) * CHUNK)


def _pad_rows(w, start, total):
    return jnp.zeros((total, w.shape[1]), w.dtype).at[start:start + w.shape[0]].set(w)


def kernel(x_prompt, x_sample, cache_a_k, cache_a_v, state_b_wkv, state_b_shift, cache_c_k, cache_c_v, state_d_conv, norm_mix_g, norm_ffn_g, norm_final_g, t5_table, w_in_e, w_out_e, a_sink, b_mu, b_w0, b_w2, b_a0, b_a2, b_g2, b_kk, b_ka, b_rk, b_lnx_w, b_lnx_b, b_v0, b_v1, b_v2, w_in_o, w_out_o, c_rel_table, d_dw_w, d_dw_b, d_ln_g, d_ln_b, ffn_w_gate, ffn_w_up, ffn_w_down):
    depth, d_model = norm_mix_g.shape
    n_even = w_in_e.shape[0]
    h_a = a_sink.shape[1]
    g_a = h_a // KVH_A
    qa = h_a * HEAD_DIM
    kva = KVH_A * HEAD_DIM
    h_b = state_b_wkv.shape[2]
    cb = h_b * HEAD_DIM
    h_c = c_rel_table.shape[1]
    qc = h_c * HEAD_DIM
    cd = d_dw_w.shape[2]
    d_ff = ffn_w_gate.shape[2]
    tf = d_ff // 2 if (d_ff // 2) % 128 == 0 else d_ff

    groups = [x_prompt, x_sample]
    dims = [x.shape[:2] for x in groups]
    xs = [x.reshape(-1, d_model) for x in groups]
    tms = [min(512, x.shape[0]) for x in xs]
    post_tms = [min(1024, x.shape[0]) for x in xs]
    wg, wu, wd = ffn_w_gate.astype(BF16), ffn_w_up.astype(BF16), ffn_w_down.astype(BF16)
    conv_tq = [min(512, t) for _, t in dims]

    o_q, o_k, o_v, o_pb = 0, qa, qa + kva, qa + 2 * kva
    o_r, o_wd, o_kb, o_vb = o_pb, o_pb + cb, o_pb + cb + DECAY_LORA, o_pb + 2 * cb + DECAY_LORA
    o_ad = o_vb + cb
    o_gd = o_ad + ICLR_LORA
    perm_ranges = [(o_q, qa), (o_r, cb), (o_kb, cb), (o_vb, cb), (o_k, kva), (o_v, kva),
                   (o_wd, DECAY_LORA), (o_ad, ICLR_LORA), (o_gd, GATE_LORA)]
    n_r, n_kb, n_vb = qa, qa + cb, qa + 2 * cb
    n_ka = qa + 3 * cb
    n_va = n_ka + kva
    n_lo = n_va + kva
    pb_pieces = [[(o_r, cb)], [(o_kb, cb)], [(o_vb, cb)], [(o_wd, DECAY_LORA), (o_ad, ICLR_LORA), (o_gd, GATE_LORA)]]

    def split_pb(z):
        return [jnp.concatenate([z[..., a - o_pb:a - o_pb + n] for a, n in piece], axis=-1) for piece in pb_pieces]

    def join_pb(r_, k_, v_, lo_):
        return jnp.concatenate([r_, lo_[..., :DECAY_LORA], k_, v_, lo_[..., DECAY_LORA:]], axis=-1)

    bias_a = _bias_a(t5_table)
    st = [[[] for _ in range(7)] for _ in range(2)]
    v_first = [None, None]
    for i in range(depth):
        j = i // 2
        last = i == depth - 1
        if i % 2 == 0:
            w_in = jnp.concatenate([w_in_e[j][:, a:a + n] for a, n in perm_ranges], axis=1).astype(BF16)
            wo = w_out_e[j].astype(BF16)
            mu_r, mu_k, mu_v, mu_lo = split_pb(b_mu[j])
            pvec = jnp.zeros((PV_ROWS, cb), F32)
            rows = {PV_MU_R: mu_r, PV_MU_K: mu_k, PV_MU_V: mu_v, PV_W0: b_w0[j],
                    PV_A0: b_a0[j], PV_KK: b_kk[j], PV_KA: b_ka[j], PV_LNW: b_lnx_w[j], PV_LNB: b_lnx_b[j],
                    PV_RK: b_rk[j].reshape(cb)}
            if j > 0:
                rows[PV_V0] = b_v0[j - 1]
            for ri, val in rows.items():
                pvec = pvec.at[ri].set(val)
            mu_l = mu_lo.reshape(1, LORA_W)
            w2p = _pad_rows(b_w2[j], 0, LORA_W).astype(BF16)
            a2p = _pad_rows(b_a2[j], DECAY_LORA, LORA_W).astype(BF16)
            g2p = _pad_rows(b_g2[j], DECAY_LORA + ICLR_LORA, LORA_W).astype(BF16)
            if j > 0:
                lora_v = b_v1.shape[2]
                v1p = jnp.zeros((cb, 128), F32).at[:, :lora_v].set(b_v1[j - 1]).astype(BF16)
                v2p = _pad_rows(b_v2[j - 1], 0, 128).astype(BF16)
            bias = bias_a
        else:
            w_in = w_in_o[j].astype(BF16)
            wo = w_out_o[j].astype(BF16)
            bias = _bias_c(c_rel_table[j])
        for gi in range(2):
            b, t = dims[gi]
            p = proj_in(xs[gi], norm_mix_g[i], w_in, tms[gi]).reshape(b, t, -1)
            if i % 2 == 0:
                attn_args = dict(n_heads=h_a, group=g_a, kcol=n_ka // kva, vcol=n_va // kva)
                if gi == 0:
                    shift0 = [jnp.zeros((b, 1, w_), F32) for w_ in (cb, cb, cb, LORA_W)]
                    s0 = jnp.zeros((b, h_b, HEAD_DIM, HEAD_DIM), F32)
                    o1 = attention_prompt(p, bias, a_sink[j], n_left=A_LEFT, **attn_args)
                else:
                    shift0 = [z[:, None, :] for z in split_pb(state_b_shift[j])]
                    s0 = state_b_wkv[j]
                    o1 = attention_sample(p, cache_a_k, cache_a_v, j, bias, a_sink[j], **attn_args)
                vres = None if j == 0 else (v1p, v2p, v_first[gi])
                o2, wkv, vf = rwkv_mixer(p, shift0, s0, pvec, mu_l, w2p, a2p, g2p, vres,
                                         cols=(n_r // cb, n_kb // cb, n_vb // cb, n_lo // LORA_W))
                if j == 0:
                    v_first[gi] = vf
                keep = min(A_LEFT * CHUNK, t)
                nk = p[:, t - keep:, n_ka:n_ka + kva].reshape(b, keep, KVH_A, HEAD_DIM)
                nv = p[:, t - keep:, n_va:n_va + kva].reshape(b, keep, KVH_A, HEAD_DIM)
                nshift = join_pb(*[p[:, t - 1, c0:c0 + w_] for c0, w_ in ((n_r, cb), (n_kb, cb), (n_vb, cb), (n_lo, LORA_W))])
                for s_list, val in zip(st[gi][0:4], (nk, nv, wkv, nshift)):
                    s_list.append(val)
            else:
                attn_args = dict(n_heads=h_c, group=1, kcol=1, vcol=2)
                if gi == 0:
                    conv_prev = None
                    o1 = attention_prompt(p, bias, None, n_left=C_LEFT, **attn_args)
                else:
                    conv_prev = state_d_conv[j]
                    o1 = attention_sample(p, cache_c_k, cache_c_v, j, bias, None, **attn_args)
                o2, nconv = conv_module(p, conv_prev, d_dw_w[j], d_dw_b[j], d_ln_g[j], d_ln_b[j],
                                        acol=3 * qc // cd, gcol=3 * qc // cd + 1, tq=conv_tq[gi])
                keep = min(C_LEFT * CHUNK, t)
                nk = p[:, t - keep:, qc:2 * qc].reshape(b, keep, h_c, HEAD_DIM)
                nv = p[:, t - keep:, 2 * qc:3 * qc].reshape(b, keep, h_c, HEAD_DIM)
                for s_list, val in zip(st[gi][4:7], (nk, nv, nconv)):
                    s_list.append(val)
            half = o1.shape[-1]
            xs[gi] = post(xs[gi], o1.reshape(-1, half), o2.reshape(-1, half), wo, norm_ffn_g[i], wg, wu, wd,
                          norm_final_g, i, tm=post_tms[gi], tf=tf, final_norm=last)
    y_prompt = xs[0].reshape(x_prompt.shape)
    y_sample = xs[1].reshape(x_sample.shape)
    (pak, pav, pbw, pbs, pck, pcv, pdc), (sak, sav, sbw, sbs, sck, scv, sdc) = [[jnp.stack(s) for s in g] for g in st]
    return (y_prompt, y_sample, pak, pav, pbw, pbs, pck, pcv, pdc, sak, sav, sbw, sbs, sck, scv, sdc)
```

```python
import functools
import math

import jax
import jax.numpy as jnp
import numpy as np
from jax import lax
from jax.experimental import pallas as pl
from jax.experimental.pallas import tpu as pltpu

F32 = jnp.float32
BF16 = jnp.bfloat16
HIGHEST = lax.Precision.HIGHEST

CHUNK = 64
HEAD_DIM = 64
RMS_EPS = 1e-6
NEG_INF = -1e30
LNX_EPS = 64e-5
LN_EPS = 1e-5
CONV_W = 31
T5_BUCKETS = 32
T5_MAX_DIST = 128
REL_CLIP = 128
A_LEFT = 2
C_LEFT = 8
KVH_A = 2
DECAY_LORA = 64
ICLR_LORA = 64
GATE_LORA = 128
LORA_W = DECAY_LORA + ICLR_LORA + GATE_LORA
CONV_HALO = 32
CONV_ROWS = 64
POST_ROWS = 256
PAIR_W = 2 * HEAD_DIM
ATTN_TQ = 256
ATTN_SAMPLE_BB = 4
ATTN_PROMPT_HEAD_GROUP = 4

VMEM_LIMIT = 56 * 1024 * 1024


def _cparams(sem):
    return pltpu.CompilerParams(dimension_semantics=sem, vmem_limit_bytes=VMEM_LIMIT)


def _batch_block(b, limit):
    return max(d for d in range(1, limit + 1) if b % d == 0)


def _rms(x, g):
    return x * lax.rsqrt(jnp.mean(x * x, axis=-1, keepdims=True) + RMS_EPS) * g


def _dot(a, b):
    return jnp.dot(a, b, preferred_element_type=F32)


def _dot_hp(a, b):
    return jnp.dot(a, b, precision=HIGHEST, preferred_element_type=F32)


def _dot_nt_hp(a, b):
    return lax.dot_general(a, b, (((1,), (1,)), ((), ())), precision=HIGHEST, preferred_element_type=F32)


def _dot_tn_hp(a, b):
    return lax.dot_general(a, b, (((0,), (0,)), ((), ())), precision=HIGHEST, preferred_element_type=F32)


def _proj_in_kernel(x_ref, g_ref, w_ref, o_ref):
    h = _rms(x_ref[...], g_ref[...]).astype(BF16)
    o_ref[...] = _dot(h, w_ref[...])


def proj_in(x2d, g, w_bf16, tm):
    m, d = x2d.shape
    n = w_bf16.shape[1]
    return pl.pallas_call(
        _proj_in_kernel,
        grid=(m // tm,),
        in_specs=[pl.BlockSpec((tm, d), lambda i: (i, 0)),
                  pl.BlockSpec((1, d), lambda i: (0, 0)),
                  pl.BlockSpec((d, n), lambda i: (0, 0))],
        out_specs=pl.BlockSpec((tm, n), lambda i: (i, 0)),
        out_shape=jax.ShapeDtypeStruct((m, n), F32),
        compiler_params=_cparams(("parallel",)),
        name="proj_in",
    )(x2d, g.reshape(1, d), w_bf16)


def _post_kernel(x_ref, o1_ref, o2_ref, wo_ref, g_ref, wg_ref, wu_ref, wd_ref, gf_ref, out_ref, h_s, *, final_norm):
    f = pl.program_id(1)
    half = o1_ref.shape[-1]
    tm = h_s.shape[0]
    rs = min(tm, POST_ROWS)

    @pl.when(f == 0)
    def _():
        for r0 in range(0, tm, rs):
            rows = slice(r0, r0 + rs)
            xn = (x_ref[rows, :] + _dot(o1_ref[rows, :], wo_ref[0:half, :])
                  + _dot(o2_ref[rows, :], wo_ref[half:2 * half, :]))
            out_ref[rows, :] = xn
            h_s[rows, :] = _rms(xn, g_ref[...]).astype(BF16)

    for r0 in range(0, tm, rs):
        rows = slice(r0, r0 + rs)
        h = h_s[rows, :]
        gate = _dot(h, wg_ref[...])
        up = _dot(h, wu_ref[...])
        act = gate * jax.nn.sigmoid(gate) * up
        out_ref[rows, :] += _dot(act.astype(BF16), wd_ref[...])

    if final_norm:
        @pl.when(f == pl.num_programs(1) - 1)
        def _():
            for r0 in range(0, tm, rs):
                rows = slice(r0, r0 + rs)
                out_ref[rows, :] = _rms(out_ref[rows, :], gf_ref[...])


def post(x2d, o1, o2, wo, g, wg, wu, wd, gf, layer, *, tm, tf, final_norm):
    m, d = x2d.shape
    half = o1.shape[-1]
    dff = wg.shape[2]
    return pl.pallas_call(
        functools.partial(_post_kernel, final_norm=final_norm),
        grid=(m // tm, dff // tf),
        in_specs=[pl.BlockSpec((tm, d), lambda i, f: (i, 0)),
                  pl.BlockSpec((tm, half), lambda i, f: (i, 0)),
                  pl.BlockSpec((tm, half), lambda i, f: (i, 0)),
                  pl.BlockSpec((2 * half, d), lambda i, f: (0, 0)),
                  pl.BlockSpec((1, d), lambda i, f: (0, 0)),
                  pl.BlockSpec((None, d, tf), lambda i, f: (layer, 0, f)),
                  pl.BlockSpec((None, d, tf), lambda i, f: (layer, 0, f)),
                  pl.BlockSpec((None, tf, d), lambda i, f: (layer, f, 0)),
                  pl.BlockSpec((1, d), lambda i, f: (0, 0))],
        out_specs=pl.BlockSpec((tm, d), lambda i, f: (i, 0)),
        out_shape=jax.ShapeDtypeStruct((m, d), F32),
        scratch_shapes=[pltpu.VMEM((tm, d), BF16)],
        compiler_params=_cparams(("parallel", "arbitrary")),
        name="post",
    )(x2d, o1, o2, wo, g.reshape(1, d), wg, wu, wd, gf.reshape(1, d))


def _attn_prompt_kernel(*refs, n_left, tq, n_heads, group, has_sink):
    q_ref, k_ref, v_ref, bias_ref = refs[:4]
    sink_ref = refs[4] if has_sink else None
    o_ref = refs[5] if has_sink else refs[4]
    scratch = refs[6:] if has_sink else refs[5:]
    hist = n_left * CHUNK
    nk = hist + tq
    i = pl.program_id(1)
    nt = (((1,), (1,)), ((), ()))
    slot = lax.rem(i, 2)
    k_win, vt_win = scratch[0].at[slot], scratch[1].at[slot]
    k_next, vt_next = scratch[0].at[1 - slot], scratch[1].at[1 - slot]
    k_swap = scratch[2].at[slot] if group > 1 else None
    k_swap_next = scratch[2].at[1 - slot] if group > 1 else None

    @pl.when(i == 0)
    def _():
        k_win[0:hist, :] = jnp.zeros((hist, k_win.shape[1]), BF16)
        vt_win[:, 0:hist] = jnp.zeros((vt_win.shape[0], hist), BF16)
        if group > 1:
            k_swap[0:hist, :] = jnp.zeros((hist, k_swap.shape[1]), BF16)

    k_new = k_ref[0]
    k_win[hist:nk, :] = k_new.astype(BF16)
    if group > 1:
        k_swap[hist:nk, :] = pltpu.roll(k_new, HEAD_DIM, axis=1).astype(BF16)
    vt_win[:, hist:nk] = v_ref[0].T.astype(BF16)
    valid = lax.broadcasted_iota(jnp.int32, (nk, tq), 0) >= hist - i * tq
    q = q_ref[0] * (HEAD_DIM ** -0.5)
    lane_lo = lax.broadcasted_iota(jnp.int32, (1, PAIR_W), 1) < HEAD_DIM
    lane_hi = jnp.logical_not(lane_lo)

    for h0 in range(0, n_heads, ATTN_PROMPT_HEAD_GROUP):
        hs = list(range(h0, h0 + ATTN_PROMPT_HEAD_GROUP))
        qh = [jnp.where(lane_lo if h % 2 == 0 else lane_hi, q[:, (h // 2) * PAIR_W:(h // 2 + 1) * PAIR_W], 0.0)
              .astype(BF16) for h in hs]
        if group > 1:
            own = [h // group for h in hs]
            kt = [(k_win if kv == h % 2 else k_swap)[...] for kv, h in zip(own, hs)]
            vt = [vt_win[...] for _ in hs]
        else:
            own = [h % 2 for h in hs]
            kt = [k_win[:, (h // 2) * PAIR_W:(h // 2 + 1) * PAIR_W] for h in hs]
            vt = [vt_win[(h // 2) * PAIR_W:(h // 2 + 1) * PAIR_W, :] for h in hs]
        s = [lax.dot_general(k_, q_, nt, preferred_element_type=F32) + bias_ref[h] for k_, q_, h in zip(kt, qh, hs)]
        s = [jnp.where(valid, z, NEG_INF) for z in s]
        m = [jnp.max(z, axis=0, keepdims=True) for z in s]
        if has_sink:
            sk = [sink_ref[h:h + 1, :] for h in hs]
            m = [jnp.maximum(z, k_) for z, k_ in zip(m, sk)]
        e = [jnp.exp(z - m_) for z, m_ in zip(s, m)]
        den = [jnp.sum(z, axis=0, keepdims=True) for z in e]
        if has_sink:
            den = [z + jnp.exp(k_ - m_) for z, k_, m_ in zip(den, sk, m)]
        ot = [_dot(v_, z.astype(BF16))[o_ * HEAD_DIM:(o_ + 1) * HEAD_DIM, :] / d_
              for v_, z, o_, d_ in zip(vt, e, own, den)]
        for j in range(0, len(hs), 2):
            pair = hs[j] // 2
            o_ref[0, :, pair * PAIR_W:(pair + 1) * PAIR_W] = (
                jnp.concatenate([ot[j], ot[j + 1]], axis=0).T.astype(o_ref.dtype))

    k_next[0:hist, :] = k_win[tq:nk, :]
    vt_next[:, 0:hist] = vt_win[:, tq:nk]
    if group > 1:
        k_swap_next[0:hist, :] = k_swap[tq:nk, :]


def attention_prompt(p, bias, sink, *, n_left, n_heads, group, kcol, vcol):
    b, t, _ = p.shape
    kvw = (n_heads // group) * HEAD_DIM
    qw = n_heads * HEAD_DIM
    has_sink = sink is not None
    assert group == 1 or kvw == PAIR_W, "grouped-query path assumes two kv heads sharing one lane tile"
    tq = min(ATTN_TQ, t)
    nk = n_left * CHUNK + tq
    bias_full = jnp.concatenate(
        [jnp.pad(bias, ((0, 0), (0, 0), (qc * CHUNK, nk - qc * CHUNK - bias.shape[2])), constant_values=NEG_INF)
         for qc in range(tq // CHUNK)], axis=1)
    bias_full = jnp.swapaxes(bias_full, 1, 2)
    sink_ops = [sink.reshape(n_heads, 1)] if has_sink else []
    sink_specs = [pl.BlockSpec((n_heads, 1), lambda bi, i: (0, 0))] if has_sink else []
    return pl.pallas_call(
        functools.partial(_attn_prompt_kernel, n_left=n_left, tq=tq, n_heads=n_heads, group=group, has_sink=has_sink),
        grid=(b, t // tq),
        in_specs=[pl.BlockSpec((1, tq, qw), lambda bi, i: (bi, i, 0)),
                  pl.BlockSpec((1, tq, kvw), lambda bi, i: (bi, i, kcol)),
                  pl.BlockSpec((1, tq, kvw), lambda bi, i: (bi, i, vcol)),
                  pl.BlockSpec(bias_full.shape, lambda bi, i: (0, 0, 0))] + sink_specs,
        out_specs=pl.BlockSpec((1, tq, qw), lambda bi, i: (bi, i, 0)),
        out_shape=jax.ShapeDtypeStruct((b, t, qw), BF16),
        scratch_shapes=([pltpu.VMEM((2, nk, kvw), BF16), pltpu.VMEM((2, kvw, nk), BF16)]
                        + ([pltpu.VMEM((2, nk, kvw), BF16)] if group > 1 else [])),
        compiler_params=_cparams(("parallel", "arbitrary")),
        name="attention_prompt",
    )(p, p, p, bias_full, *sink_ops)


def _attn_sample_kernel(*refs, n_heads, group, has_sink, bb):
    q_ref, kc_ref, vc_ref, kn_ref, vn_ref, bias_ref = refs[:6]
    sink_ref = refs[6] if has_sink else None
    o_ref = refs[-1]
    hist = kc_ref.shape[-1]
    nt = (((1,), (1,)), ((), ()))
    heads = range(n_heads)
    q_sl = [slice(h * HEAD_DIM, (h + 1) * HEAD_DIM) for h in heads]
    kv_sl = [slice((h // group) * HEAD_DIM, (h // group + 1) * HEAD_DIM) for h in heads]
    for bi in range(bb):
        q = (q_ref[bi] * (HEAD_DIM ** -0.5)).astype(BF16)
        kn = kn_ref[bi].astype(BF16)
        vn = vn_ref[bi].astype(BF16)
        qh = [q[:, sl] for sl in q_sl]
        s = [jnp.concatenate([_dot(qh[h], kc_ref[bi, h // group].astype(BF16)),
                              lax.dot_general(qh[h], kn[:, kv_sl[h]], nt, preferred_element_type=F32)], axis=1)
             + bias_ref[h] for h in heads]
        m = [jnp.max(z, axis=-1, keepdims=True) for z in s]
        if has_sink:
            sk = [sink_ref[h:h + 1, :] for h in heads]
            m = [jnp.maximum(z, k_) for z, k_ in zip(m, sk)]
        e = [jnp.exp(z - m_) for z, m_ in zip(s, m)]
        den = [jnp.sum(z, axis=-1, keepdims=True) for z in e]
        if has_sink:
            den = [z + jnp.exp(k_ - m_) for z, k_, m_ in zip(den, sk, m)]
        for h in heads:
            eb = e[h].astype(BF16)
            o = (lax.dot_general(eb[:, 0:hist], vc_ref[bi, h // group].astype(BF16), nt, preferred_element_type=F32)
                 + _dot(eb[:, hist:], vn[:, kv_sl[h]]))
            o_ref[bi, :, q_sl[h]] = (o / den[h]).astype(o_ref.dtype)


def attention_sample(p, k_cache, v_cache, layer, bias, sink, *, n_heads, group, kcol, vcol):
    b, t, _ = p.shape
    qw = n_heads * HEAD_DIM
    kv_heads = n_heads // group
    kvw = kv_heads * HEAD_DIM
    hist = k_cache.shape[2]
    has_sink = sink is not None
    bb = _batch_block(b, ATTN_SAMPLE_BB)
    transposed = lambda z: jnp.transpose(z, (0, 1, 3, 4, 2))
    cache_spec = pl.BlockSpec((None, bb, kv_heads, HEAD_DIM, hist), lambda i: (layer, i, 0, 0, 0))
    sink_ops = [sink.reshape(n_heads, 1)] if has_sink else []
    sink_specs = [pl.BlockSpec((n_heads, 1), lambda i: (0, 0))] if has_sink else []
    return pl.pallas_call(
        functools.partial(_attn_sample_kernel, n_heads=n_heads, group=group, has_sink=has_sink, bb=bb),
        grid=(b // bb,),
        in_specs=[pl.BlockSpec((bb, t, qw), lambda i: (i, 0, 0)), cache_spec, cache_spec,
                  pl.BlockSpec((bb, t, kvw), lambda i: (i, 0, kcol)),
                  pl.BlockSpec((bb, t, kvw), lambda i: (i, 0, vcol)),
                  pl.BlockSpec(bias.shape, lambda i: (0, 0, 0))] + sink_specs,
        out_specs=pl.BlockSpec((bb, t, qw), lambda i: (i, 0, 0)),
        out_shape=jax.ShapeDtypeStruct((b, t, qw), BF16),
        compiler_params=_cparams(("parallel",)),
        name="attention_sample",
    )(p, transposed(k_cache), transposed(v_cache), p, p, bias, *sink_ops)


def _conv_kernel(*refs, tq, from_state):
    if from_state:
        a_ref, gt_ref, st_in_ref, dw_ref, db_ref, lg_ref, lb_ref, o_ref, st_ref, buf = refs
    else:
        a_ref, gt_ref, pa_ref, pg_ref, dw_ref, db_ref, lg_ref, lb_ref, o_ref, st_ref, buf = refs
    keep = CONV_W - 1
    pad = CONV_HALO - keep
    if from_state:
        buf[0:pad, :] = jnp.zeros((pad, buf.shape[1]), F32)
        buf[pad:CONV_HALO, :] = st_in_ref[0]
    else:
        prev = pa_ref[0] * jax.nn.sigmoid(pg_ref[0])
        buf[0:CONV_HALO, :] = jnp.where(pl.program_id(1) > 0, prev, 0.0)
    buf[CONV_HALO:CONV_HALO + tq, :] = a_ref[0] * jax.nn.sigmoid(gt_ref[0])
    rs = min(tq, CONV_ROWS)
    for r0 in range(0, tq, rs):
        z = jnp.zeros((rs, buf.shape[1]), F32) + db_ref[...]
        base = buf[r0:r0 + rs + CONV_HALO, :]
        for s in range(8):
            rolled = pltpu.roll(base, rs + CONV_HALO - (pad + s), axis=0)
            for j, w in enumerate(range(s, CONV_W, 8)):
                z = z + rolled[8 * j:8 * j + rs, :] * dw_ref[w:w + 1, :]
        mean = jnp.mean(z, axis=-1, keepdims=True)
        zc = z - mean
        var = jnp.mean(zc * zc, axis=-1, keepdims=True)
        zn = zc * lax.rsqrt(var + LN_EPS) * lg_ref[...] + lb_ref[...]
        o_ref[0, r0:r0 + rs, :] = (zn * jax.nn.sigmoid(zn)).astype(o_ref.dtype)
    st_ref[0] = buf[tq + pad:tq + CONV_HALO, :]


def conv_module(p, state, dw_w, dw_b, ln_g, ln_b, *, acol, gcol, tq):
    b, t, _ = p.shape
    cd = dw_w.shape[1]
    keep = CONV_W - 1
    from_state = state is not None
    operands = [p, p]
    in_specs = [pl.BlockSpec((1, tq, cd), lambda bi, i: (bi, i, acol)),
                pl.BlockSpec((1, tq, cd), lambda bi, i: (bi, i, gcol))]
    if from_state:
        operands.append(state)
        in_specs.append(pl.BlockSpec((1, keep, cd), lambda bi, i: (bi, 0, 0)))
    else:
        r = tq // CONV_HALO
        for col in (acol, gcol):
            operands.append(p)
            in_specs.append(pl.BlockSpec((1, CONV_HALO, cd),
                                         lambda bi, i, col=col: (bi, jnp.maximum(i * r - 1, 0), col)))
    operands += [dw_w, dw_b.reshape(1, cd), ln_g.reshape(1, cd), ln_b.reshape(1, cd)]
    in_specs += [pl.BlockSpec((CONV_W, cd), lambda bi, i: (0, 0))] + [pl.BlockSpec((1, cd), lambda bi, i: (0, 0))] * 3
    return pl.pallas_call(
        functools.partial(_conv_kernel, tq=tq, from_state=from_state),
        grid=(b, t // tq),
        in_specs=in_specs,
        out_specs=[pl.BlockSpec((1, tq, cd), lambda bi, i: (bi, i, 0)),
                   pl.BlockSpec((1, keep, cd), lambda bi, i: (bi, 0, 0))],
        out_shape=[jax.ShapeDtypeStruct((b, t, cd), BF16), jax.ShapeDtypeStruct((b, keep, cd), F32)],
        scratch_shapes=[pltpu.VMEM((CONV_HALO + tq, cd), F32)],
        compiler_params=_cparams(("parallel", "arbitrary")),
        name="conv_module",
    )(*operands)


PV_MU_R, PV_MU_K, PV_MU_V, PV_W0, PV_A0, PV_KK, PV_KA, PV_V0, PV_LNW, PV_LNB, PV_RK = range(11)
PV_ROWS = 16
RWKV_BATCH_BLOCK = 4


def _shift_rows(x, first_row):
    rolled = pltpu.roll(x, 1, axis=0)
    row = lax.broadcasted_iota(jnp.int32, x.shape, 0)
    return jnp.where(row == 0, first_row, rolled)


def _rwkv_kernel(*refs, n_heads, has_vres, bb):
    (xr_ref, xk_ref, xv_ref, xl_ref, shr_ref, shk_ref, shv_ref, shl_ref, s0_ref, pv_ref, mul_ref,
     w2_ref, a2_ref, g2_ref) = refs[:14]
    pos = 14
    if has_vres:
        v1_ref, v2_ref, vf_ref = refs[pos:pos + 3]
        pos += 3
    ob_ref, sout_ref = refs[pos:pos + 2]
    pos += 2
    if not has_vres:
        vf_out_ref = refs[pos]
        pos += 1
    s_s, lr_s, lk_s, lv_s, ll_s = refs[pos:pos + 5]
    L = CHUNK
    c = pl.program_id(1)

    n_pairs = n_heads // 2
    PW = 2 * HEAD_DIM

    @pl.when(c == 0)
    def _():
        zero = jnp.zeros((HEAD_DIM, HEAD_DIM), F32)
        for bi in range(bb):
            for p in range(n_pairs):
                top = jnp.concatenate([s0_ref[bi, 2 * p], zero], axis=1)
                bottom = jnp.concatenate([zero, s0_ref[bi, 2 * p + 1]], axis=1)
                s_s[bi * n_pairs + p] = jnp.concatenate([top, bottom], axis=0)
            lr_s[bi, 0:1, :] = shr_ref[bi]
            lk_s[bi, 0:1, :] = shk_ref[bi]
            lv_s[bi, 0:1, :] = shv_ref[bi]
            ll_s[bi, 0:1, :] = shl_ref[bi]

    def token_shift(x_ref, last_s, mu, bi):
        x = x_ref[bi]
        prev = _shift_rows(x, last_s[bi, 0:1, :])
        last_s[bi, 0:1, :] = x[L - 1:L, :]
        return x + (prev - x) * mu

    pv = lambda i: pv_ref[i:i + 1, :]
    row = lax.broadcasted_iota(jnp.int32, (L, L), 0)
    col = lax.broadcasted_iota(jnp.int32, (L, L), 1)
    tri_incl = (row >= col).astype(F32)

    full = []
    for bi in range(bb):
        r = token_shift(xr_ref, lr_s, pv(PV_MU_R), bi)
        k = token_shift(xk_ref, lk_s, pv(PV_MU_K), bi)
        v = token_shift(xv_ref, lv_s, pv(PV_MU_V), bi)
        lo = token_shift(xl_ref, ll_s, mul_ref[...], bi)

        zw = pv(PV_W0) + _dot(jnp.tanh(lo).astype(BF16), w2_ref[...])
        w = -(jnp.maximum(-zw, 0.0) + jnp.log(1.0 + jnp.exp(-jnp.abs(zw)))) - 0.5
        d = -jnp.exp(w)
        iclr = jax.nn.sigmoid(pv(PV_A0) + _dot(lo.astype(BF16), a2_ref[...]))
        g = _dot(jax.nn.sigmoid(lo).astype(BF16), g2_ref[...])
        if has_vres:
            mix = jax.nn.sigmoid(pv(PV_V0) + _dot(_dot(v.astype(BF16), v1_ref[...]).astype(BF16), v2_ref[...]))
            v = v + (vf_ref[bi] - v) * mix
        else:
            vf_out_ref[bi] = v
        cs = _dot_hp(tri_incl, d)
        cprev = cs - d
        c_last = cs[L - 1:L, :]
        c_mid = cs[L // 2:L // 2 + 1, :]
        full.append(dict(
            r=r, v=v, g=g, iclr=iclr, kks=k * pv(PV_KK), kh=k * (1.0 + (iclr - 1.0) * pv(PV_KA)),
            e_start_prev=jnp.exp(cprev),
            e_start_incl=jnp.exp(cs),
            e_mid_prev=jnp.exp(cprev - c_mid), e_mid_incl=jnp.exp(cs - c_mid), e_from_mid=jnp.exp(c_mid - cs),
            e_to_end=jnp.exp(c_last - cs),
            e_chunk=jnp.exp(c_last)))

    lane_lo = lax.broadcasted_iota(jnp.int32, (1, PW), 1) < HEAD_DIM
    r2 = lax.broadcasted_iota(jnp.int32, (2 * L, 2 * L), 0)
    c2 = lax.broadcasted_iota(jnp.int32, (2 * L, 2 * L), 1)
    same_head = (r2 >= L) == (c2 >= L)
    strict_bd = same_head & (jnp.bitwise_and(r2, L - 1) > jnp.bitwise_and(c2, L - 1))
    eye_bd = (r2 == c2).astype(F32)
    incl_c = (lax.broadcasted_iota(jnp.int32, (L, 2 * L), 0)
              >= jnp.bitwise_and(lax.broadcasted_iota(jnp.int32, (L, 2 * L), 1), L - 1))

    def bd(x):
        return jnp.concatenate([jnp.where(lane_lo, x, 0.0), jnp.where(lane_lo, 0.0, x)], axis=0)

    def tile2(x):
        return jnp.concatenate([x, x], axis=0)

    def head_sum(x):
        lo_sum = jnp.sum(jnp.where(lane_lo, x, 0.0), axis=-1, keepdims=True)
        hi_sum = jnp.sum(jnp.where(lane_lo, 0.0, x), axis=-1, keepdims=True)
        return jnp.where(lane_lo, lo_sum, hi_sum)

    bf = lambda z: z.astype(BF16)
    dot_nt = lambda p_, q_: lax.dot_general(p_, q_, (((1,), (1,)), ((), ())), preferred_element_type=F32)
    dot_tn = lambda p_, q_: lax.dot_general(p_, q_, (((0,), (0,)), ((), ())), preferred_element_type=F32)

    units = [(bi, p) for bi in range(bb) for p in range(n_pairs)]
    sls = [slice(p * PW, (p + 1) * PW) for _, p in units]
    get = lambda name: [full[bi][name][:, sl] for (bi, _), sl in zip(units, sls)]
    mul = lambda xs_, ys_: [x_ * y_ for x_, y_ in zip(xs_, ys_)]

    kk = get("kks")
    kk = [z / jnp.maximum(jnp.sqrt(head_sum(z * z)), 1e-12) for z in kk]
    a_p = [-z for z in kk]
    b_p = mul(kk, get("iclr"))
    k_p, r_p, v_p = get("kh"), get("r"), get("v")
    a0 = mul(a_p, get("e_start_prev"))
    r0 = mul(r_p, get("e_start_incl"))
    at = mul(a_p, get("e_mid_prev"))
    rt = mul(r_p, get("e_mid_incl"))
    e_from_mid, e_to_end = get("e_from_mid"), get("e_to_end")
    bt, kt = mul(b_p, e_from_mid), mul(k_p, e_from_mid)
    bh, kh = mul(b_p, e_to_end), mul(k_p, e_to_end)

    cat0 = lambda x_, y_: jnp.concatenate([x_, y_], axis=0)
    cat1 = lambda x_, y_: jnp.concatenate([x_, y_], axis=1)
    at_bd = [bf(bd(z)) for z in at]
    g_a = [dot_nt(x_, bf(cat0(tile2(b_), tile2(k_)))) for x_, b_, k_ in zip(at_bd, bt, kt)]
    n_ab = [jnp.where(strict_bd, z[:, 0:PW], 0.0) for z in g_a]
    n_ak = [jnp.where(strict_bd, z[:, PW:2 * PW], 0.0) for z in g_a]
    n_r = [jnp.where(cat1(incl_c, incl_c), dot_nt(bf(x_), bf(cat0(bd(b_), bd(k_)))), 0.0)
           for x_, b_, k_ in zip(rt, bt, kt)]
    y_inv = [eye_bd + z for z in n_ab]
    pw = [_dot(bf(z), bf(z)) for z in n_ab]
    for _ in range(int(math.log2(L)) - 2):
        lvl = [_dot(bf(p_), bf(cat1(y_, p_))) for p_, y_ in zip(pw, y_inv)]
        y_inv = [y_ + z[:, 0:PW] for y_, z in zip(y_inv, lvl)]
        pw = [z[:, PW:2 * PW] for z in lvl]
    y_inv = [y_ + _dot(bf(p_), bf(y_)) for p_, y_ in zip(pw, y_inv)]
    v_bd = [bf(bd(z)) for z in v_p]
    t2 = [_dot(bf(x_), y_) for x_, y_ in zip(n_ak, v_bd)]
    w12 = [_dot(bf(x_), bf(cat1(bd(a_), t_))) for x_, a_, t_ in zip(y_inv, a0, t2)]
    s_old = [s_s[i] for i in range(len(units))]
    s_b = [bf(z) for z in s_old]
    u = [dot_nt(bf(z[:, 0:PW]), s_) + z[:, PW:2 * PW] for z, s_ in zip(w12, s_b)]
    y = [dot_nt(bf(r_), s_) + _dot(bf(n_), cat0(bf(u_), v_))
         for r_, s_, n_, u_, v_ in zip(r0, s_b, n_r, u, v_bd)]
    upd = [dot_tn(bf(jnp.concatenate([u_[0:L] + u_[L:2 * L], v_], axis=0)), bf(jnp.concatenate([b_, k_], axis=0)))
           for u_, v_, b_, k_ in zip(u, v_p, bh, kh)]
    for i, (s_, e_, d_) in enumerate(zip(s_old, get("e_chunk"), upd)):
        s_s[i] = s_ * e_ + jnp.where(same_head, d_, 0.0)
    g_p = get("g")
    for i, ((bi, _), sl) in enumerate(zip(units, sls)):
        mean = head_sum(y[i]) * (1.0 / HEAD_DIM)
        yc = y[i] - mean
        var = head_sum(yc * yc) * (1.0 / HEAD_DIM)
        yn = yc * lax.rsqrt(var + LNX_EPS) * pv(PV_LNW)[:, sl] + pv(PV_LNB)[:, sl]
        bonus = head_sum(r_p[i] * k_p[i] * pv(PV_RK)[:, sl])
        ob_ref[bi, :, sl] = ((yn + bonus * v_p[i]) * g_p[i]).astype(ob_ref.dtype)

    @pl.when(c == pl.num_programs(1) - 1)
    def _():
        for i, (bi, p) in enumerate(units):
            s_p = s_s[i]
            sout_ref[bi, 2 * p] = s_p[0:HEAD_DIM, 0:HEAD_DIM]
            sout_ref[bi, 2 * p + 1] = s_p[HEAD_DIM:PW, HEAD_DIM:PW]


def rwkv_mixer(p, shift0, s0, pvec, mu_l, w2p, a2p, g2p, vres, *, cols):
    b, t, _ = p.shape
    nc = t // CHUNK
    n_heads = s0.shape[1]
    cb = n_heads * HEAD_DIM
    rc, kc, vc, lc = cols
    has_vres = vres is not None
    bb = _batch_block(b, RWKV_BATCH_BLOCK)
    tok = lambda col, w: pl.BlockSpec((bb, CHUNK, w), lambda bi, c, col=col: (bi, c, col))
    const2 = lambda shape: pl.BlockSpec(shape, lambda bi, c: (0, 0))
    perb = lambda w: pl.BlockSpec((bb, 1, w), lambda bi, c: (bi, 0, 0))
    state_spec = pl.BlockSpec((bb, n_heads, HEAD_DIM, HEAD_DIM), lambda bi, c: (bi, 0, 0, 0))
    operands = [p, p, p, p, *shift0, s0, pvec, mu_l, w2p, a2p, g2p]
    in_specs = [tok(rc, cb), tok(kc, cb), tok(vc, cb), tok(lc, LORA_W),
                perb(cb), perb(cb), perb(cb), perb(LORA_W), state_spec,
                const2((PV_ROWS, cb)), const2((1, LORA_W)),
                const2((LORA_W, cb)), const2((LORA_W, cb)), const2((LORA_W, cb))]
    out_specs = [tok(0, cb), state_spec]
    out_shape = [jax.ShapeDtypeStruct((b, t, cb), BF16), jax.ShapeDtypeStruct(s0.shape, F32)]
    if has_vres:
        v1p, v2p, v_first = vres
        operands += [v1p, v2p, v_first]
        in_specs += [const2(v1p.shape), const2(v2p.shape), tok(0, cb)]
    else:
        out_specs.append(tok(0, cb))
        out_shape.append(jax.ShapeDtypeStruct((b, t, cb), F32))
    outs = pl.pallas_call(
        functools.partial(_rwkv_kernel, n_heads=n_heads, has_vres=has_vres, bb=bb),
        grid=(b // bb, nc),
        in_specs=in_specs,
        out_specs=out_specs,
        out_shape=out_shape,
        scratch_shapes=[pltpu.VMEM((bb * n_heads // 2, 2 * HEAD_DIM, 2 * HEAD_DIM), F32),
                        pltpu.VMEM((bb, 8, cb), F32), pltpu.VMEM((bb, 8, cb), F32), pltpu.VMEM((bb, 8, cb), F32),
                        pltpu.VMEM((bb, 8, LORA_W), F32)],
        compiler_params=_cparams(("parallel", "arbitrary")),
        name="rwkv_mixer",
    )(*operands)
    return (outs[0], outs[1], None) if has_vres else tuple(outs)


def _band_offsets(n_left):
    return np.arange(-(CHUNK - 1), (n_left + 1) * CHUNK) - n_left * CHUNK


def _lookup_static(table, idx):
    h = table.shape[0]
    pieces = []
    i, n = 0, len(idx)
    while i < n:
        j = i + 1
        step = int(idx[j] - idx[i]) if j < n else 0
        if step in (-1, 0, 1):
            while j < n and idx[j] - idx[j - 1] == step:
                j += 1
        first, last = int(idx[i]), int(idx[j - 1])
        if first == last:
            pieces.append(jnp.broadcast_to(table[:, first:first + 1], (h, j - i)))
        elif first < last:
            pieces.append(table[:, first:last + 1])
        else:
            pieces.append(jnp.flip(table[:, last:first + 1], axis=1))
        i = j
    return jnp.concatenate(pieces, axis=1)


def _toeplitz(e, n_rows, n_cols):
    h, n = e.shape
    z = jnp.concatenate([e, jnp.zeros((h, 1), e.dtype)], axis=1)
    shifted = jnp.tile(z, (1, n_rows))[:, :n_rows * n].reshape(h, n_rows, n)
    return shifted[:, :, n_rows - 1:n_rows - 1 + n_cols]


def _t5_bucket(rel):
    nb = T5_BUCKETS // 2
    exact = nb // 2
    n = np.abs(rel)
    nf = np.maximum(n, exact).astype(np.float32)
    large = exact + (np.log(nf / exact) / math.log(T5_MAX_DIST / exact) * (nb - exact)).astype(np.int32)
    return np.where(rel > 0, nb, 0) + np.where(n < exact, n, np.minimum(large, nb - 1))


def _bias_a(t5_table):
    e = _lookup_static(t5_table.T.astype(F32), _t5_bucket(_band_offsets(A_LEFT)))
    return _toeplitz(e, CHUNK, (A_LEFT + 1) * CHUNK)


def _bias_c(rel_table):
    idx = np.clip(-_band_offsets(C_LEFT), -REL_CLIP, REL_CLIP) + REL_CLIP
    return _toeplitz(_lookup_static(rel_table.astype(F32), idx), CHUNK, (C_LEFT + 1) * CHUNK)


def _pad_rows(w, start, total):
    return jnp.zeros((total, w.shape[1]), w.dtype).at[start:start + w.shape[0]].set(w)


def kernel(x_prompt, x_sample, cache_a_k, cache_a_v, state_b_wkv, state_b_shift, cache_c_k, cache_c_v, state_d_conv, norm_mix_g, norm_ffn_g, norm_final_g, t5_table, w_in_e, w_out_e, a_sink, b_mu, b_w0, b_w2, b_a0, b_a2, b_g2, b_kk, b_ka, b_rk, b_lnx_w, b_lnx_b, b_v0, b_v1, b_v2, w_in_o, w_out_o, c_rel_table, d_dw_w, d_dw_b, d_ln_g, d_ln_b, ffn_w_gate, ffn_w_up, ffn_w_down):
    depth, d_model = norm_mix_g.shape
    n_even = w_in_e.shape[0]
    h_a = a_sink.shape[1]
    g_a = h_a // KVH_A
    qa = h_a * HEAD_DIM
    kva = KVH_A * HEAD_DIM
    h_b = state_b_wkv.shape[2]
    cb = h_b * HEAD_DIM
    h_c = c_rel_table.shape[1]
    qc = h_c * HEAD_DIM
    cd = d_dw_w.shape[2]
    d_ff = ffn_w_gate.shape[2]
    tf = d_ff // 2 if (d_ff // 2) % 128 == 0 else d_ff

    groups = [x_prompt, x_sample]
    dims = [x.shape[:2] for x in groups]
    xs = [x.reshape(-1, d_model) for x in groups]
    tms = [min(512, x.shape[0]) for x in xs]
    post_tms = [min(1024, x.shape[0]) for x in xs]
    wg, wu, wd = ffn_w_gate.astype(BF16), ffn_w_up.astype(BF16), ffn_w_down.astype(BF16)
    conv_tq = [min(512, t) for _, t in dims]

    o_q, o_k, o_v, o_pb = 0, qa, qa + kva, qa + 2 * kva
    o_r, o_wd, o_kb, o_vb = o_pb, o_pb + cb, o_pb + cb + DECAY_LORA, o_pb + 2 * cb + DECAY_LORA
    o_ad = o_vb + cb
    o_gd = o_ad + ICLR_LORA
    perm_ranges = [(o_q, qa), (o_r, cb), (o_kb, cb), (o_vb, cb), (o_k, kva), (o_v, kva),
                   (o_wd, DECAY_LORA), (o_ad, ICLR_LORA), (o_gd, GATE_LORA)]
    n_r, n_kb, n_vb = qa, qa + cb, qa + 2 * cb
    n_ka = qa + 3 * cb
    n_va = n_ka + kva
    n_lo = n_va + kva
    pb_pieces = [[(o_r, cb)], [(o_kb, cb)], [(o_vb, cb)], [(o_wd, DECAY_LORA), (o_ad, ICLR_LORA), (o_gd, GATE_LORA)]]

    def split_pb(z):
        return [jnp.concatenate([z[..., a - o_pb:a - o_pb + n] for a, n in piece], axis=-1) for piece in pb_pieces]

    def join_pb(r_, k_, v_, lo_):
        return jnp.concatenate([r_, lo_[..., :DECAY_LORA], k_, v_, lo_[..., DECAY_LORA:]], axis=-1)

    bias_a = _bias_a(t5_table)
    st = [[[] for _ in range(7)] for _ in range(2)]
    v_first = [None, None]
    for i in range(depth):
        j = i // 2
        last = i == depth - 1
        if i % 2 == 0:
            w_in = jnp.concatenate([w_in_e[j][:, a:a + n] for a, n in perm_ranges], axis=1).astype(BF16)
            wo = w_out_e[j].astype(BF16)
            mu_r, mu_k, mu_v, mu_lo = split_pb(b_mu[j])
            pvec = jnp.zeros((PV_ROWS, cb), F32)
            rows = {PV_MU_R: mu_r, PV_MU_K: mu_k, PV_MU_V: mu_v, PV_W0: b_w0[j],
                    PV_A0: b_a0[j], PV_KK: b_kk[j], PV_KA: b_ka[j], PV_LNW: b_lnx_w[j], PV_LNB: b_lnx_b[j],
                    PV_RK: b_rk[j].reshape(cb)}
            if j > 0:
                rows[PV_V0] = b_v0[j - 1]
            for ri, val in rows.items():
                pvec = pvec.at[ri].set(val)
            mu_l = mu_lo.reshape(1, LORA_W)
            w2p = _pad_rows(b_w2[j], 0, LORA_W).astype(BF16)
            a2p = _pad_rows(b_a2[j], DECAY_LORA, LORA_W).astype(BF16)
            g2p = _pad_rows(b_g2[j], DECAY_LORA + ICLR_LORA, LORA_W).astype(BF16)
            if j > 0:
                lora_v = b_v1.shape[2]
                v1p = jnp.zeros((cb, 128), F32).at[:, :lora_v].set(b_v1[j - 1]).astype(BF16)
                v2p = _pad_rows(b_v2[j - 1], 0, 128).astype(BF16)
            bias = bias_a
        else:
            w_in = w_in_o[j].astype(BF16)
            wo = w_out_o[j].astype(BF16)
            bias = _bias_c(c_rel_table[j])
        for gi in range(2):
            b, t = dims[gi]
            p = proj_in(xs[gi], norm_mix_g[i], w_in, tms[gi]).reshape(b, t, -1)
            if i % 2 == 0:
                attn_args = dict(n_heads=h_a, group=g_a, kcol=n_ka // kva, vcol=n_va // kva)
                if gi == 0:
                    shift0 = [jnp.zeros((b, 1, w_), F32) for w_ in (cb, cb, cb, LORA_W)]
                    s0 = jnp.zeros((b, h_b, HEAD_DIM, HEAD_DIM), F32)
                    o1 = attention_prompt(p, bias, a_sink[j], n_left=A_LEFT, **attn_args)
                else:
                    shift0 = [z[:, None, :] for z in split_pb(state_b_shift[j])]
                    s0 = state_b_wkv[j]
                    o1 = attention_sample(p, cache_a_k, cache_a_v, j, bias, a_sink[j], **attn_args)
                vres = None if j == 0 else (v1p, v2p, v_first[gi])
                o2, wkv, vf = rwkv_mixer(p, shift0, s0, pvec, mu_l, w2p, a2p, g2p, vres,
                                         cols=(n_r // cb, n_kb // cb, n_vb // cb, n_lo // LORA_W))
                if j == 0:
                    v_first[gi] = vf
                keep = min(A_LEFT * CHUNK, t)
                nk = p[:, t - keep:, n_ka:n_ka + kva].reshape(b, keep, KVH_A, HEAD_DIM)
                nv = p[:, t - keep:, n_va:n_va + kva].reshape(b, keep, KVH_A, HEAD_DIM)
                nshift = join_pb(*[p[:, t - 1, c0:c0 + w_] for c0, w_ in ((n_r, cb), (n_kb, cb), (n_vb, cb), (n_lo, LORA_W))])
                for s_list, val in zip(st[gi][0:4], (nk, nv, wkv, nshift)):
                    s_list.append(val)
            else:
                attn_args = dict(n_heads=h_c, group=1, kcol=1, vcol=2)
                if gi == 0:
                    conv_prev = None
                    o1 = attention_prompt(p, bias, None, n_left=C_LEFT, **attn_args)
                else:
                    conv_prev = state_d_conv[j]
                    o1 = attention_sample(p, cache_c_k, cache_c_v, j, bias, None, **attn_args)
                o2, nconv = conv_module(p, conv_prev, d_dw_w[j], d_dw_b[j], d_ln_g[j], d_ln_b[j],
                                        acol=3 * qc // cd, gcol=3 * qc // cd + 1, tq=conv_tq[gi])
                keep = min(C_LEFT * CHUNK, t)
                nk = p[:, t - keep:, qc:2 * qc].reshape(b, keep, h_c, HEAD_DIM)
                nv = p[:, t - keep:, 2 * qc:3 * qc].reshape(b, keep, h_c, HEAD_DIM)
                for s_list, val in zip(st[gi][4:7], (nk, nv, nconv)):
                    s_list.append(val)
            half = o1.shape[-1]
            xs[gi] = post(xs[gi], o1.reshape(-1, half), o2.reshape(-1, half), wo, norm_ffn_g[i], wg, wu, wd,
                          norm_final_g, i, tm=post_tms[gi], tf=tf, final_norm=last)
    y_prompt = xs[0].reshape(x_prompt.shape)
    y_sample = xs[1].reshape(x_sample.shape)
    (pak, pav, pbw, pbs, pck, pcv, pdc), (sak, sav, sbw, sbs, sck, scv, sdc) = [[jnp.stack(s) for s in g] for g in st]
    return (y_prompt, y_sample, pak, pav, pbw, pbs, pck, pcv, pdc, sak, sav, sbw, sbs, sck, scv, sdc)
```

```python
import functools
import math

import jax
import jax.numpy as jnp
import numpy as np
from jax import lax
from jax.experimental import pallas as pl
from jax.experimental.pallas import tpu as pltpu

F32 = jnp.float32
BF16 = jnp.bfloat16
HIGHEST = lax.Precision.HIGHEST

CHUNK = 64
HEAD_DIM = 64
RMS_EPS = 1e-6
NEG_INF = -1e30
LNX_EPS = 64e-5
LN_EPS = 1e-5
CONV_W = 31
T5_BUCKETS = 32
T5_MAX_DIST = 128
REL_CLIP = 128
A_LEFT = 2
C_LEFT = 8
KVH_A = 2
DECAY_LORA = 64
ICLR_LORA = 64
GATE_LORA = 128
LORA_W = DECAY_LORA + ICLR_LORA + GATE_LORA
CONV_HALO = 32
CONV_ROWS = 64
LOG2E = math.log2(math.e)
QK_SCALE = HEAD_DIM ** -0.5 * LOG2E
POST_ROWS = 256
PAIR_W = 2 * HEAD_DIM
ATTN_TQ = 256
ATTN_SAMPLE_BB = 4
ATTN_PROMPT_HEAD_GROUP = 4

VMEM_LIMIT = 56 * 1024 * 1024


def _cparams(sem):
    return pltpu.CompilerParams(dimension_semantics=sem, vmem_limit_bytes=VMEM_LIMIT)


def _batch_block(b, limit):
    return max(d for d in range(1, limit + 1) if b % d == 0)


def _rms(x, g):
    return x * lax.rsqrt(jnp.mean(x * x, axis=-1, keepdims=True) + RMS_EPS) * g


def _dot(a, b):
    return jnp.dot(a, b, preferred_element_type=F32)


def _dot_hp(a, b):
    return jnp.dot(a, b, precision=HIGHEST, preferred_element_type=F32)


def _dot_nt_hp(a, b):
    return lax.dot_general(a, b, (((1,), (1,)), ((), ())), precision=HIGHEST, preferred_element_type=F32)


def _dot_tn_hp(a, b):
    return lax.dot_general(a, b, (((0,), (0,)), ((), ())), precision=HIGHEST, preferred_element_type=F32)


def _proj_in_kernel(x_ref, g_ref, w_ref, o_ref):
    h = _rms(x_ref[...], g_ref[...]).astype(BF16)
    o_ref[...] = _dot(h, w_ref[...])


def proj_in(x2d, g, w_bf16, tm):
    m, d = x2d.shape
    n = w_bf16.shape[1]
    return pl.pallas_call(
        _proj_in_kernel,
        grid=(m // tm,),
        in_specs=[pl.BlockSpec((tm, d), lambda i: (i, 0)),
                  pl.BlockSpec((1, d), lambda i: (0, 0)),
                  pl.BlockSpec((d, n), lambda i: (0, 0))],
        out_specs=pl.BlockSpec((tm, n), lambda i: (i, 0)),
        out_shape=jax.ShapeDtypeStruct((m, n), F32),
        compiler_params=_cparams(("parallel",)),
        name="proj_in",
    )(x2d, g.reshape(1, d), w_bf16)


def _post_kernel(x_ref, o1_ref, o2_ref, wo_ref, g_ref, wg_ref, wu_ref, wd_ref, gf_ref, out_ref, h_s, *, final_norm):
    f = pl.program_id(1)
    half = o1_ref.shape[-1]
    tm = h_s.shape[0]
    rs = min(tm, POST_ROWS)

    @pl.when(f == 0)
    def _():
        for r0 in range(0, tm, rs):
            rows = slice(r0, r0 + rs)
            xn = (x_ref[rows, :] + _dot(o1_ref[rows, :], wo_ref[0:half, :])
                  + _dot(o2_ref[rows, :], wo_ref[half:2 * half, :]))
            out_ref[rows, :] = xn
            h_s[rows, :] = _rms(xn, g_ref[...]).astype(BF16)

    for r0 in range(0, tm, rs):
        rows = slice(r0, r0 + rs)
        h = h_s[rows, :]
        gate = _dot(h, wg_ref[...])
        up = _dot(h, wu_ref[...])
        act = gate * jax.nn.sigmoid(gate) * up
        out_ref[rows, :] += _dot(act.astype(BF16), wd_ref[...])

    if final_norm:
        @pl.when(f == pl.num_programs(1) - 1)
        def _():
            for r0 in range(0, tm, rs):
                rows = slice(r0, r0 + rs)
                out_ref[rows, :] = _rms(out_ref[rows, :], gf_ref[...])


def post(x2d, o1, o2, wo, g, wg, wu, wd, gf, layer, *, tm, tf, final_norm):
    m, d = x2d.shape
    half = o1.shape[-1]
    dff = wg.shape[2]
    return pl.pallas_call(
        functools.partial(_post_kernel, final_norm=final_norm),
        grid=(m // tm, dff // tf),
        in_specs=[pl.BlockSpec((tm, d), lambda i, f: (i, 0)),
                  pl.BlockSpec((tm, half), lambda i, f: (i, 0)),
                  pl.BlockSpec((tm, half), lambda i, f: (i, 0)),
                  pl.BlockSpec((2 * half, d), lambda i, f: (0, 0)),
                  pl.BlockSpec((1, d), lambda i, f: (0, 0)),
                  pl.BlockSpec((None, d, tf), lambda i, f: (layer, 0, f)),
                  pl.BlockSpec((None, d, tf), lambda i, f: (layer, 0, f)),
                  pl.BlockSpec((None, tf, d), lambda i, f: (layer, f, 0)),
                  pl.BlockSpec((1, d), lambda i, f: (0, 0))],
        out_specs=pl.BlockSpec((tm, d), lambda i, f: (i, 0)),
        out_shape=jax.ShapeDtypeStruct((m, d), F32),
        scratch_shapes=[pltpu.VMEM((tm, d), BF16)],
        compiler_params=_cparams(("parallel", "arbitrary")),
        name="post",
    )(x2d, o1, o2, wo, g.reshape(1, d), wg, wu, wd, gf.reshape(1, d))


def _attn_prompt_kernel(*refs, n_left, tq, n_heads, group, has_sink):
    q_ref, k_ref, v_ref, bias_ref = refs[:4]
    sink_ref = refs[4] if has_sink else None
    o_ref = refs[5] if has_sink else refs[4]
    scratch = refs[6:] if has_sink else refs[5:]
    hist = n_left * CHUNK
    nk = hist + tq
    i = pl.program_id(1)
    nt = (((1,), (1,)), ((), ()))
    slot = lax.rem(i, 2)
    k_win, vt_win = scratch[0].at[slot], scratch[1].at[slot]
    k_next, vt_next = scratch[0].at[1 - slot], scratch[1].at[1 - slot]
    k_swap = scratch[2].at[slot] if group > 1 else None
    k_swap_next = scratch[2].at[1 - slot] if group > 1 else None

    @pl.when(i == 0)
    def _():
        k_win[0:hist, :] = jnp.zeros((hist, k_win.shape[1]), BF16)
        vt_win[:, 0:hist] = jnp.zeros((vt_win.shape[0], hist), BF16)
        if group > 1:
            k_swap[0:hist, :] = jnp.zeros((hist, k_swap.shape[1]), BF16)

    k_new = k_ref[0]
    k_win[hist:nk, :] = k_new.astype(BF16)
    if group > 1:
        k_swap[hist:nk, :] = pltpu.roll(k_new, HEAD_DIM, axis=1).astype(BF16)
    vt_win[:, hist:nk] = v_ref[0].T.astype(BF16)
    q = q_ref[0] * QK_SCALE
    lane_lo = lax.broadcasted_iota(jnp.int32, (1, PAIR_W), 1) < HEAD_DIM
    lane_hi = jnp.logical_not(lane_lo)

    for h0 in range(0, n_heads, ATTN_PROMPT_HEAD_GROUP):
        hs = list(range(h0, h0 + ATTN_PROMPT_HEAD_GROUP))
        qh = [jnp.where(lane_lo if h % 2 == 0 else lane_hi, q[:, (h // 2) * PAIR_W:(h // 2 + 1) * PAIR_W], 0.0)
              .astype(BF16) for h in hs]
        if group > 1:
            own = [h // group for h in hs]
            kt = [(k_win if kv == h % 2 else k_swap)[...] for kv, h in zip(own, hs)]
            vt = [vt_win[...] for _ in hs]
        else:
            own = [h % 2 for h in hs]
            kt = [k_win[:, (h // 2) * PAIR_W:(h // 2 + 1) * PAIR_W] for h in hs]
            vt = [vt_win[(h // 2) * PAIR_W:(h // 2 + 1) * PAIR_W, :] for h in hs]
        s = [lax.dot_general(k_, q_, nt, preferred_element_type=F32) + bias_ref[h] for k_, q_, h in zip(kt, qh, hs)]
        m = [jnp.max(z, axis=0, keepdims=True) for z in s]
        if has_sink:
            sk = [sink_ref[h:h + 1, :] for h in hs]
            m = [jnp.maximum(z, k_) for z, k_ in zip(m, sk)]
        e = [jnp.exp2(z - m_) for z, m_ in zip(s, m)]
        den = [jnp.sum(z, axis=0, keepdims=True) for z in e]
        if has_sink:
            den = [z + jnp.exp2(k_ - m_) for z, k_, m_ in zip(den, sk, m)]
        ot = [_dot(v_, z.astype(BF16))[o_ * HEAD_DIM:(o_ + 1) * HEAD_DIM, :] / d_
              for v_, z, o_, d_ in zip(vt, e, own, den)]
        for j in range(0, len(hs), 2):
            pair = hs[j] // 2
            o_ref[0, :, pair * PAIR_W:(pair + 1) * PAIR_W] = (
                jnp.concatenate([ot[j], ot[j + 1]], axis=0).T.astype(o_ref.dtype))

    k_next[0:hist, :] = k_win[tq:nk, :]
    vt_next[:, 0:hist] = vt_win[:, tq:nk]
    if group > 1:
        k_swap_next[0:hist, :] = k_swap[tq:nk, :]


def attention_prompt(p, bias, sink, *, n_left, n_heads, group, kcol, vcol):
    b, t, _ = p.shape
    kvw = (n_heads // group) * HEAD_DIM
    qw = n_heads * HEAD_DIM
    has_sink = sink is not None
    assert group == 1 or kvw == PAIR_W, "grouped-query path assumes two kv heads sharing one lane tile"
    tq = min(ATTN_TQ, t)
    nk = n_left * CHUNK + tq
    bias_full = jnp.concatenate(
        [jnp.pad(bias, ((0, 0), (0, 0), (qc * CHUNK, nk - qc * CHUNK - bias.shape[2])), constant_values=NEG_INF)
         for qc in range(tq // CHUNK)], axis=1)
    bias_full = jnp.swapaxes(bias_full, 1, 2)
    hist = n_left * CHUNK
    n_var = -(-hist // tq) + 1
    row = jnp.arange(nk)[None, :, None]
    bias_var = jnp.stack([jnp.where(row >= hist - v * tq, bias_full, NEG_INF) for v in range(n_var - 1)] + [bias_full])
    sink_ops = [sink.reshape(n_heads, 1)] if has_sink else []
    sink_specs = [pl.BlockSpec((n_heads, 1), lambda bi, i: (0, 0))] if has_sink else []
    return pl.pallas_call(
        functools.partial(_attn_prompt_kernel, n_left=n_left, tq=tq, n_heads=n_heads, group=group, has_sink=has_sink),
        grid=(b, t // tq),
        in_specs=[pl.BlockSpec((1, tq, qw), lambda bi, i: (bi, i, 0)),
                  pl.BlockSpec((1, tq, kvw), lambda bi, i: (bi, i, kcol)),
                  pl.BlockSpec((1, tq, kvw), lambda bi, i: (bi, i, vcol)),
                  pl.BlockSpec((None,) + bias_full.shape, lambda bi, i: (jnp.minimum(i, n_var - 1), 0, 0, 0))]
        + sink_specs,
        out_specs=pl.BlockSpec((1, tq, qw), lambda bi, i: (bi, i, 0)),
        out_shape=jax.ShapeDtypeStruct((b, t, qw), BF16),
        scratch_shapes=([pltpu.VMEM((2, nk, kvw), BF16), pltpu.VMEM((2, kvw, nk), BF16)]
                        + ([pltpu.VMEM((2, nk, kvw), BF16)] if group > 1 else [])),
        compiler_params=_cparams(("parallel", "arbitrary")),
        name="attention_prompt",
    )(p, p, p, bias_var, *sink_ops)


def _attn_sample_kernel(*refs, n_heads, group, has_sink, bb):
    q_ref, kc_ref, vc_ref, kn_ref, vn_ref, bias_ref = refs[:6]
    sink_ref = refs[6] if has_sink else None
    o_ref = refs[-1]
    hist = kc_ref.shape[-1]
    nt = (((1,), (1,)), ((), ()))
    heads = range(n_heads)
    q_sl = [slice(h * HEAD_DIM, (h + 1) * HEAD_DIM) for h in heads]
    kv_sl = [slice((h // group) * HEAD_DIM, (h // group + 1) * HEAD_DIM) for h in heads]
    for bi in range(bb):
        q = (q_ref[bi] * QK_SCALE).astype(BF16)
        kn = kn_ref[bi].astype(BF16)
        vn = vn_ref[bi].astype(BF16)
        qh = [q[:, sl] for sl in q_sl]
        s = [jnp.concatenate([_dot(qh[h], kc_ref[bi, h // group].astype(BF16)),
                              lax.dot_general(qh[h], kn[:, kv_sl[h]], nt, preferred_element_type=F32)], axis=1)
             + bias_ref[h] for h in heads]
        m = [jnp.max(z, axis=-1, keepdims=True) for z in s]
        if has_sink:
            sk = [sink_ref[h:h + 1, :] for h in heads]
            m = [jnp.maximum(z, k_) for z, k_ in zip(m, sk)]
        e = [jnp.exp2(z - m_) for z, m_ in zip(s, m)]
        den = [jnp.sum(z, axis=-1, keepdims=True) for z in e]
        if has_sink:
            den = [z + jnp.exp2(k_ - m_) for z, k_, m_ in zip(den, sk, m)]
        for h in heads:
            eb = e[h].astype(BF16)
            o = (lax.dot_general(eb[:, 0:hist], vc_ref[bi, h // group].astype(BF16), nt, preferred_element_type=F32)
                 + _dot(eb[:, hist:], vn[:, kv_sl[h]]))
            o_ref[bi, :, q_sl[h]] = (o / den[h]).astype(o_ref.dtype)


def attention_sample(p, k_cache, v_cache, layer, bias, sink, *, n_heads, group, kcol, vcol):
    b, t, _ = p.shape
    qw = n_heads * HEAD_DIM
    kv_heads = n_heads // group
    kvw = kv_heads * HEAD_DIM
    hist = k_cache.shape[2]
    has_sink = sink is not None
    bb = _batch_block(b, ATTN_SAMPLE_BB)
    transposed = lambda z: jnp.transpose(z, (0, 1, 3, 4, 2))
    cache_spec = pl.BlockSpec((None, bb, kv_heads, HEAD_DIM, hist), lambda i: (layer, i, 0, 0, 0))
    sink_ops = [sink.reshape(n_heads, 1)] if has_sink else []
    sink_specs = [pl.BlockSpec((n_heads, 1), lambda i: (0, 0))] if has_sink else []
    return pl.pallas_call(
        functools.partial(_attn_sample_kernel, n_heads=n_heads, group=group, has_sink=has_sink, bb=bb),
        grid=(b // bb,),
        in_specs=[pl.BlockSpec((bb, t, qw), lambda i: (i, 0, 0)), cache_spec, cache_spec,
                  pl.BlockSpec((bb, t, kvw), lambda i: (i, 0, kcol)),
                  pl.BlockSpec((bb, t, kvw), lambda i: (i, 0, vcol)),
                  pl.BlockSpec(bias.shape, lambda i: (0, 0, 0))] + sink_specs,
        out_specs=pl.BlockSpec((bb, t, qw), lambda i: (i, 0, 0)),
        out_shape=jax.ShapeDtypeStruct((b, t, qw), BF16),
        compiler_params=_cparams(("parallel",)),
        name="attention_sample",
    )(p, transposed(k_cache), transposed(v_cache), p, p, bias, *sink_ops)


def _conv_kernel(*refs, tq, from_state):
    if from_state:
        a_ref, gt_ref, st_in_ref, dw_ref, db_ref, lg_ref, lb_ref, o_ref, st_ref, buf = refs
    else:
        a_ref, gt_ref, pa_ref, pg_ref, dw_ref, db_ref, lg_ref, lb_ref, o_ref, st_ref, buf = refs
    keep = CONV_W - 1
    pad = CONV_HALO - keep
    if from_state:
        buf[0:pad, :] = jnp.zeros((pad, buf.shape[1]), F32)
        buf[pad:CONV_HALO, :] = st_in_ref[0]
    else:
        prev = pa_ref[0] * jax.nn.sigmoid(pg_ref[0])
        buf[0:CONV_HALO, :] = jnp.where(pl.program_id(1) > 0, prev, 0.0)
    buf[CONV_HALO:CONV_HALO + tq, :] = a_ref[0] * jax.nn.sigmoid(gt_ref[0])
    rs = min(tq, CONV_ROWS)
    for r0 in range(0, tq, rs):
        z = jnp.zeros((rs, buf.shape[1]), F32) + db_ref[...]
        base = buf[r0:r0 + rs + CONV_HALO, :]
        for s in range(8):
            rolled = pltpu.roll(base, rs + CONV_HALO - (pad + s), axis=0)
            for j, w in enumerate(range(s, CONV_W, 8)):
                z = z + rolled[8 * j:8 * j + rs, :] * dw_ref[w:w + 1, :]
        mean = jnp.mean(z, axis=-1, keepdims=True)
        zc = z - mean
        var = jnp.mean(zc * zc, axis=-1, keepdims=True)
        zn = zc * lax.rsqrt(var + LN_EPS) * lg_ref[...] + lb_ref[...]
        o_ref[0, r0:r0 + rs, :] = (zn * jax.nn.sigmoid(zn)).astype(o_ref.dtype)
    st_ref[0] = buf[tq + pad:tq + CONV_HALO, :]


def conv_module(p, state, dw_w, dw_b, ln_g, ln_b, *, acol, gcol, tq):
    b, t, _ = p.shape
    cd = dw_w.shape[1]
    keep = CONV_W - 1
    from_state = state is not None
    operands = [p, p]
    in_specs = [pl.BlockSpec((1, tq, cd), lambda bi, i: (bi, i, acol)),
                pl.BlockSpec((1, tq, cd), lambda bi, i: (bi, i, gcol))]
    if from_state:
        operands.append(state)
        in_specs.append(pl.BlockSpec((1, keep, cd), lambda bi, i: (bi, 0, 0)))
    else:
        r = tq // CONV_HALO
        for col in (acol, gcol):
            operands.append(p)
            in_specs.append(pl.BlockSpec((1, CONV_HALO, cd),
                                         lambda bi, i, col=col: (bi, jnp.maximum(i * r - 1, 0), col)))
    operands += [dw_w, dw_b.reshape(1, cd), ln_g.reshape(1, cd), ln_b.reshape(1, cd)]
    in_specs += [pl.BlockSpec((CONV_W, cd), lambda bi, i: (0, 0))] + [pl.BlockSpec((1, cd), lambda bi, i: (0, 0))] * 3
    return pl.pallas_call(
        functools.partial(_conv_kernel, tq=tq, from_state=from_state),
        grid=(b, t // tq),
        in_specs=in_specs,
        out_specs=[pl.BlockSpec((1, tq, cd), lambda bi, i: (bi, i, 0)),
                   pl.BlockSpec((1, keep, cd), lambda bi, i: (bi, 0, 0))],
        out_shape=[jax.ShapeDtypeStruct((b, t, cd), BF16), jax.ShapeDtypeStruct((b, keep, cd), F32)],
        scratch_shapes=[pltpu.VMEM((CONV_HALO + tq, cd), F32)],
        compiler_params=_cparams(("parallel", "arbitrary")),
        name="conv_module",
    )(*operands)


PV_MU_R, PV_MU_K, PV_MU_V, PV_W0, PV_A0, PV_KK, PV_KA, PV_V0, PV_LNW, PV_LNB, PV_RK = range(11)
PV_ROWS = 16
RWKV_CHUNKS_PER_STEP = 2
RWKV_UNITS_PER_STEP = 16


def _shift_rows(x, first_row):
    rolled = pltpu.roll(x, 1, axis=0)
    row = lax.broadcasted_iota(jnp.int32, x.shape, 0)
    return jnp.where(row == 0, first_row, rolled)


def _rwkv_kernel(*refs, n_heads, has_vres, bb, n_cc):
    (xr_ref, xk_ref, xv_ref, xl_ref, shr_ref, shk_ref, shv_ref, shl_ref, s0_ref, pv_ref, mul_ref,
     w2_ref, a2_ref, g2_ref) = refs[:14]
    pos = 14
    if has_vres:
        v1_ref, v2_ref, vf_ref = refs[pos:pos + 3]
        pos += 3
    ob_ref, sout_ref = refs[pos:pos + 2]
    pos += 2
    if not has_vres:
        vf_out_ref = refs[pos]
        pos += 1
    s_s, lr_s, lk_s, lv_s, ll_s = refs[pos:pos + 5]
    L = CHUNK
    c = pl.program_id(1)

    n_pairs = n_heads // 2
    PW = 2 * HEAD_DIM

    @pl.when(c == 0)
    def _():
        zero = jnp.zeros((HEAD_DIM, HEAD_DIM), F32)
        for bi in range(bb):
            for p in range(n_pairs):
                top = jnp.concatenate([s0_ref[bi, 2 * p], zero], axis=1)
                bottom = jnp.concatenate([zero, s0_ref[bi, 2 * p + 1]], axis=1)
                s_s[bi * n_pairs + p] = jnp.concatenate([top, bottom], axis=0)
            lr_s[bi, 0:1, :] = shr_ref[bi]
            lk_s[bi, 0:1, :] = shk_ref[bi]
            lv_s[bi, 0:1, :] = shv_ref[bi]
            ll_s[bi, 0:1, :] = shl_ref[bi]

    tl = n_cc * L

    def token_shift(x_ref, last_s, mu, bi):
        x = x_ref[bi]
        prev = _shift_rows(x, last_s[bi, 0:1, :])
        last_s[bi, 0:1, :] = x[tl - 1:tl, :]
        return x + (prev - x) * mu

    pv = lambda i: pv_ref[i:i + 1, :]
    row = lax.broadcasted_iota(jnp.int32, (L, L), 0)
    col = lax.broadcasted_iota(jnp.int32, (L, L), 1)
    tri_incl = (row >= col).astype(F32)

    full = {}
    for bi in range(bb):
        r = token_shift(xr_ref, lr_s, pv(PV_MU_R), bi)
        k = token_shift(xk_ref, lk_s, pv(PV_MU_K), bi)
        v = token_shift(xv_ref, lv_s, pv(PV_MU_V), bi)
        lo = token_shift(xl_ref, ll_s, mul_ref[...], bi)

        zw = pv(PV_W0) + _dot(jnp.tanh(lo).astype(BF16), w2_ref[...])
        w = -(jnp.maximum(-zw, 0.0) + jnp.log(1.0 + jnp.exp(-jnp.abs(zw)))) - 0.5
        d = -jnp.exp(w)
        iclr = jax.nn.sigmoid(pv(PV_A0) + _dot(lo.astype(BF16), a2_ref[...]))
        g = _dot(jax.nn.sigmoid(lo).astype(BF16), g2_ref[...])
        if has_vres:
            mix = jax.nn.sigmoid(pv(PV_V0) + _dot(_dot(v.astype(BF16), v1_ref[...]).astype(BF16), v2_ref[...]))
            v = v + (vf_ref[bi] - v) * mix
        else:
            vf_out_ref[bi] = v
        kks = k * pv(PV_KK)
        kh_all = k * (1.0 + (iclr - 1.0) * pv(PV_KA))
        for cc in range(n_cc):
            rows = slice(cc * L, (cc + 1) * L)
            dc = d[rows, :]
            cs = _dot_hp(tri_incl, dc)
            cprev = cs - dc
            c_last = cs[L - 1:L, :]
            c_mid = cs[L // 2:L // 2 + 1, :]
            full[bi, cc] = dict(
                r=r[rows, :], v=v[rows, :], g=g[rows, :], iclr=iclr[rows, :], kks=kks[rows, :], kh=kh_all[rows, :],
                e_mid=jnp.exp(c_mid),
                e_mid_prev=jnp.exp(cprev - c_mid), e_mid_incl=jnp.exp(cs - c_mid), e_from_mid=jnp.exp(c_mid - cs),
                e_to_end=jnp.exp(c_last - cs),
                e_chunk=jnp.exp(c_last))

    lane_lo = lax.broadcasted_iota(jnp.int32, (1, PW), 1) < HEAD_DIM
    r2 = lax.broadcasted_iota(jnp.int32, (2 * L, 2 * L), 0)
    c2 = lax.broadcasted_iota(jnp.int32, (2 * L, 2 * L), 1)
    same_head = (r2 >= L) == (c2 >= L)
    strict_bd = same_head & (jnp.bitwise_and(r2, L - 1) > jnp.bitwise_and(c2, L - 1))
    eye_bd = (r2 == c2).astype(F32)
    incl_c = (lax.broadcasted_iota(jnp.int32, (L, 2 * L), 0)
              >= jnp.bitwise_and(lax.broadcasted_iota(jnp.int32, (L, 2 * L), 1), L - 1))

    def bd(x):
        return jnp.concatenate([jnp.where(lane_lo, x, 0.0), jnp.where(lane_lo, 0.0, x)], axis=0)

    def tile2(x):
        return jnp.concatenate([x, x], axis=0)

    def head_sum(x):
        lo_sum = jnp.sum(jnp.where(lane_lo, x, 0.0), axis=-1, keepdims=True)
        hi_sum = jnp.sum(jnp.where(lane_lo, 0.0, x), axis=-1, keepdims=True)
        return jnp.where(lane_lo, lo_sum, hi_sum)

    bf = lambda z: z.astype(BF16)
    dot_nt = lambda p_, q_: lax.dot_general(p_, q_, (((1,), (1,)), ((), ())), preferred_element_type=F32)
    dot_tn = lambda p_, q_: lax.dot_general(p_, q_, (((0,), (0,)), ((), ())), preferred_element_type=F32)

    units = [(bi, cc, p) for cc in range(n_cc) for bi in range(bb) for p in range(n_pairs)]
    sls = [slice(p * PW, (p + 1) * PW) for _, _, p in units]
    get = lambda name: [full[bi, cc][name][:, sl] for (bi, cc, _), sl in zip(units, sls)]
    mul = lambda xs_, ys_: [x_ * y_ for x_, y_ in zip(xs_, ys_)]

    kk = get("kks")
    kk = [z / jnp.maximum(jnp.sqrt(head_sum(z * z)), 1e-12) for z in kk]
    a_p = [-z for z in kk]
    b_p = mul(kk, get("iclr"))
    k_p, r_p, v_p = get("kh"), get("r"), get("v")
    at = mul(a_p, get("e_mid_prev"))
    rt = mul(r_p, get("e_mid_incl"))
    e_from_mid, e_to_end = get("e_from_mid"), get("e_to_end")
    bt, kt = mul(b_p, e_from_mid), mul(k_p, e_from_mid)
    bh, kh = mul(b_p, e_to_end), mul(k_p, e_to_end)

    cat0 = lambda x_, y_: jnp.concatenate([x_, y_], axis=0)
    cat1 = lambda x_, y_: jnp.concatenate([x_, y_], axis=1)
    at_bd = [bf(bd(z)) for z in at]
    g_a = [dot_nt(x_, bf(cat0(tile2(b_), tile2(k_)))) for x_, b_, k_ in zip(at_bd, bt, kt)]
    n_ab = [jnp.where(strict_bd, z[:, 0:PW], 0.0) for z in g_a]
    n_ak = [jnp.where(strict_bd, z[:, PW:2 * PW], 0.0) for z in g_a]
    rt_b = [bf(z) for z in rt]
    n_r = [jnp.where(cat1(incl_c, incl_c), dot_nt(x_, bf(cat0(bd(b_), bd(k_)))), 0.0)
           for x_, b_, k_ in zip(rt_b, bt, kt)]
    y_inv = [eye_bd + z for z in n_ab]
    pw = [_dot(bf(z), bf(z)) for z in n_ab]
    for _ in range(int(math.log2(L)) - 2):
        lvl = [_dot(bf(p_), bf(cat1(y_, p_))) for p_, y_ in zip(pw, y_inv)]
        y_inv = [y_ + z[:, 0:PW] for y_, z in zip(y_inv, lvl)]
        pw = [z[:, PW:2 * PW] for z in lvl]
    y_inv = [y_ + _dot(bf(p_), bf(y_)) for p_, y_ in zip(pw, y_inv)]
    v_bd = [bf(bd(z)) for z in v_p]
    t2 = [_dot(bf(x_), y_) for x_, y_ in zip(n_ak, v_bd)]
    w12 = [_dot(bf(x_), cat1(a_, bf(t_))) for x_, a_, t_ in zip(y_inv, at_bd, t2)]
    e_mid, e_chunk, g_p = get("e_mid"), get("e_chunk"), get("g")
    n_seq = bb * n_pairs
    state = [s_s[i] for i in range(n_seq)]
    y = []
    for cc in range(n_cc):
        ids = range(cc * n_seq, (cc + 1) * n_seq)
        s_b = [bf(state[q] * e_mid[i]) for q, i in enumerate(ids)]
        u = [dot_nt(bf(w12[i][:, 0:PW]), s_) + w12[i][:, PW:2 * PW] for i, s_ in zip(ids, s_b)]
        y += [dot_nt(rt_b[i], s_) + _dot(bf(n_r[i]), cat0(bf(u_), v_bd[i])) for i, s_, u_ in zip(ids, s_b, u)]
        upd = [dot_tn(bf(cat0(u_[0:L] + u_[L:2 * L], v_p[i])), bf(cat0(bh[i], kh[i]))) for i, u_ in zip(ids, u)]
        state = [state[q] * e_chunk[i] + jnp.where(same_head, d_, 0.0) for q, (i, d_) in enumerate(zip(ids, upd))]
    for q in range(n_seq):
        s_s[q] = state[q]
    for i, ((bi, cc, _), sl) in enumerate(zip(units, sls)):
        mean = head_sum(y[i]) * (1.0 / HEAD_DIM)
        yc = y[i] - mean
        var = head_sum(yc * yc) * (1.0 / HEAD_DIM)
        yn = yc * lax.rsqrt(var + LNX_EPS) * pv(PV_LNW)[:, sl] + pv(PV_LNB)[:, sl]
        bonus = head_sum(r_p[i] * k_p[i] * pv(PV_RK)[:, sl])
        ob_ref[bi, cc * L:(cc + 1) * L, sl] = ((yn + bonus * v_p[i]) * g_p[i]).astype(ob_ref.dtype)

    @pl.when(c == pl.num_programs(1) - 1)
    def _():
        for q in range(n_seq):
            bi, p = q // n_pairs, q % n_pairs
            s_p = s_s[q]
            sout_ref[bi, 2 * p] = s_p[0:HEAD_DIM, 0:HEAD_DIM]
            sout_ref[bi, 2 * p + 1] = s_p[HEAD_DIM:PW, HEAD_DIM:PW]


def rwkv_mixer(p, shift0, s0, pvec, mu_l, w2p, a2p, g2p, vres, *, cols):
    b, t, _ = p.shape
    nc = t // CHUNK
    n_heads = s0.shape[1]
    cb = n_heads * HEAD_DIM
    rc, kc, vc, lc = cols
    has_vres = vres is not None
    n_cc = _batch_block(nc, RWKV_CHUNKS_PER_STEP)
    bb = _batch_block(b, RWKV_UNITS_PER_STEP // (n_cc * (n_heads // 2)))
    tok = lambda col, w: pl.BlockSpec((bb, n_cc * CHUNK, w), lambda bi, c, col=col: (bi, c, col))
    const2 = lambda shape: pl.BlockSpec(shape, lambda bi, c: (0, 0))
    perb = lambda w: pl.BlockSpec((bb, 1, w), lambda bi, c: (bi, 0, 0))
    state_spec = pl.BlockSpec((bb, n_heads, HEAD_DIM, HEAD_DIM), lambda bi, c: (bi, 0, 0, 0))
    operands = [p, p, p, p, *shift0, s0, pvec, mu_l, w2p, a2p, g2p]
    in_specs = [tok(rc, cb), tok(kc, cb), tok(vc, cb), tok(lc, LORA_W),
                perb(cb), perb(cb), perb(cb), perb(LORA_W), state_spec,
                const2((PV_ROWS, cb)), const2((1, LORA_W)),
                const2((LORA_W, cb)), const2((LORA_W, cb)), const2((LORA_W, cb))]
    out_specs = [tok(0, cb), state_spec]
    out_shape = [jax.ShapeDtypeStruct((b, t, cb), BF16), jax.ShapeDtypeStruct(s0.shape, F32)]
    if has_vres:
        v1p, v2p, v_first = vres
        operands += [v1p, v2p, v_first]
        in_specs += [const2(v1p.shape), const2(v2p.shape), tok(0, cb)]
    else:
        out_specs.append(tok(0, cb))
        out_shape.append(jax.ShapeDtypeStruct((b, t, cb), F32))
    outs = pl.pallas_call(
        functools.partial(_rwkv_kernel, n_heads=n_heads, has_vres=has_vres, bb=bb, n_cc=n_cc),
        grid=(b // bb, nc // n_cc),
        in_specs=in_specs,
        out_specs=out_specs,
        out_shape=out_shape,
        scratch_shapes=[pltpu.VMEM((bb * n_heads // 2, 2 * HEAD_DIM, 2 * HEAD_DIM), F32),
                        pltpu.VMEM((bb, 8, cb), F32), pltpu.VMEM((bb, 8, cb), F32), pltpu.VMEM((bb, 8, cb), F32),
                        pltpu.VMEM((bb, 8, LORA_W), F32)],
        compiler_params=_cparams(("parallel", "arbitrary")),
        name="rwkv_mixer",
    )(*operands)
    return (outs[0], outs[1], None) if has_vres else tuple(outs)


def _band_offsets(n_left):
    return np.arange(-(CHUNK - 1), (n_left + 1) * CHUNK) - n_left * CHUNK


def _lookup_static(table, idx):
    h = table.shape[0]
    pieces = []
    i, n = 0, len(idx)
    while i < n:
        j = i + 1
        step = int(idx[j] - idx[i]) if j < n else 0
        if step in (-1, 0, 1):
            while j < n and idx[j] - idx[j - 1] == step:
                j += 1
        first, last = int(idx[i]), int(idx[j - 1])
        if first == last:
            pieces.append(jnp.broadcast_to(table[:, first:first + 1], (h, j - i)))
        elif first < last:
            pieces.append(table[:, first:last + 1])
        else:
            pieces.append(jnp.flip(table[:, last:first + 1], axis=1))
        i = j
    return jnp.concatenate(pieces, axis=1)


def _toeplitz(e, n_rows, n_cols):
    h, n = e.shape
    z = jnp.concatenate([e, jnp.zeros((h, 1), e.dtype)], axis=1)
    shifted = jnp.tile(z, (1, n_rows))[:, :n_rows * n].reshape(h, n_rows, n)
    return shifted[:, :, n_rows - 1:n_rows - 1 + n_cols]


def _t5_bucket(rel):
    nb = T5_BUCKETS // 2
    exact = nb // 2
    n = np.abs(rel)
    nf = np.maximum(n, exact).astype(np.float32)
    large = exact + (np.log(nf / exact) / math.log(T5_MAX_DIST / exact) * (nb - exact)).astype(np.int32)
    return np.where(rel > 0, nb, 0) + np.where(n < exact, n, np.minimum(large, nb - 1))


def _bias_a(t5_table):
    e = _lookup_static(t5_table.T.astype(F32), _t5_bucket(_band_offsets(A_LEFT)))
    return _toeplitz(e, CHUNK, (A_LEFT + 1) * CHUNK)


def _bias_c(rel_table):
    idx = np.clip(-_band_offsets(C_LEFT), -REL_CLIP, REL_CLIP) + REL_CLIP
    return _toeplitz(_lookup_static(rel_table.astype(F32), idx), CHUNK, (C_LEFT + 1) * CHUNK)


def _pad_rows(w, start, total):
    return jnp.zeros((total, w.shape[1]), w.dtype).at[start:start + w.shape[0]].set(w)


def kernel(x_prompt, x_sample, cache_a_k, cache_a_v, state_b_wkv, state_b_shift, cache_c_k, cache_c_v, state_d_conv, norm_mix_g, norm_ffn_g, norm_final_g, t5_table, w_in_e, w_out_e, a_sink, b_mu, b_w0, b_w2, b_a0, b_a2, b_g2, b_kk, b_ka, b_rk, b_lnx_w, b_lnx_b, b_v0, b_v1, b_v2, w_in_o, w_out_o, c_rel_table, d_dw_w, d_dw_b, d_ln_g, d_ln_b, ffn_w_gate, ffn_w_up, ffn_w_down):
    depth, d_model = norm_mix_g.shape
    n_even = w_in_e.shape[0]
    h_a = a_sink.shape[1]
    g_a = h_a // KVH_A
    qa = h_a * HEAD_DIM
    kva = KVH_A * HEAD_DIM
    h_b = state_b_wkv.shape[2]
    cb = h_b * HEAD_DIM
    h_c = c_rel_table.shape[1]
    qc = h_c * HEAD_DIM
    cd = d_dw_w.shape[2]
    d_ff = ffn_w_gate.shape[2]
    tf = d_ff // 2 if (d_ff // 2) % 128 == 0 else d_ff

    groups = [x_prompt, x_sample]
    dims = [x.shape[:2] for x in groups]
    xs = [x.reshape(-1, d_model) for x in groups]
    tms = [min(1024, x.shape[0]) for x in xs]
    post_tms = [min(1024, x.shape[0]) for x in xs]
    wg, wu, wd = ffn_w_gate.astype(BF16), ffn_w_up.astype(BF16), ffn_w_down.astype(BF16)
    conv_tq = [min(512, t) for _, t in dims]

    o_q, o_k, o_v, o_pb = 0, qa, qa + kva, qa + 2 * kva
    o_r, o_wd, o_kb, o_vb = o_pb, o_pb + cb, o_pb + cb + DECAY_LORA, o_pb + 2 * cb + DECAY_LORA
    o_ad = o_vb + cb
    o_gd = o_ad + ICLR_LORA
    perm_ranges = [(o_q, qa), (o_r, cb), (o_kb, cb), (o_vb, cb), (o_k, kva), (o_v, kva),
                   (o_wd, DECAY_LORA), (o_ad, ICLR_LORA), (o_gd, GATE_LORA)]
    n_r, n_kb, n_vb = qa, qa + cb, qa + 2 * cb
    n_ka = qa + 3 * cb
    n_va = n_ka + kva
    n_lo = n_va + kva
    pb_pieces = [[(o_r, cb)], [(o_kb, cb)], [(o_vb, cb)], [(o_wd, DECAY_LORA), (o_ad, ICLR_LORA), (o_gd, GATE_LORA)]]

    def split_pb(z):
        return [jnp.concatenate([z[..., a - o_pb:a - o_pb + n] for a, n in piece], axis=-1) for piece in pb_pieces]

    def join_pb(r_, k_, v_, lo_):
        return jnp.concatenate([r_, lo_[..., :DECAY_LORA], k_, v_, lo_[..., DECAY_LORA:]], axis=-1)

    bias_a = _bias_a(t5_table) * LOG2E
    st = [[[] for _ in range(7)] for _ in range(2)]
    v_first = [None, None]
    for i in range(depth):
        j = i // 2
        last = i == depth - 1
        if i % 2 == 0:
            w_in = jnp.concatenate([w_in_e[j][:, a:a + n] for a, n in perm_ranges], axis=1).astype(BF16)
            wo = w_out_e[j].astype(BF16)
            mu_r, mu_k, mu_v, mu_lo = split_pb(b_mu[j])
            pvec = jnp.zeros((PV_ROWS, cb), F32)
            rows = {PV_MU_R: mu_r, PV_MU_K: mu_k, PV_MU_V: mu_v, PV_W0: b_w0[j],
                    PV_A0: b_a0[j], PV_KK: b_kk[j], PV_KA: b_ka[j], PV_LNW: b_lnx_w[j], PV_LNB: b_lnx_b[j],
                    PV_RK: b_rk[j].reshape(cb)}
            if j > 0:
                rows[PV_V0] = b_v0[j - 1]
            for ri, val in rows.items():
                pvec = pvec.at[ri].set(val)
            mu_l = mu_lo.reshape(1, LORA_W)
            w2p = _pad_rows(b_w2[j], 0, LORA_W).astype(BF16)
            a2p = _pad_rows(b_a2[j], DECAY_LORA, LORA_W).astype(BF16)
            g2p = _pad_rows(b_g2[j], DECAY_LORA + ICLR_LORA, LORA_W).astype(BF16)
            if j > 0:
                lora_v = b_v1.shape[2]
                v1p = jnp.zeros((cb, 128), F32).at[:, :lora_v].set(b_v1[j - 1]).astype(BF16)
                v2p = _pad_rows(b_v2[j - 1], 0, 128).astype(BF16)
            bias = bias_a
        else:
            w_in = w_in_o[j].astype(BF16)
            wo = w_out_o[j].astype(BF16)
            bias = _bias_c(c_rel_table[j]) * LOG2E
        for gi in range(2):
            b, t = dims[gi]
            p = proj_in(xs[gi], norm_mix_g[i], w_in, tms[gi]).reshape(b, t, -1)
            if i % 2 == 0:
                attn_args = dict(n_heads=h_a, group=g_a, kcol=n_ka // kva, vcol=n_va // kva)
                if gi == 0:
                    shift0 = [jnp.zeros((b, 1, w_), F32) for w_ in (cb, cb, cb, LORA_W)]
                    s0 = jnp.zeros((b, h_b, HEAD_DIM, HEAD_DIM), F32)
                    o1 = attention_prompt(p, bias, a_sink[j] * LOG2E, n_left=A_LEFT, **attn_args)
                else:
                    shift0 = [z[:, None, :] for z in split_pb(state_b_shift[j])]
                    s0 = state_b_wkv[j]
                    o1 = attention_sample(p, cache_a_k, cache_a_v, j, bias, a_sink[j] * LOG2E, **attn_args)
                vres = None if j == 0 else (v1p, v2p, v_first[gi])
                o2, wkv, vf = rwkv_mixer(p, shift0, s0, pvec, mu_l, w2p, a2p, g2p, vres,
                                         cols=(n_r // cb, n_kb // cb, n_vb // cb, n_lo // LORA_W))
                if j == 0:
                    v_first[gi] = vf
                keep = min(A_LEFT * CHUNK, t)
                nk = p[:, t - keep:, n_ka:n_ka + kva].reshape(b, keep, KVH_A, HEAD_DIM)
                nv = p[:, t - keep:, n_va:n_va + kva].reshape(b, keep, KVH_A, HEAD_DIM)
                nshift = join_pb(*[p[:, t - 1, c0:c0 + w_] for c0, w_ in ((n_r, cb), (n_kb, cb), (n_vb, cb), (n_lo, LORA_W))])
                for s_list, val in zip(st[gi][0:4], (nk, nv, wkv, nshift)):
                    s_list.append(val)
            else:
                attn_args = dict(n_heads=h_c, group=1, kcol=1, vcol=2)
                if gi == 0:
                    conv_prev = None
                    o1 = attention_prompt(p, bias, None, n_left=C_LEFT, **attn_args)
                else:
                    conv_prev = state_d_conv[j]
                    o1 = attention_sample(p, cache_c_k, cache_c_v, j, bias, None, **attn_args)
                o2, nconv = conv_module(p, conv_prev, d_dw_w[j], d_dw_b[j], d_ln_g[j], d_ln_b[j],
                                        acol=3 * qc // cd, gcol=3 * qc // cd + 1, tq=conv_tq[gi])
                keep = min(C_LEFT * CHUNK, t)
                nk = p[:, t - keep:, qc:2 * qc].reshape(b, keep, h_c, HEAD_DIM)
                nv = p[:, t - keep:, 2 * qc:3 * qc].reshape(b, keep, h_c, HEAD_DIM)
                for s_list, val in zip(st[gi][4:7], (nk, nv, nconv)):
                    s_list.append(val)
            half = o1.shape[-1]
            xs[gi] = post(xs[gi], o1.reshape(-1, half), o2.reshape(-1, half), wo, norm_ffn_g[i], wg, wu, wd,
                          norm_final_g, i, tm=post_tms[gi], tf=tf, final_norm=last)
    y_prompt = xs[0].reshape(x_prompt.shape)
    y_sample = xs[1].reshape(x_sample.shape)
    (pak, pav, pbw, pbs, pck, pcv, pdc), (sak, sav, sbw, sbs, sck, scv, sdc) = [[jnp.stack(s) for s in g] for g in st]
    return (y_prompt, y_sample, pak, pav, pbw, pbs, pck, pcv, pdc, sak, sav, sbw, sbs, sck, scv, sdc)
```

```python
import functools
import math

import jax
import jax.numpy as jnp
import numpy as np
from jax import lax
from jax.experimental import pallas as pl
from jax.experimental.pallas import tpu as pltpu

F32 = jnp.float32
BF16 = jnp.bfloat16
HIGHEST = lax.Precision.HIGHEST

CHUNK = 64
HEAD_DIM = 64
RMS_EPS = 1e-6
NEG_INF = -1e30
LNX_EPS = 64e-5
LN_EPS = 1e-5
CONV_W = 31
T5_BUCKETS = 32
T5_MAX_DIST = 128
REL_CLIP = 128
A_LEFT = 2
C_LEFT = 8
KVH_A = 2
DECAY_LORA = 64
ICLR_LORA = 64
GATE_LORA = 128
LORA_W = DECAY_LORA + ICLR_LORA + GATE_LORA
CONV_HALO = 32
CONV_ROWS = 64
LOG2E = math.log2(math.e)
QK_SCALE = HEAD_DIM ** -0.5 * LOG2E
TOKEN_ROWS = 1024
CONV_TQ = 1024
POST_ROWS = 256
PAIR_W = 2 * HEAD_DIM
ATTN_TQ = 256
ATTN_SAMPLE_BB = 4
ATTN_PROMPT_HEAD_GROUP = 4

VMEM_LIMIT = 56 * 1024 * 1024


def _cparams(sem):
    return pltpu.CompilerParams(dimension_semantics=sem, vmem_limit_bytes=VMEM_LIMIT)


def _batch_block(b, limit):
    return max(d for d in range(1, limit + 1) if b % d == 0)


def _rms(x, g):
    return x * lax.rsqrt(jnp.mean(x * x, axis=-1, keepdims=True) + RMS_EPS) * g


def _dot(a, b):
    return jnp.dot(a, b, preferred_element_type=F32)


def _dot_hp(a, b):
    return jnp.dot(a, b, precision=HIGHEST, preferred_element_type=F32)


def _proj_in_kernel(x_ref, g_ref, w_ref, o_ref):
    h = _rms(x_ref[...], g_ref[...]).astype(BF16)
    o_ref[...] = _dot(h, w_ref[...])


def proj_in(x2d, g, w_bf16, tm):
    m, d = x2d.shape
    n = w_bf16.shape[1]
    return pl.pallas_call(
        _proj_in_kernel,
        grid=(m // tm,),
        in_specs=[pl.BlockSpec((tm, d), lambda i: (i, 0)),
                  pl.BlockSpec((1, d), lambda i: (0, 0)),
                  pl.BlockSpec((d, n), lambda i: (0, 0))],
        out_specs=pl.BlockSpec((tm, n), lambda i: (i, 0)),
        out_shape=jax.ShapeDtypeStruct((m, n), F32),
        compiler_params=_cparams(("parallel",)),
        name="proj_in",
    )(x2d, g.reshape(1, d), w_bf16)


def _post_kernel(x_ref, o1_ref, o2_ref, wo_ref, g_ref, wg_ref, wu_ref, wd_ref, gf_ref, out_ref, h_s, *, final_norm):
    f = pl.program_id(1)
    half = o1_ref.shape[-1]
    tm = h_s.shape[0]
    rs = min(tm, POST_ROWS)

    @pl.when(f == 0)
    def _():
        for r0 in range(0, tm, rs):
            rows = slice(r0, r0 + rs)
            xn = (x_ref[rows, :] + _dot(o1_ref[rows, :], wo_ref[0:half, :])
                  + _dot(o2_ref[rows, :], wo_ref[half:2 * half, :]))
            out_ref[rows, :] = xn
            h_s[rows, :] = _rms(xn, g_ref[...]).astype(BF16)

    for r0 in range(0, tm, rs):
        rows = slice(r0, r0 + rs)
        h = h_s[rows, :]
        gate = _dot(h, wg_ref[...])
        up = _dot(h, wu_ref[...])
        act = gate * jax.nn.sigmoid(gate) * up
        out_ref[rows, :] += _dot(act.astype(BF16), wd_ref[...])

    if final_norm:
        @pl.when(f == pl.num_programs(1) - 1)
        def _():
            for r0 in range(0, tm, rs):
                rows = slice(r0, r0 + rs)
                out_ref[rows, :] = _rms(out_ref[rows, :], gf_ref[...])


def post(x2d, o1, o2, wo, g, wg, wu, wd, gf, layer, *, tm, tf, final_norm):
    m, d = x2d.shape
    half = o1.shape[-1]
    dff = wg.shape[2]
    return pl.pallas_call(
        functools.partial(_post_kernel, final_norm=final_norm),
        grid=(m // tm, dff // tf),
        in_specs=[pl.BlockSpec((tm, d), lambda i, f: (i, 0)),
                  pl.BlockSpec((tm, half), lambda i, f: (i, 0)),
                  pl.BlockSpec((tm, half), lambda i, f: (i, 0)),
                  pl.BlockSpec((2 * half, d), lambda i, f: (0, 0)),
                  pl.BlockSpec((1, d), lambda i, f: (0, 0)),
                  pl.BlockSpec((None, d, tf), lambda i, f: (layer, 0, f)),
                  pl.BlockSpec((None, d, tf), lambda i, f: (layer, 0, f)),
                  pl.BlockSpec((None, tf, d), lambda i, f: (layer, f, 0)),
                  pl.BlockSpec((1, d), lambda i, f: (0, 0))],
        out_specs=pl.BlockSpec((tm, d), lambda i, f: (i, 0)),
        out_shape=jax.ShapeDtypeStruct((m, d), F32),
        scratch_shapes=[pltpu.VMEM((tm, d), BF16)],
        compiler_params=_cparams(("parallel", "arbitrary")),
        name="post",
    )(x2d, o1, o2, wo, g.reshape(1, d), wg, wu, wd, gf.reshape(1, d))


def _attn_prompt_kernel(*refs, n_left, tq, n_heads, group, has_sink):
    q_ref, k_ref, v_ref, bias_ref = refs[:4]
    sink_ref = refs[4] if has_sink else None
    o_ref = refs[5] if has_sink else refs[4]
    scratch = refs[6:] if has_sink else refs[5:]
    hist = n_left * CHUNK
    nk = hist + tq
    i = pl.program_id(1)
    nt = (((1,), (1,)), ((), ()))
    slot = lax.rem(i, 2)
    k_win, vt_win = scratch[0].at[slot], scratch[1].at[slot]
    k_next, vt_next = scratch[0].at[1 - slot], scratch[1].at[1 - slot]
    k_swap = scratch[2].at[slot] if group > 1 else None
    k_swap_next = scratch[2].at[1 - slot] if group > 1 else None

    @pl.when(i == 0)
    def _():
        k_win[0:hist, :] = jnp.zeros((hist, k_win.shape[1]), BF16)
        vt_win[:, 0:hist] = jnp.zeros((vt_win.shape[0], hist), BF16)
        if group > 1:
            k_swap[0:hist, :] = jnp.zeros((hist, k_swap.shape[1]), BF16)

    k_new = k_ref[0]
    k_win[hist:nk, :] = k_new.astype(BF16)
    if group > 1:
        k_swap[hist:nk, :] = pltpu.roll(k_new, HEAD_DIM, axis=1).astype(BF16)
    vt_win[:, hist:nk] = v_ref[0].T.astype(BF16)
    q = q_ref[0] * QK_SCALE
    lane_lo = lax.broadcasted_iota(jnp.int32, (1, PAIR_W), 1) < HEAD_DIM
    lane_hi = jnp.logical_not(lane_lo)

    for h0 in range(0, n_heads, ATTN_PROMPT_HEAD_GROUP):
        hs = list(range(h0, h0 + ATTN_PROMPT_HEAD_GROUP))
        qh = [jnp.where(lane_lo if h % 2 == 0 else lane_hi, q[:, (h // 2) * PAIR_W:(h // 2 + 1) * PAIR_W], 0.0)
              .astype(BF16) for h in hs]
        if group > 1:
            own = [h // group for h in hs]
            kt = [(k_win if kv == h % 2 else k_swap)[...] for kv, h in zip(own, hs)]
            vt = [vt_win[...] for _ in hs]
        else:
            own = [h % 2 for h in hs]
            kt = [k_win[:, (h // 2) * PAIR_W:(h // 2 + 1) * PAIR_W] for h in hs]
            vt = [vt_win[(h // 2) * PAIR_W:(h // 2 + 1) * PAIR_W, :] for h in hs]
        s = [lax.dot_general(k_, q_, nt, preferred_element_type=F32) + bias_ref[h] for k_, q_, h in zip(kt, qh, hs)]
        m = [jnp.max(z, axis=0, keepdims=True) for z in s]
        if has_sink:
            sk = [sink_ref[h:h + 1, :] for h in hs]
            m = [jnp.maximum(z, k_) for z, k_ in zip(m, sk)]
        e = [jnp.exp2(z - m_) for z, m_ in zip(s, m)]
        den = [jnp.sum(z, axis=0, keepdims=True) for z in e]
        if has_sink:
            den = [z + jnp.exp2(k_ - m_) for z, k_, m_ in zip(den, sk, m)]
        ot = [_dot(v_, z.astype(BF16))[o_ * HEAD_DIM:(o_ + 1) * HEAD_DIM, :] / d_
              for v_, z, o_, d_ in zip(vt, e, own, den)]
        for j in range(0, len(hs), 2):
            pair = hs[j] // 2
            o_ref[0, :, pair * PAIR_W:(pair + 1) * PAIR_W] = (
                jnp.concatenate([ot[j], ot[j + 1]], axis=0).T.astype(o_ref.dtype))

    k_next[0:hist, :] = k_win[tq:nk, :]
    vt_next[:, 0:hist] = vt_win[:, tq:nk]
    if group > 1:
        k_swap_next[0:hist, :] = k_swap[tq:nk, :]


def attention_prompt(p, bias, sink, *, n_left, n_heads, group, kcol, vcol):
    b, t, _ = p.shape
    kvw = (n_heads // group) * HEAD_DIM
    qw = n_heads * HEAD_DIM
    has_sink = sink is not None
    assert group == 1 or kvw == PAIR_W, "grouped-query path assumes two kv heads sharing one lane tile"
    tq = min(ATTN_TQ, t)
    nk = n_left * CHUNK + tq
    band_t = jnp.swapaxes(bias, 1, 2)
    bias_full = jnp.concatenate(
        [jnp.pad(band_t, ((0, 0), (qc * CHUNK, nk - qc * CHUNK - band_t.shape[1]), (0, 0)), constant_values=NEG_INF)
         for qc in range(tq // CHUNK)], axis=2)
    hist = n_left * CHUNK
    n_var = -(-hist // tq) + 1
    row = jnp.arange(nk)[None, :, None]
    bias_var = jnp.stack([jnp.where(row >= hist - v * tq, bias_full, NEG_INF) for v in range(n_var - 1)] + [bias_full])
    sink_ops = [sink.reshape(n_heads, 1)] if has_sink else []
    sink_specs = [pl.BlockSpec((n_heads, 1), lambda bi, i: (0, 0))] if has_sink else []
    return pl.pallas_call(
        functools.partial(_attn_prompt_kernel, n_left=n_left, tq=tq, n_heads=n_heads, group=group, has_sink=has_sink),
        grid=(b, t // tq),
        in_specs=[pl.BlockSpec((1, tq, qw), lambda bi, i: (bi, i, 0)),
                  pl.BlockSpec((1, tq, kvw), lambda bi, i: (bi, i, kcol)),
                  pl.BlockSpec((1, tq, kvw), lambda bi, i: (bi, i, vcol)),
                  pl.BlockSpec((None,) + bias_full.shape, lambda bi, i: (jnp.minimum(i, n_var - 1), 0, 0, 0))]
        + sink_specs,
        out_specs=pl.BlockSpec((1, tq, qw), lambda bi, i: (bi, i, 0)),
        out_shape=jax.ShapeDtypeStruct((b, t, qw), BF16),
        scratch_shapes=([pltpu.VMEM((2, nk, kvw), BF16), pltpu.VMEM((2, kvw, nk), BF16)]
                        + ([pltpu.VMEM((2, nk, kvw), BF16)] if group > 1 else [])),
        compiler_params=_cparams(("parallel", "arbitrary")),
        name="attention_prompt",
    )(p, p, p, bias_var, *sink_ops)


def _attn_sample_kernel(*refs, n_heads, group, has_sink, bb):
    q_ref, kc_ref, vc_ref, kn_ref, vn_ref, bias_ref = refs[:6]
    sink_ref = refs[6] if has_sink else None
    o_ref = refs[-1]
    hist = kc_ref.shape[-1]
    nt = (((1,), (1,)), ((), ()))
    heads = range(n_heads)
    q_sl = [slice(h * HEAD_DIM, (h + 1) * HEAD_DIM) for h in heads]
    kv_sl = [slice((h // group) * HEAD_DIM, (h // group + 1) * HEAD_DIM) for h in heads]
    for bi in range(bb):
        q = (q_ref[bi] * QK_SCALE).astype(BF16)
        kn = kn_ref[bi].astype(BF16)
        vn = vn_ref[bi].astype(BF16)
        qh = [q[:, sl] for sl in q_sl]
        s = [jnp.concatenate([_dot(qh[h], kc_ref[bi, h // group].astype(BF16)),
                              lax.dot_general(qh[h], kn[:, kv_sl[h]], nt, preferred_element_type=F32)], axis=1)
             + bias_ref[h] for h in heads]
        m = [jnp.max(z, axis=-1, keepdims=True) for z in s]
        if has_sink:
            sk = [sink_ref[h:h + 1, :] for h in heads]
            m = [jnp.maximum(z, k_) for z, k_ in zip(m, sk)]
        e = [jnp.exp2(z - m_) for z, m_ in zip(s, m)]
        den = [jnp.sum(z, axis=-1, keepdims=True) for z in e]
        if has_sink:
            den = [z + jnp.exp2(k_ - m_) for z, k_, m_ in zip(den, sk, m)]
        for h in heads:
            eb = e[h].astype(BF16)
            o = (lax.dot_general(eb[:, 0:hist], vc_ref[bi, h // group].astype(BF16), nt, preferred_element_type=F32)
                 + _dot(eb[:, hist:], vn[:, kv_sl[h]]))
            o_ref[bi, :, q_sl[h]] = (o / den[h]).astype(o_ref.dtype)


def attention_sample(p, k_cache, v_cache, layer, bias, sink, *, n_heads, group, kcol, vcol):
    b, t, _ = p.shape
    qw = n_heads * HEAD_DIM
    kv_heads = n_heads // group
    kvw = kv_heads * HEAD_DIM
    hist = k_cache.shape[2]
    has_sink = sink is not None
    bb = _batch_block(b, ATTN_SAMPLE_BB)
    transposed = lambda z: jnp.transpose(z, (0, 1, 3, 4, 2))
    cache_spec = pl.BlockSpec((None, bb, kv_heads, HEAD_DIM, hist), lambda i: (layer, i, 0, 0, 0))
    sink_ops = [sink.reshape(n_heads, 1)] if has_sink else []
    sink_specs = [pl.BlockSpec((n_heads, 1), lambda i: (0, 0))] if has_sink else []
    return pl.pallas_call(
        functools.partial(_attn_sample_kernel, n_heads=n_heads, group=group, has_sink=has_sink, bb=bb),
        grid=(b // bb,),
        in_specs=[pl.BlockSpec((bb, t, qw), lambda i: (i, 0, 0)), cache_spec, cache_spec,
                  pl.BlockSpec((bb, t, kvw), lambda i: (i, 0, kcol)),
                  pl.BlockSpec((bb, t, kvw), lambda i: (i, 0, vcol)),
                  pl.BlockSpec(bias.shape, lambda i: (0, 0, 0))] + sink_specs,
        out_specs=pl.BlockSpec((bb, t, qw), lambda i: (i, 0, 0)),
        out_shape=jax.ShapeDtypeStruct((b, t, qw), BF16),
        compiler_params=_cparams(("parallel",)),
        name="attention_sample",
    )(p, transposed(k_cache), transposed(v_cache), p, p, bias, *sink_ops)


def _conv_kernel(*refs, tq, from_state):
    if from_state:
        a_ref, gt_ref, st_in_ref, dw_ref, db_ref, lg_ref, lb_ref, o_ref, st_ref, buf = refs
    else:
        a_ref, gt_ref, pa_ref, pg_ref, dw_ref, db_ref, lg_ref, lb_ref, o_ref, st_ref, buf = refs
    keep = CONV_W - 1
    pad = CONV_HALO - keep
    if from_state:
        buf[0:pad, :] = jnp.zeros((pad, buf.shape[1]), F32)
        buf[pad:CONV_HALO, :] = st_in_ref[0]
    else:
        prev = pa_ref[0] * jax.nn.sigmoid(pg_ref[0])
        buf[0:CONV_HALO, :] = jnp.where(pl.program_id(1) > 0, prev, 0.0)
    buf[CONV_HALO:CONV_HALO + tq, :] = a_ref[0] * jax.nn.sigmoid(gt_ref[0])
    rs = min(tq, CONV_ROWS)
    for r0 in range(0, tq, rs):
        z = jnp.zeros((rs, buf.shape[1]), F32) + db_ref[...]
        base = buf[r0:r0 + rs + CONV_HALO, :]
        for s in range(8):
            rolled = pltpu.roll(base, rs + CONV_HALO - (pad + s), axis=0)
            for j, w in enumerate(range(s, CONV_W, 8)):
                z = z + rolled[8 * j:8 * j + rs, :] * dw_ref[w:w + 1, :]
        mean = jnp.mean(z, axis=-1, keepdims=True)
        zc = z - mean
        var = jnp.mean(zc * zc, axis=-1, keepdims=True)
        zn = zc * lax.rsqrt(var + LN_EPS) * lg_ref[...] + lb_ref[...]
        o_ref[0, r0:r0 + rs, :] = (zn * jax.nn.sigmoid(zn)).astype(o_ref.dtype)
    st_ref[0] = buf[tq + pad:tq + CONV_HALO, :]


def conv_module(p, state, dw_w, dw_b, ln_g, ln_b, *, acol, gcol, tq):
    b, t, _ = p.shape
    cd = dw_w.shape[1]
    keep = CONV_W - 1
    from_state = state is not None
    operands = [p, p]
    in_specs = [pl.BlockSpec((1, tq, cd), lambda bi, i: (bi, i, acol)),
                pl.BlockSpec((1, tq, cd), lambda bi, i: (bi, i, gcol))]
    if from_state:
        operands.append(state)
        in_specs.append(pl.BlockSpec((1, keep, cd), lambda bi, i: (bi, 0, 0)))
    else:
        r = tq // CONV_HALO
        for col in (acol, gcol):
            operands.append(p)
            in_specs.append(pl.BlockSpec((1, CONV_HALO, cd),
                                         lambda bi, i, col=col: (bi, jnp.maximum(i * r - 1, 0), col)))
    operands += [dw_w, dw_b.reshape(1, cd), ln_g.reshape(1, cd), ln_b.reshape(1, cd)]
    in_specs += [pl.BlockSpec((CONV_W, cd), lambda bi, i: (0, 0))] + [pl.BlockSpec((1, cd), lambda bi, i: (0, 0))] * 3
    return pl.pallas_call(
        functools.partial(_conv_kernel, tq=tq, from_state=from_state),
        grid=(b, t // tq),
        in_specs=in_specs,
        out_specs=[pl.BlockSpec((1, tq, cd), lambda bi, i: (bi, i, 0)),
                   pl.BlockSpec((1, keep, cd), lambda bi, i: (bi, 0, 0))],
        out_shape=[jax.ShapeDtypeStruct((b, t, cd), BF16), jax.ShapeDtypeStruct((b, keep, cd), F32)],
        scratch_shapes=[pltpu.VMEM((CONV_HALO + tq, cd), F32)],
        compiler_params=_cparams(("parallel", "arbitrary")),
        name="conv_module",
    )(*operands)


PV_MU_R, PV_MU_K, PV_MU_V, PV_W0, PV_A0, PV_KK, PV_KA, PV_V0, PV_LNW, PV_LNB, PV_RK = range(11)
PV_ROWS = 16
RWKV_CHUNKS_PER_STEP = 2
RWKV_UNITS_PER_STEP = 16


def _shift_rows(x, first_row):
    rolled = pltpu.roll(x, 1, axis=0)
    row = lax.broadcasted_iota(jnp.int32, x.shape, 0)
    return jnp.where(row == 0, first_row, rolled)


def _rwkv_kernel(*refs, n_heads, has_vres, bb, n_cc):
    (xr_ref, xk_ref, xv_ref, xl_ref, shr_ref, shk_ref, shv_ref, shl_ref, s0_ref, pv_ref, mul_ref,
     w2_ref, a2_ref, g2_ref) = refs[:14]
    pos = 14
    if has_vres:
        v1_ref, v2_ref, vf_ref = refs[pos:pos + 3]
        pos += 3
    ob_ref, sout_ref = refs[pos:pos + 2]
    pos += 2
    if not has_vres:
        vf_out_ref = refs[pos]
        pos += 1
    s_s, lr_s, lk_s, lv_s, ll_s = refs[pos:pos + 5]
    L = CHUNK
    c = pl.program_id(1)

    n_pairs = n_heads // 2
    PW = 2 * HEAD_DIM

    @pl.when(c == 0)
    def _():
        zero = jnp.zeros((HEAD_DIM, HEAD_DIM), F32)
        for bi in range(bb):
            for p in range(n_pairs):
                top = jnp.concatenate([s0_ref[bi, 2 * p], zero], axis=1)
                bottom = jnp.concatenate([zero, s0_ref[bi, 2 * p + 1]], axis=1)
                s_s[bi * n_pairs + p] = jnp.concatenate([top, bottom], axis=0)
            lr_s[bi, 0:1, :] = shr_ref[bi]
            lk_s[bi, 0:1, :] = shk_ref[bi]
            lv_s[bi, 0:1, :] = shv_ref[bi]
            ll_s[bi, 0:1, :] = shl_ref[bi]

    tl = n_cc * L

    def token_shift(x_ref, last_s, mu, bi):
        x = x_ref[bi]
        prev = _shift_rows(x, last_s[bi, 0:1, :])
        last_s[bi, 0:1, :] = x[tl - 1:tl, :]
        return x + (prev - x) * mu

    pv = lambda i: pv_ref[i:i + 1, :]
    row = lax.broadcasted_iota(jnp.int32, (L, L), 0)
    col = lax.broadcasted_iota(jnp.int32, (L, L), 1)
    tri_incl = (row >= col).astype(F32)

    full = {}
    for bi in range(bb):
        r = token_shift(xr_ref, lr_s, pv(PV_MU_R), bi)
        k = token_shift(xk_ref, lk_s, pv(PV_MU_K), bi)
        v = token_shift(xv_ref, lv_s, pv(PV_MU_V), bi)
        lo = token_shift(xl_ref, ll_s, mul_ref[...], bi)

        zw = pv(PV_W0) + _dot(jnp.tanh(lo).astype(BF16), w2_ref[...])
        w = -(jnp.maximum(-zw, 0.0) + jnp.log(1.0 + jnp.exp(-jnp.abs(zw)))) - 0.5
        d = -jnp.exp(w)
        iclr = jax.nn.sigmoid(pv(PV_A0) + _dot(lo.astype(BF16), a2_ref[...]))
        g = _dot(jax.nn.sigmoid(lo).astype(BF16), g2_ref[...])
        if has_vres:
            mix = jax.nn.sigmoid(pv(PV_V0) + _dot(_dot(v.astype(BF16), v1_ref[...]).astype(BF16), v2_ref[...]))
            v = v + (vf_ref[bi] - v) * mix
        else:
            vf_out_ref[bi] = v
        kks = k * pv(PV_KK)
        kh_all = k * (1.0 + (iclr - 1.0) * pv(PV_KA))
        for cc in range(n_cc):
            rows = slice(cc * L, (cc + 1) * L)
            dc = d[rows, :]
            cs = _dot_hp(tri_incl, dc)
            cprev = cs - dc
            c_last = cs[L - 1:L, :]
            c_mid = cs[L // 2:L // 2 + 1, :]
            full[bi, cc] = dict(
                r=r[rows, :], v=v[rows, :], g=g[rows, :], iclr=iclr[rows, :], kks=kks[rows, :], kh=kh_all[rows, :],
                e_mid=jnp.exp(c_mid),
                e_mid_prev=jnp.exp(cprev - c_mid), e_mid_incl=jnp.exp(cs - c_mid), e_from_mid=jnp.exp(c_mid - cs),
                e_to_end=jnp.exp(c_last - cs),
                e_chunk=jnp.exp(c_last))

    lane_lo = lax.broadcasted_iota(jnp.int32, (1, PW), 1) < HEAD_DIM
    r2 = lax.broadcasted_iota(jnp.int32, (2 * L, 2 * L), 0)
    c2 = lax.broadcasted_iota(jnp.int32, (2 * L, 2 * L), 1)
    same_head = (r2 >= L) == (c2 >= L)
    strict_bd = same_head & (jnp.bitwise_and(r2, L - 1) > jnp.bitwise_and(c2, L - 1))
    eye_bd = (r2 == c2).astype(F32)
    incl_c = (lax.broadcasted_iota(jnp.int32, (L, 2 * L), 0)
              >= jnp.bitwise_and(lax.broadcasted_iota(jnp.int32, (L, 2 * L), 1), L - 1))

    def bd(x):
        return jnp.concatenate([jnp.where(lane_lo, x, 0.0), jnp.where(lane_lo, 0.0, x)], axis=0)

    def tile2(x):
        return jnp.concatenate([x, x], axis=0)

    def head_sum(x):
        lo_sum = jnp.sum(jnp.where(lane_lo, x, 0.0), axis=-1, keepdims=True)
        hi_sum = jnp.sum(jnp.where(lane_lo, 0.0, x), axis=-1, keepdims=True)
        return jnp.where(lane_lo, lo_sum, hi_sum)

    bf = lambda z: z.astype(BF16)
    dot_nt = lambda p_, q_: lax.dot_general(p_, q_, (((1,), (1,)), ((), ())), preferred_element_type=F32)
    dot_tn = lambda p_, q_: lax.dot_general(p_, q_, (((0,), (0,)), ((), ())), preferred_element_type=F32)

    units = [(bi, cc, p) for cc in range(n_cc) for bi in range(bb) for p in range(n_pairs)]
    sls = [slice(p * PW, (p + 1) * PW) for _, _, p in units]
    get = lambda name: [full[bi, cc][name][:, sl] for (bi, cc, _), sl in zip(units, sls)]
    mul = lambda xs_, ys_: [x_ * y_ for x_, y_ in zip(xs_, ys_)]

    kk = get("kks")
    kk = [z / jnp.maximum(jnp.sqrt(head_sum(z * z)), 1e-12) for z in kk]
    a_p = [-z for z in kk]
    b_p = mul(kk, get("iclr"))
    k_p, r_p, v_p = get("kh"), get("r"), get("v")
    at = mul(a_p, get("e_mid_prev"))
    rt = mul(r_p, get("e_mid_incl"))
    e_from_mid, e_to_end = get("e_from_mid"), get("e_to_end")
    bt, kt = mul(b_p, e_from_mid), mul(k_p, e_from_mid)
    bh, kh = mul(b_p, e_to_end), mul(k_p, e_to_end)

    cat0 = lambda x_, y_: jnp.concatenate([x_, y_], axis=0)
    cat1 = lambda x_, y_: jnp.concatenate([x_, y_], axis=1)
    at_bd = [bf(bd(z)) for z in at]
    g_a = [dot_nt(x_, bf(cat0(tile2(b_), tile2(k_)))) for x_, b_, k_ in zip(at_bd, bt, kt)]
    n_ab = [jnp.where(strict_bd, z[:, 0:PW], 0.0) for z in g_a]
    n_ak = [jnp.where(strict_bd, z[:, PW:2 * PW], 0.0) for z in g_a]
    rt_b = [bf(z) for z in rt]
    n_r = [jnp.where(cat1(incl_c, incl_c), dot_nt(x_, bf(cat0(bd(b_), bd(k_)))), 0.0)
           for x_, b_, k_ in zip(rt_b, bt, kt)]
    y_inv = [eye_bd + z for z in n_ab]
    pw = [_dot(bf(z), bf(z)) for z in n_ab]
    for _ in range(int(math.log2(L)) - 2):
        lvl = [_dot(bf(p_), bf(cat1(y_, p_))) for p_, y_ in zip(pw, y_inv)]
        y_inv = [y_ + z[:, 0:PW] for y_, z in zip(y_inv, lvl)]
        pw = [z[:, PW:2 * PW] for z in lvl]
    y_inv = [y_ + _dot(bf(p_), bf(y_)) for p_, y_ in zip(pw, y_inv)]
    v_bd = [bf(bd(z)) for z in v_p]
    t2 = [_dot(bf(x_), y_) for x_, y_ in zip(n_ak, v_bd)]
    w12 = [_dot(bf(x_), cat1(a_, bf(t_))) for x_, a_, t_ in zip(y_inv, at_bd, t2)]
    e_mid, e_chunk, g_p = get("e_mid"), get("e_chunk"), get("g")
    n_seq = bb * n_pairs
    state = [s_s[i] for i in range(n_seq)]
    y = []
    for cc in range(n_cc):
        ids = range(cc * n_seq, (cc + 1) * n_seq)
        s_b = [bf(state[q] * e_mid[i]) for q, i in enumerate(ids)]
        u = [dot_nt(bf(w12[i][:, 0:PW]), s_) + w12[i][:, PW:2 * PW] for i, s_ in zip(ids, s_b)]
        y += [dot_nt(rt_b[i], s_) + _dot(bf(n_r[i]), cat0(bf(u_), v_bd[i])) for i, s_, u_ in zip(ids, s_b, u)]
        upd = [dot_tn(bf(cat0(u_[0:L] + u_[L:2 * L], v_p[i])), bf(cat0(bh[i], kh[i]))) for i, u_ in zip(ids, u)]
        state = [state[q] * e_chunk[i] + jnp.where(same_head, d_, 0.0) for q, (i, d_) in enumerate(zip(ids, upd))]
    for q in range(n_seq):
        s_s[q] = state[q]
    for i, ((bi, cc, _), sl) in enumerate(zip(units, sls)):
        mean = head_sum(y[i]) * (1.0 / HEAD_DIM)
        yc = y[i] - mean
        var = head_sum(yc * yc) * (1.0 / HEAD_DIM)
        yn = yc * lax.rsqrt(var + LNX_EPS) * pv(PV_LNW)[:, sl] + pv(PV_LNB)[:, sl]
        bonus = head_sum(r_p[i] * k_p[i] * pv(PV_RK)[:, sl])
        ob_ref[bi, cc * L:(cc + 1) * L, sl] = ((yn + bonus * v_p[i]) * g_p[i]).astype(ob_ref.dtype)

    @pl.when(c == pl.num_programs(1) - 1)
    def _():
        for q in range(n_seq):
            bi, p = q // n_pairs, q % n_pairs
            s_p = s_s[q]
            sout_ref[bi, 2 * p] = s_p[0:HEAD_DIM, 0:HEAD_DIM]
            sout_ref[bi, 2 * p + 1] = s_p[HEAD_DIM:PW, HEAD_DIM:PW]


def rwkv_mixer(p, shift0, s0, layer, pvec, mu_l, w2p, a2p, g2p, vres, *, cols):
    b, t, _ = p.shape
    nc = t // CHUNK
    n_heads = s0.shape[2]
    cb = n_heads * HEAD_DIM
    rc, kc, vc, lc = cols
    has_vres = vres is not None
    n_cc = _batch_block(nc, RWKV_CHUNKS_PER_STEP)
    bb = _batch_block(b, RWKV_UNITS_PER_STEP // (n_cc * (n_heads // 2)))
    tok = lambda col, w: pl.BlockSpec((bb, n_cc * CHUNK, w), lambda bi, c, col=col: (bi, c, col))
    const2 = lambda shape: pl.BlockSpec(shape, lambda bi, c: (0, 0))
    perb = lambda w: pl.BlockSpec((bb, 1, w), lambda bi, c: (bi, 0, 0))
    state_spec = pl.BlockSpec((bb, n_heads, HEAD_DIM, HEAD_DIM), lambda bi, c: (bi, 0, 0, 0))
    state_in_spec = pl.BlockSpec((None, bb, n_heads, HEAD_DIM, HEAD_DIM), lambda bi, c: (layer, bi, 0, 0, 0))
    operands = [p, p, p, p, *shift0, s0, pvec, mu_l, w2p, a2p, g2p]
    in_specs = [tok(rc, cb), tok(kc, cb), tok(vc, cb), tok(lc, LORA_W),
                perb(cb), perb(cb), perb(cb), perb(LORA_W), state_in_spec,
                const2((PV_ROWS, cb)), const2((1, LORA_W)),
                const2((LORA_W, cb)), const2((LORA_W, cb)), const2((LORA_W, cb))]
    out_specs = [tok(0, cb), state_spec]
    out_shape = [jax.ShapeDtypeStruct((b, t, cb), BF16), jax.ShapeDtypeStruct(s0.shape[1:], F32)]
    if has_vres:
        v1p, v2p, v_first = vres
        operands += [v1p, v2p, v_first]
        in_specs += [const2(v1p.shape), const2(v2p.shape), tok(0, cb)]
    else:
        out_specs.append(tok(0, cb))
        out_shape.append(jax.ShapeDtypeStruct((b, t, cb), F32))
    outs = pl.pallas_call(
        functools.partial(_rwkv_kernel, n_heads=n_heads, has_vres=has_vres, bb=bb, n_cc=n_cc),
        grid=(b // bb, nc // n_cc),
        in_specs=in_specs,
        out_specs=out_specs,
        out_shape=out_shape,
        scratch_shapes=[pltpu.VMEM((bb * n_heads // 2, 2 * HEAD_DIM, 2 * HEAD_DIM), F32),
                        pltpu.VMEM((bb, 8, cb), F32), pltpu.VMEM((bb, 8, cb), F32), pltpu.VMEM((bb, 8, cb), F32),
                        pltpu.VMEM((bb, 8, LORA_W), F32)],
        compiler_params=_cparams(("parallel", "arbitrary")),
        name="rwkv_mixer",
    )(*operands)
    return (outs[0], outs[1], None) if has_vres else tuple(outs)


def _band_offsets(n_left):
    return np.arange(-(CHUNK - 1), (n_left + 1) * CHUNK) - n_left * CHUNK


def _lookup_static(table, idx):
    h = table.shape[0]
    pieces = []
    i, n = 0, len(idx)
    while i < n:
        j = i + 1
        step = int(idx[j] - idx[i]) if j < n else 0
        if step in (-1, 0, 1):
            while j < n and idx[j] - idx[j - 1] == step:
                j += 1
        first, last = int(idx[i]), int(idx[j - 1])
        if first == last:
            pieces.append(jnp.broadcast_to(table[:, first:first + 1], (h, j - i)))
        elif first < last:
            pieces.append(table[:, first:last + 1])
        else:
            pieces.append(jnp.flip(table[:, last:first + 1], axis=1))
        i = j
    return jnp.concatenate(pieces, axis=1)


def _toeplitz(e, n_rows, n_cols):
    h, n = e.shape
    z = jnp.concatenate([e, jnp.zeros((h, 1), e.dtype)], axis=1)
    shifted = jnp.tile(z, (1, n_rows))[:, :n_rows * n].reshape(h, n_rows, n)
    return shifted[:, :, n_rows - 1:n_rows - 1 + n_cols]


def _t5_bucket(rel):
    nb = T5_BUCKETS // 2
    exact = nb // 2
    n = np.abs(rel)
    nf = np.maximum(n, exact).astype(np.float32)
    large = exact + (np.log(nf / exact) / math.log(T5_MAX_DIST / exact) * (nb - exact)).astype(np.int32)
    return np.where(rel > 0, nb, 0) + np.where(n < exact, n, np.minimum(large, nb - 1))


def _bias_a(t5_table):
    e = _lookup_static(t5_table.T.astype(F32), _t5_bucket(_band_offsets(A_LEFT)))
    return _toeplitz(e, CHUNK, (A_LEFT + 1) * CHUNK)


def _bias_c(rel_table):
    idx = np.clip(-_band_offsets(C_LEFT), -REL_CLIP, REL_CLIP) + REL_CLIP
    return _toeplitz(_lookup_static(rel_table.astype(F32), idx), CHUNK, (C_LEFT + 1) * CHUNK)


def _pad_rows(w, start, total):
    return jnp.zeros((total, w.shape[1]), w.dtype).at[start:start + w.shape[0]].set(w)


def kernel(x_prompt, x_sample, cache_a_k, cache_a_v, state_b_wkv, state_b_shift, cache_c_k, cache_c_v, state_d_conv, norm_mix_g, norm_ffn_g, norm_final_g, t5_table, w_in_e, w_out_e, a_sink, b_mu, b_w0, b_w2, b_a0, b_a2, b_g2, b_kk, b_ka, b_rk, b_lnx_w, b_lnx_b, b_v0, b_v1, b_v2, w_in_o, w_out_o, c_rel_table, d_dw_w, d_dw_b, d_ln_g, d_ln_b, ffn_w_gate, ffn_w_up, ffn_w_down):
    depth, d_model = norm_mix_g.shape
    h_a = a_sink.shape[1]
    g_a = h_a // KVH_A
    qa = h_a * HEAD_DIM
    kva = KVH_A * HEAD_DIM
    h_b = state_b_wkv.shape[2]
    cb = h_b * HEAD_DIM
    h_c = c_rel_table.shape[1]
    qc = h_c * HEAD_DIM
    cd = d_dw_w.shape[2]
    d_ff = ffn_w_gate.shape[2]
    tf = d_ff // 2 if (d_ff // 2) % 128 == 0 else d_ff

    groups = [x_prompt, x_sample]
    dims = [x.shape[:2] for x in groups]
    xs = [x.reshape(-1, d_model) for x in groups]
    tms = [min(TOKEN_ROWS, x.shape[0]) for x in xs]
    wg, wu, wd = ffn_w_gate.astype(BF16), ffn_w_up.astype(BF16), ffn_w_down.astype(BF16)
    conv_tq = [min(CONV_TQ, t) for _, t in dims]

    o_q, o_k, o_v, o_pb = 0, qa, qa + kva, qa + 2 * kva
    o_r, o_wd, o_kb, o_vb = o_pb, o_pb + cb, o_pb + cb + DECAY_LORA, o_pb + 2 * cb + DECAY_LORA
    o_ad = o_vb + cb
    o_gd = o_ad + ICLR_LORA
    perm_ranges = [(o_q, qa), (o_r, cb), (o_kb, cb), (o_vb, cb), (o_k, kva), (o_v, kva),
                   (o_wd, DECAY_LORA), (o_ad, ICLR_LORA), (o_gd, GATE_LORA)]
    n_r, n_kb, n_vb = qa, qa + cb, qa + 2 * cb
    n_ka = qa + 3 * cb
    n_va = n_ka + kva
    n_lo = n_va + kva
    pb_pieces = [[(o_r, cb)], [(o_kb, cb)], [(o_vb, cb)], [(o_wd, DECAY_LORA), (o_ad, ICLR_LORA), (o_gd, GATE_LORA)]]

    def split_pb(z):
        return [jnp.concatenate([z[..., a - o_pb:a - o_pb + n] for a, n in piece], axis=-1) for piece in pb_pieces]

    def join_pb(r_, k_, v_, lo_):
        return jnp.concatenate([r_, lo_[..., :DECAY_LORA], k_, v_, lo_[..., DECAY_LORA:]], axis=-1)

    bias_a = _bias_a(t5_table) * LOG2E
    st = [[[] for _ in range(7)] for _ in range(2)]
    v_first = [None, None]
    for i in range(depth):
        j = i // 2
        last = i == depth - 1
        if i % 2 == 0:
            w_in = jnp.concatenate([w_in_e[j][:, a:a + n] for a, n in perm_ranges], axis=1).astype(BF16)
            wo = w_out_e[j].astype(BF16)
            mu_r, mu_k, mu_v, mu_lo = split_pb(b_mu[j])
            rows = {PV_MU_R: mu_r, PV_MU_K: mu_k, PV_MU_V: mu_v, PV_W0: b_w0[j],
                    PV_A0: b_a0[j], PV_KK: b_kk[j], PV_KA: b_ka[j], PV_LNW: b_lnx_w[j], PV_LNB: b_lnx_b[j],
                    PV_RK: b_rk[j].reshape(cb)}
            if j > 0:
                rows[PV_V0] = b_v0[j - 1]
            pvec = jnp.stack([rows.get(ri, jnp.zeros((cb,), F32)) for ri in range(PV_ROWS)])
            mu_l = mu_lo.reshape(1, LORA_W)
            w2p = _pad_rows(b_w2[j], 0, LORA_W).astype(BF16)
            a2p = _pad_rows(b_a2[j], DECAY_LORA, LORA_W).astype(BF16)
            g2p = _pad_rows(b_g2[j], DECAY_LORA + ICLR_LORA, LORA_W).astype(BF16)
            if j > 0:
                lora_v = b_v1.shape[2]
                v1p = jnp.zeros((cb, 128), F32).at[:, :lora_v].set(b_v1[j - 1]).astype(BF16)
                v2p = _pad_rows(b_v2[j - 1], 0, 128).astype(BF16)
            bias = bias_a
        else:
            w_in = w_in_o[j].astype(BF16)
            wo = w_out_o[j].astype(BF16)
            bias = _bias_c(c_rel_table[j]) * LOG2E
        for gi in range(2):
            b, t = dims[gi]
            p = proj_in(xs[gi], norm_mix_g[i], w_in, tms[gi]).reshape(b, t, -1)
            if i % 2 == 0:
                attn_args = dict(n_heads=h_a, group=g_a, kcol=n_ka // kva, vcol=n_va // kva)
                if gi == 0:
                    shift0 = [jnp.zeros((b, 1, w_), F32) for w_ in (cb, cb, cb, LORA_W)]
                    s0, s0_layer = jnp.zeros((1, b, h_b, HEAD_DIM, HEAD_DIM), F32), 0
                    o1 = attention_prompt(p, bias, a_sink[j] * LOG2E, n_left=A_LEFT, **attn_args)
                else:
                    shift0 = [z[:, None, :] for z in split_pb(state_b_shift[j])]
                    s0, s0_layer = state_b_wkv, j
                    o1 = attention_sample(p, cache_a_k, cache_a_v, j, bias, a_sink[j] * LOG2E, **attn_args)
                vres = None if j == 0 else (v1p, v2p, v_first[gi])
                o2, wkv, vf = rwkv_mixer(p, shift0, s0, s0_layer, pvec, mu_l, w2p, a2p, g2p, vres,
                                         cols=(n_r // cb, n_kb // cb, n_vb // cb, n_lo // LORA_W))
                if j == 0:
                    v_first[gi] = vf
                keep = min(A_LEFT * CHUNK, t)
                nk = p[:, t - keep:, n_ka:n_ka + kva].reshape(b, keep, KVH_A, HEAD_DIM)
                nv = p[:, t - keep:, n_va:n_va + kva].reshape(b, keep, KVH_A, HEAD_DIM)
                nshift = join_pb(*[p[:, t - 1, c0:c0 + w_] for c0, w_ in ((n_r, cb), (n_kb, cb), (n_vb, cb), (n_lo, LORA_W))])
                for s_list, val in zip(st[gi][0:4], (nk, nv, wkv, nshift)):
                    s_list.append(val)
            else:
                attn_args = dict(n_heads=h_c, group=1, kcol=1, vcol=2)
                if gi == 0:
                    conv_prev = None
                    o1 = attention_prompt(p, bias, None, n_left=C_LEFT, **attn_args)
                else:
                    conv_prev = state_d_conv[j]
                    o1 = attention_sample(p, cache_c_k, cache_c_v, j, bias, None, **attn_args)
                o2, nconv = conv_module(p, conv_prev, d_dw_w[j], d_dw_b[j], d_ln_g[j], d_ln_b[j],
                                        acol=3 * qc // cd, gcol=3 * qc // cd + 1, tq=conv_tq[gi])
                keep = min(C_LEFT * CHUNK, t)
                nk = p[:, t - keep:, qc:2 * qc].reshape(b, keep, h_c, HEAD_DIM)
                nv = p[:, t - keep:, 2 * qc:3 * qc].reshape(b, keep, h_c, HEAD_DIM)
                for s_list, val in zip(st[gi][4:7], (nk, nv, nconv)):
                    s_list.append(val)
            half = o1.shape[-1]
            xs[gi] = post(xs[gi], o1.reshape(-1, half), o2.reshape(-1, half), wo, norm_ffn_g[i], wg, wu, wd,
                          norm_final_g, i, tm=tms[gi], tf=tf, final_norm=last)
    y_prompt = xs[0].reshape(x_prompt.shape)
    y_sample = xs[1].reshape(x_sample.shape)
    (pak, pav, pbw, pbs, pck, pcv, pdc), (sak, sav, sbw, sbs, sck, scv, sdc) = [[jnp.stack(s) for s in g] for g in st]
    return (y_prompt, y_sample, pak, pav, pbw, pbs, pck, pcv, pdc, sak, sav, sbw, sbs, sck, scv, sdc)
```

```python
import functools
import math

import jax
import jax.numpy as jnp
import numpy as np
from jax import lax
from jax.experimental import pallas as pl
from jax.experimental.pallas import tpu as pltpu

F32 = jnp.float32
BF16 = jnp.bfloat16
HIGHEST = lax.Precision.HIGHEST

CHUNK = 64
HEAD_DIM = 64
RMS_EPS = 1e-6
NEG_INF = -1e30
LNX_EPS = 64e-5
LN_EPS = 1e-5
CONV_W = 31
T5_BUCKETS = 32
T5_MAX_DIST = 128
REL_CLIP = 128
A_LEFT = 2
C_LEFT = 8
KVH_A = 2
DECAY_LORA = 64
ICLR_LORA = 64
GATE_LORA = 128
LORA_W = DECAY_LORA + ICLR_LORA + GATE_LORA
CONV_HALO = 32
CONV_ROWS = 64
LOG2E = math.log2(math.e)
QK_SCALE = HEAD_DIM ** -0.5 * LOG2E
TOKEN_ROWS = 1024
CONV_TQ = 512
POST_ROWS = 256
PAIR_W = 2 * HEAD_DIM
ATTN_TQ = 256
ATTN_SAMPLE_BB = 4
ATTN_PROMPT_HEAD_GROUP = {False: 4, True: 8}

VMEM_LIMIT = 56 * 1024 * 1024


def _cparams(sem):
    return pltpu.CompilerParams(dimension_semantics=sem, vmem_limit_bytes=VMEM_LIMIT)


def _batch_block(b, limit):
    return max(d for d in range(1, limit + 1) if b % d == 0)


def _rms(x, g):
    return x * lax.rsqrt(jnp.mean(x * x, axis=-1, keepdims=True) + RMS_EPS) * g


def _dot(a, b):
    return jnp.dot(a, b, preferred_element_type=F32)


def _dot_hp(a, b):
    return jnp.dot(a, b, precision=HIGHEST, preferred_element_type=F32)


def _proj_in_kernel(x_ref, g_ref, w_ref, o_ref):
    h = _rms(x_ref[...], g_ref[...]).astype(BF16)
    o_ref[...] = _dot(h, w_ref[...])


def proj_in(x2d, g, w_bf16, tm):
    m, d = x2d.shape
    n = w_bf16.shape[1]
    return pl.pallas_call(
        _proj_in_kernel,
        grid=(m // tm,),
        in_specs=[pl.BlockSpec((tm, d), lambda i: (i, 0)),
                  pl.BlockSpec((1, d), lambda i: (0, 0)),
                  pl.BlockSpec((d, n), lambda i: (0, 0))],
        out_specs=pl.BlockSpec((tm, n), lambda i: (i, 0)),
        out_shape=jax.ShapeDtypeStruct((m, n), F32),
        compiler_params=_cparams(("parallel",)),
        name="proj_in",
    )(x2d, g.reshape(1, d), w_bf16)


def _post_kernel(x_ref, o1_ref, o2_ref, wo_ref, g_ref, wg_ref, wu_ref, wd_ref, gf_ref, out_ref, h_s, *, final_norm):
    f = pl.program_id(1)
    half = o1_ref.shape[-1]
    tm = h_s.shape[0]
    rs = min(tm, POST_ROWS)

    @pl.when(f == 0)
    def _():
        for r0 in range(0, tm, rs):
            rows = slice(r0, r0 + rs)
            xn = (x_ref[rows, :] + _dot(o1_ref[rows, :], wo_ref[0:half, :])
                  + _dot(o2_ref[rows, :], wo_ref[half:2 * half, :]))
            out_ref[rows, :] = xn
            h_s[rows, :] = _rms(xn, g_ref[...]).astype(BF16)

    for r0 in range(0, tm, rs):
        rows = slice(r0, r0 + rs)
        h = h_s[rows, :]
        gate = _dot(h, wg_ref[...])
        up = _dot(h, wu_ref[...])
        act = gate * jax.nn.sigmoid(gate) * up
        out_ref[rows, :] += _dot(act.astype(BF16), wd_ref[...])

    if final_norm:
        @pl.when(f == pl.num_programs(1) - 1)
        def _():
            for r0 in range(0, tm, rs):
                rows = slice(r0, r0 + rs)
                out_ref[rows, :] = _rms(out_ref[rows, :], gf_ref[...])


def post(x2d, o1, o2, wo, g, wg, wu, wd, gf, layer, *, tm, tf, final_norm):
    m, d = x2d.shape
    half = o1.shape[-1]
    dff = wg.shape[2]
    return pl.pallas_call(
        functools.partial(_post_kernel, final_norm=final_norm),
        grid=(m // tm, dff // tf),
        in_specs=[pl.BlockSpec((tm, d), lambda i, f: (i, 0)),
                  pl.BlockSpec((tm, half), lambda i, f: (i, 0)),
                  pl.BlockSpec((tm, half), lambda i, f: (i, 0)),
                  pl.BlockSpec((2 * half, d), lambda i, f: (0, 0)),
                  pl.BlockSpec((1, d), lambda i, f: (0, 0)),
                  pl.BlockSpec((None, d, tf), lambda i, f: (layer, 0, f)),
                  pl.BlockSpec((None, d, tf), lambda i, f: (layer, 0, f)),
                  pl.BlockSpec((None, tf, d), lambda i, f: (layer, f, 0)),
                  pl.BlockSpec((1, d), lambda i, f: (0, 0))],
        out_specs=pl.BlockSpec((tm, d), lambda i, f: (i, 0)),
        out_shape=jax.ShapeDtypeStruct((m, d), F32),
        scratch_shapes=[pltpu.VMEM((tm, d), BF16)],
        compiler_params=_cparams(("parallel", "arbitrary")),
        name="post",
    )(x2d, o1, o2, wo, g.reshape(1, d), wg, wu, wd, gf.reshape(1, d))


def _attn_prompt_kernel(*refs, n_left, tq, n_heads, group, has_sink):
    q_ref, k_ref, v_ref, bias_ref = refs[:4]
    sink_ref = refs[4] if has_sink else None
    o_ref = refs[5] if has_sink else refs[4]
    scratch = refs[6:] if has_sink else refs[5:]
    hist = n_left * CHUNK
    nk = hist + tq
    i = pl.program_id(1)
    nt = (((1,), (1,)), ((), ()))
    slot = lax.rem(i, 2)
    k_win, vt_win = scratch[0].at[slot], scratch[1].at[slot]
    k_next, vt_next = scratch[0].at[1 - slot], scratch[1].at[1 - slot]
    k_swap = scratch[2].at[slot] if group > 1 else None
    k_swap_next = scratch[2].at[1 - slot] if group > 1 else None

    @pl.when(i == 0)
    def _():
        k_win[0:hist, :] = jnp.zeros((hist, k_win.shape[1]), BF16)
        vt_win[:, 0:hist] = jnp.zeros((vt_win.shape[0], hist), BF16)
        if group > 1:
            k_swap[0:hist, :] = jnp.zeros((hist, k_swap.shape[1]), BF16)

    k_new = k_ref[0]
    k_win[hist:nk, :] = k_new.astype(BF16)
    if group > 1:
        k_swap[hist:nk, :] = pltpu.roll(k_new, HEAD_DIM, axis=1).astype(BF16)
    vt_win[:, hist:nk] = v_ref[0].T.astype(BF16)
    q = q_ref[0] * QK_SCALE
    lane_lo = lax.broadcasted_iota(jnp.int32, (1, PAIR_W), 1) < HEAD_DIM
    lane_hi = jnp.logical_not(lane_lo)

    head_group = ATTN_PROMPT_HEAD_GROUP[group > 1]
    for h0 in range(0, n_heads, head_group):
        hs = list(range(h0, h0 + head_group))
        qh = [jnp.where(lane_lo if h % 2 == 0 else lane_hi, q[:, (h // 2) * PAIR_W:(h // 2 + 1) * PAIR_W], 0.0)
              .astype(BF16) for h in hs]
        if group > 1:
            own = [h // group for h in hs]
            kt = [(k_win if kv == h % 2 else k_swap)[...] for kv, h in zip(own, hs)]
            vt = [vt_win[...] for _ in hs]
        else:
            own = [h % 2 for h in hs]
            kt = [k_win[:, (h // 2) * PAIR_W:(h // 2 + 1) * PAIR_W] for h in hs]
            vt = [vt_win[(h // 2) * PAIR_W:(h // 2 + 1) * PAIR_W, :] for h in hs]
        s = [lax.dot_general(k_, q_, nt, preferred_element_type=F32) + bias_ref[h] for k_, q_, h in zip(kt, qh, hs)]
        m = [jnp.max(z, axis=0, keepdims=True) for z in s]
        if has_sink:
            sk = [sink_ref[h:h + 1, :] for h in hs]
            m = [jnp.maximum(z, k_) for z, k_ in zip(m, sk)]
        e = [jnp.exp2(z - m_) for z, m_ in zip(s, m)]
        den = [jnp.sum(z, axis=0, keepdims=True) for z in e]
        if has_sink:
            den = [z + jnp.exp2(k_ - m_) for z, k_, m_ in zip(den, sk, m)]
        ot = [_dot(v_, z.astype(BF16))[o_ * HEAD_DIM:(o_ + 1) * HEAD_DIM, :] / d_
              for v_, z, o_, d_ in zip(vt, e, own, den)]
        for j in range(0, len(hs), 2):
            pair = hs[j] // 2
            o_ref[0, :, pair * PAIR_W:(pair + 1) * PAIR_W] = (
                jnp.concatenate([ot[j], ot[j + 1]], axis=0).T.astype(o_ref.dtype))

    k_next[0:hist, :] = k_win[tq:nk, :]
    vt_next[:, 0:hist] = vt_win[:, tq:nk]
    if group > 1:
        k_swap_next[0:hist, :] = k_swap[tq:nk, :]


def attention_prompt(p, bias, sink, *, n_left, n_heads, group, kcol, vcol):
    b, t, _ = p.shape
    kvw = (n_heads // group) * HEAD_DIM
    qw = n_heads * HEAD_DIM
    has_sink = sink is not None
    assert group == 1 or kvw == PAIR_W, "grouped-query path assumes two kv heads sharing one lane tile"
    tq = min(ATTN_TQ, t)
    nk = n_left * CHUNK + tq
    bias_full = jnp.concatenate(
        [jnp.pad(bias, ((0, 0), (qc * CHUNK, nk - qc * CHUNK - bias.shape[1]), (0, 0)), constant_values=NEG_INF)
         for qc in range(tq // CHUNK)], axis=2)
    hist = n_left * CHUNK
    n_var = -(-hist // tq) + 1
    row = jnp.arange(nk)[None, :, None]
    bias_var = jnp.stack([jnp.where(row >= hist - v * tq, bias_full, NEG_INF) for v in range(n_var - 1)] + [bias_full])
    sink_ops = [sink.reshape(n_heads, 1)] if has_sink else []
    sink_specs = [pl.BlockSpec((n_heads, 1), lambda bi, i: (0, 0))] if has_sink else []
    return pl.pallas_call(
        functools.partial(_attn_prompt_kernel, n_left=n_left, tq=tq, n_heads=n_heads, group=group, has_sink=has_sink),
        grid=(b, t // tq),
        in_specs=[pl.BlockSpec((1, tq, qw), lambda bi, i: (bi, i, 0)),
                  pl.BlockSpec((1, tq, kvw), lambda bi, i: (bi, i, kcol)),
                  pl.BlockSpec((1, tq, kvw), lambda bi, i: (bi, i, vcol)),
                  pl.BlockSpec((None,) + bias_full.shape, lambda bi, i: (jnp.minimum(i, n_var - 1), 0, 0, 0))]
        + sink_specs,
        out_specs=pl.BlockSpec((1, tq, qw), lambda bi, i: (bi, i, 0)),
        out_shape=jax.ShapeDtypeStruct((b, t, qw), BF16),
        scratch_shapes=([pltpu.VMEM((2, nk, kvw), BF16), pltpu.VMEM((2, kvw, nk), BF16)]
                        + ([pltpu.VMEM((2, nk, kvw), BF16)] if group > 1 else [])),
        compiler_params=_cparams(("parallel", "arbitrary")),
        name="attention_prompt",
    )(p, p, p, bias_var, *sink_ops)


def _attn_sample_kernel(*refs, n_heads, group, has_sink, bb):
    q_ref, kc_ref, vc_ref, kn_ref, vn_ref, bias_ref = refs[:6]
    sink_ref = refs[6] if has_sink else None
    o_ref = refs[-1]
    hist = kc_ref.shape[-1]
    nt = (((1,), (1,)), ((), ()))
    heads = range(n_heads)
    q_sl = [slice(h * HEAD_DIM, (h + 1) * HEAD_DIM) for h in heads]
    kv_sl = [slice((h // group) * HEAD_DIM, (h // group + 1) * HEAD_DIM) for h in heads]
    for bi in range(bb):
        q = (q_ref[bi] * QK_SCALE).astype(BF16)
        kn = kn_ref[bi].astype(BF16)
        vn = vn_ref[bi].astype(BF16)
        qh = [q[:, sl] for sl in q_sl]
        s = [jnp.concatenate([_dot(qh[h], kc_ref[bi, h // group].astype(BF16)),
                              lax.dot_general(qh[h], kn[:, kv_sl[h]], nt, preferred_element_type=F32)], axis=1)
             + bias_ref[h] for h in heads]
        m = [jnp.max(z, axis=-1, keepdims=True) for z in s]
        if has_sink:
            sk = [sink_ref[h:h + 1, :] for h in heads]
            m = [jnp.maximum(z, k_) for z, k_ in zip(m, sk)]
        e = [jnp.exp2(z - m_) for z, m_ in zip(s, m)]
        den = [jnp.sum(z, axis=-1, keepdims=True) for z in e]
        if has_sink:
            den = [z + jnp.exp2(k_ - m_) for z, k_, m_ in zip(den, sk, m)]
        for h in heads:
            eb = e[h].astype(BF16)
            o = (lax.dot_general(eb[:, 0:hist], vc_ref[bi, h // group].astype(BF16), nt, preferred_element_type=F32)
                 + _dot(eb[:, hist:], vn[:, kv_sl[h]]))
            o_ref[bi, :, q_sl[h]] = (o / den[h]).astype(o_ref.dtype)


def attention_sample(p, k_cache, v_cache, layer, bias, sink, *, n_heads, group, kcol, vcol):
    b, t, _ = p.shape
    qw = n_heads * HEAD_DIM
    kv_heads = n_heads // group
    kvw = kv_heads * HEAD_DIM
    hist = k_cache.shape[2]
    has_sink = sink is not None
    bb = _batch_block(b, ATTN_SAMPLE_BB)
    transposed = lambda z: jnp.transpose(z, (0, 1, 3, 4, 2))
    cache_spec = pl.BlockSpec((None, bb, kv_heads, HEAD_DIM, hist), lambda i: (layer, i, 0, 0, 0))
    sink_ops = [sink.reshape(n_heads, 1)] if has_sink else []
    sink_specs = [pl.BlockSpec((n_heads, 1), lambda i: (0, 0))] if has_sink else []
    return pl.pallas_call(
        functools.partial(_attn_sample_kernel, n_heads=n_heads, group=group, has_sink=has_sink, bb=bb),
        grid=(b // bb,),
        in_specs=[pl.BlockSpec((bb, t, qw), lambda i: (i, 0, 0)), cache_spec, cache_spec,
                  pl.BlockSpec((bb, t, kvw), lambda i: (i, 0, kcol)),
                  pl.BlockSpec((bb, t, kvw), lambda i: (i, 0, vcol)),
                  pl.BlockSpec(bias.shape, lambda i: (0, 0, 0))] + sink_specs,
        out_specs=pl.BlockSpec((bb, t, qw), lambda i: (i, 0, 0)),
        out_shape=jax.ShapeDtypeStruct((b, t, qw), BF16),
        compiler_params=_cparams(("parallel",)),
        name="attention_sample",
    )(p, transposed(k_cache), transposed(v_cache), p, p, bias, *sink_ops)


def _conv_kernel(*refs, tq, from_state):
    if from_state:
        a_ref, gt_ref, st_in_ref, dw_ref, db_ref, lg_ref, lb_ref, o_ref, st_ref, buf = refs
    else:
        a_ref, gt_ref, pa_ref, pg_ref, dw_ref, db_ref, lg_ref, lb_ref, o_ref, st_ref, buf = refs
    keep = CONV_W - 1
    pad = CONV_HALO - keep
    if from_state:
        buf[0:pad, :] = jnp.zeros((pad, buf.shape[1]), F32)
        buf[pad:CONV_HALO, :] = st_in_ref[0]
    else:
        prev = pa_ref[0] * jax.nn.sigmoid(pg_ref[0])
        buf[0:CONV_HALO, :] = jnp.where(pl.program_id(1) > 0, prev, 0.0)
    buf[CONV_HALO:CONV_HALO + tq, :] = a_ref[0] * jax.nn.sigmoid(gt_ref[0])
    rs = min(tq, CONV_ROWS)
    for r0 in range(0, tq, rs):
        z = jnp.zeros((rs, buf.shape[1]), F32) + db_ref[...]
        base = buf[r0:r0 + rs + CONV_HALO, :]
        for s in range(8):
            rolled = pltpu.roll(base, rs + CONV_HALO - (pad + s), axis=0)
            for j, w in enumerate(range(s, CONV_W, 8)):
                z = z + rolled[8 * j:8 * j + rs, :] * dw_ref[w:w + 1, :]
        mean = jnp.mean(z, axis=-1, keepdims=True)
        zc = z - mean
        var = jnp.mean(zc * zc, axis=-1, keepdims=True)
        zn = zc * lax.rsqrt(var + LN_EPS) * lg_ref[...] + lb_ref[...]
        o_ref[0, r0:r0 + rs, :] = (zn * jax.nn.sigmoid(zn)).astype(o_ref.dtype)
    st_ref[0] = buf[tq + pad:tq + CONV_HALO, :]


def conv_module(p, state, dw_w, dw_b, ln_g, ln_b, *, acol, gcol, tq):
    b, t, _ = p.shape
    cd = dw_w.shape[1]
    keep = CONV_W - 1
    from_state = state is not None
    operands = [p, p]
    in_specs = [pl.BlockSpec((1, tq, cd), lambda bi, i: (bi, i, acol)),
                pl.BlockSpec((1, tq, cd), lambda bi, i: (bi, i, gcol))]
    if from_state:
        operands.append(state)
        in_specs.append(pl.BlockSpec((1, keep, cd), lambda bi, i: (bi, 0, 0)))
    else:
        r = tq // CONV_HALO
        for col in (acol, gcol):
            operands.append(p)
            in_specs.append(pl.BlockSpec((1, CONV_HALO, cd),
                                         lambda bi, i, col=col: (bi, jnp.maximum(i * r - 1, 0), col)))
    operands += [dw_w, dw_b.reshape(1, cd), ln_g.reshape(1, cd), ln_b.reshape(1, cd)]
    in_specs += [pl.BlockSpec((CONV_W, cd), lambda bi, i: (0, 0))] + [pl.BlockSpec((1, cd), lambda bi, i: (0, 0))] * 3
    return pl.pallas_call(
        functools.partial(_conv_kernel, tq=tq, from_state=from_state),
        grid=(b, t // tq),
        in_specs=in_specs,
        out_specs=[pl.BlockSpec((1, tq, cd), lambda bi, i: (bi, i, 0)),
                   pl.BlockSpec((1, keep, cd), lambda bi, i: (bi, 0, 0))],
        out_shape=[jax.ShapeDtypeStruct((b, t, cd), BF16), jax.ShapeDtypeStruct((b, keep, cd), F32)],
        scratch_shapes=[pltpu.VMEM((CONV_HALO + tq, cd), F32)],
        compiler_params=_cparams(("parallel", "arbitrary")),
        name="conv_module",
    )(*operands)


PV_MU_R, PV_MU_K, PV_MU_V, PV_W0, PV_A0, PV_KK, PV_KA, PV_V0, PV_LNW, PV_LNB, PV_RK = range(11)
PV_ROWS = 16
RWKV_CHUNKS_PER_STEP = 4
RWKV_UNITS_PER_STEP = 32


def _shift_rows(x, first_row):
    rolled = pltpu.roll(x, 1, axis=0)
    row = lax.broadcasted_iota(jnp.int32, x.shape, 0)
    return jnp.where(row == 0, first_row, rolled)


def _rwkv_kernel(*refs, n_heads, has_vres, bb, n_cc):
    (xr_ref, xk_ref, xv_ref, xl_ref, shr_ref, shk_ref, shv_ref, shl_ref, s0_ref, pv_ref, mul_ref,
     w2_ref, a2_ref, g2_ref) = refs[:14]
    pos = 14
    if has_vres:
        v1_ref, v2_ref, vf_ref = refs[pos:pos + 3]
        pos += 3
    ob_ref, sout_ref = refs[pos:pos + 2]
    pos += 2
    if not has_vres:
        vf_out_ref = refs[pos]
        pos += 1
    s_s, lr_s, lk_s, lv_s, ll_s = refs[pos:pos + 5]
    L = CHUNK
    c = pl.program_id(1)

    n_pairs = n_heads // 2
    PW = 2 * HEAD_DIM

    @pl.when(c == 0)
    def _():
        zero = jnp.zeros((HEAD_DIM, HEAD_DIM), F32)
        for bi in range(bb):
            for p in range(n_pairs):
                top = jnp.concatenate([s0_ref[bi, 2 * p], zero], axis=1)
                bottom = jnp.concatenate([zero, s0_ref[bi, 2 * p + 1]], axis=1)
                s_s[bi * n_pairs + p] = jnp.concatenate([top, bottom], axis=0)
            lr_s[bi, 0:1, :] = shr_ref[bi]
            lk_s[bi, 0:1, :] = shk_ref[bi]
            lv_s[bi, 0:1, :] = shv_ref[bi]
            ll_s[bi, 0:1, :] = shl_ref[bi]

    tl = n_cc * L

    def token_shift(x_ref, last_s, mu, bi):
        x = x_ref[bi]
        prev = _shift_rows(x, last_s[bi, 0:1, :])
        last_s[bi, 0:1, :] = x[tl - 1:tl, :]
        return x + (prev - x) * mu

    pv = lambda i: pv_ref[i:i + 1, :]
    row = lax.broadcasted_iota(jnp.int32, (L, L), 0)
    col = lax.broadcasted_iota(jnp.int32, (L, L), 1)
    tri_incl = (row >= col).astype(F32)

    full = {}
    for bi in range(bb):
        r = token_shift(xr_ref, lr_s, pv(PV_MU_R), bi)
        k = token_shift(xk_ref, lk_s, pv(PV_MU_K), bi)
        v = token_shift(xv_ref, lv_s, pv(PV_MU_V), bi)
        lo = token_shift(xl_ref, ll_s, mul_ref[...], bi)

        zw = pv(PV_W0) + _dot(jnp.tanh(lo).astype(BF16), w2_ref[...])
        w = -(jnp.maximum(-zw, 0.0) + jnp.log(1.0 + jnp.exp(-jnp.abs(zw)))) - 0.5
        d = -jnp.exp(w)
        iclr = jax.nn.sigmoid(pv(PV_A0) + _dot(lo.astype(BF16), a2_ref[...]))
        g = _dot(jax.nn.sigmoid(lo).astype(BF16), g2_ref[...])
        if has_vres:
            mix = jax.nn.sigmoid(pv(PV_V0) + _dot(_dot(v.astype(BF16), v1_ref[...]).astype(BF16), v2_ref[...]))
            v = v + (vf_ref[bi] - v) * mix
        else:
            vf_out_ref[bi] = v
        kks = k * pv(PV_KK)
        kh_all = k * (1.0 + (iclr - 1.0) * pv(PV_KA))
        for cc in range(n_cc):
            rows = slice(cc * L, (cc + 1) * L)
            dc = d[rows, :]
            cs = _dot_hp(tri_incl, dc)
            cprev = cs - dc
            c_last = cs[L - 1:L, :]
            c_mid = cs[L // 2:L // 2 + 1, :]
            full[bi, cc] = dict(
                r=r[rows, :], v=v[rows, :], g=g[rows, :], iclr=iclr[rows, :], kks=kks[rows, :], kh=kh_all[rows, :],
                e_mid=jnp.exp(c_mid),
                e_mid_prev=jnp.exp(cprev - c_mid), e_mid_incl=jnp.exp(cs - c_mid), e_from_mid=jnp.exp(c_mid - cs),
                e_to_end=jnp.exp(c_last - cs),
                e_chunk=jnp.exp(c_last))

    lane_lo = lax.broadcasted_iota(jnp.int32, (1, PW), 1) < HEAD_DIM
    r2 = lax.broadcasted_iota(jnp.int32, (2 * L, 2 * L), 0)
    c2 = lax.broadcasted_iota(jnp.int32, (2 * L, 2 * L), 1)
    same_head = (r2 >= L) == (c2 >= L)
    strict_bd = same_head & (jnp.bitwise_and(r2, L - 1) > jnp.bitwise_and(c2, L - 1))
    eye_bd = (r2 == c2).astype(F32)
    incl_c = (lax.broadcasted_iota(jnp.int32, (L, 2 * L), 0)
              >= jnp.bitwise_and(lax.broadcasted_iota(jnp.int32, (L, 2 * L), 1), L - 1))

    def bd(x):
        return jnp.concatenate([jnp.where(lane_lo, x, 0.0), jnp.where(lane_lo, 0.0, x)], axis=0)

    def tile2(x):
        return jnp.concatenate([x, x], axis=0)

    def head_sum(x):
        lo_sum = jnp.sum(jnp.where(lane_lo, x, 0.0), axis=-1, keepdims=True)
        hi_sum = jnp.sum(jnp.where(lane_lo, 0.0, x), axis=-1, keepdims=True)
        return jnp.where(lane_lo, lo_sum, hi_sum)

    bf = lambda z: z.astype(BF16)
    dot_nt = lambda p_, q_: lax.dot_general(p_, q_, (((1,), (1,)), ((), ())), preferred_element_type=F32)
    dot_tn = lambda p_, q_: lax.dot_general(p_, q_, (((0,), (0,)), ((), ())), preferred_element_type=F32)

    units = [(bi, cc, p) for cc in range(n_cc) for bi in range(bb) for p in range(n_pairs)]
    sls = [slice(p * PW, (p + 1) * PW) for _, _, p in units]
    get = lambda name: [full[bi, cc][name][:, sl] for (bi, cc, _), sl in zip(units, sls)]
    mul = lambda xs_, ys_: [x_ * y_ for x_, y_ in zip(xs_, ys_)]

    kk = get("kks")
    kk = [z / jnp.maximum(jnp.sqrt(head_sum(z * z)), 1e-12) for z in kk]
    a_p = [-z for z in kk]
    b_p = mul(kk, get("iclr"))
    k_p, r_p, v_p = get("kh"), get("r"), get("v")
    at = mul(a_p, get("e_mid_prev"))
    rt = mul(r_p, get("e_mid_incl"))
    e_from_mid, e_to_end = get("e_from_mid"), get("e_to_end")
    bt, kt = mul(b_p, e_from_mid), mul(k_p, e_from_mid)
    bh, kh = mul(b_p, e_to_end), mul(k_p, e_to_end)

    cat0 = lambda x_, y_: jnp.concatenate([x_, y_], axis=0)
    cat1 = lambda x_, y_: jnp.concatenate([x_, y_], axis=1)
    at_bd = [bf(bd(z)) for z in at]
    g_a = [dot_nt(x_, bf(cat0(tile2(b_), tile2(k_)))) for x_, b_, k_ in zip(at_bd, bt, kt)]
    n_ab = [jnp.where(strict_bd, z[:, 0:PW], 0.0) for z in g_a]
    n_ak = [jnp.where(strict_bd, z[:, PW:2 * PW], 0.0) for z in g_a]
    rt_b = [bf(z) for z in rt]
    n_r = [jnp.where(cat1(incl_c, incl_c), dot_nt(x_, bf(cat0(bd(b_), bd(k_)))), 0.0)
           for x_, b_, k_ in zip(rt_b, bt, kt)]
    y_inv = [eye_bd + z for z in n_ab]
    pw = [_dot(bf(z), bf(z)) for z in n_ab]
    for _ in range(int(math.log2(L)) - 2):
        lvl = [_dot(bf(p_), bf(cat1(y_, p_))) for p_, y_ in zip(pw, y_inv)]
        y_inv = [y_ + z[:, 0:PW] for y_, z in zip(y_inv, lvl)]
        pw = [z[:, PW:2 * PW] for z in lvl]
    y_inv = [y_ + _dot(bf(p_), bf(y_)) for p_, y_ in zip(pw, y_inv)]
    v_bd = [bf(bd(z)) for z in v_p]
    t2 = [_dot(bf(x_), y_) for x_, y_ in zip(n_ak, v_bd)]
    w12 = [_dot(bf(x_), cat1(a_, bf(t_))) for x_, a_, t_ in zip(y_inv, at_bd, t2)]
    e_mid, e_chunk, g_p = get("e_mid"), get("e_chunk"), get("g")
    n_seq = bb * n_pairs
    state = [s_s[i] for i in range(n_seq)]
    y = []
    for cc in range(n_cc):
        ids = range(cc * n_seq, (cc + 1) * n_seq)
        s_b = [bf(state[q] * e_mid[i]) for q, i in enumerate(ids)]
        u = [dot_nt(bf(w12[i][:, 0:PW]), s_) + w12[i][:, PW:2 * PW] for i, s_ in zip(ids, s_b)]
        y += [dot_nt(rt_b[i], s_) + _dot(bf(n_r[i]), cat0(bf(u_), v_bd[i])) for i, s_, u_ in zip(ids, s_b, u)]
        upd = [dot_tn(bf(cat0(u_[0:L] + u_[L:2 * L], v_p[i])), bf(cat0(bh[i], kh[i]))) for i, u_ in zip(ids, u)]
        state = [state[q] * e_chunk[i] + jnp.where(same_head, d_, 0.0) for q, (i, d_) in enumerate(zip(ids, upd))]
    for q in range(n_seq):
        s_s[q] = state[q]
    for i, ((bi, cc, _), sl) in enumerate(zip(units, sls)):
        mean = head_sum(y[i]) * (1.0 / HEAD_DIM)
        yc = y[i] - mean
        var = head_sum(yc * yc) * (1.0 / HEAD_DIM)
        yn = yc * lax.rsqrt(var + LNX_EPS) * pv(PV_LNW)[:, sl] + pv(PV_LNB)[:, sl]
        bonus = head_sum(r_p[i] * k_p[i] * pv(PV_RK)[:, sl])
        ob_ref[bi, cc * L:(cc + 1) * L, sl] = ((yn + bonus * v_p[i]) * g_p[i]).astype(ob_ref.dtype)

    @pl.when(c == pl.num_programs(1) - 1)
    def _():
        for q in range(n_seq):
            bi, p = q // n_pairs, q % n_pairs
            s_p = s_s[q]
            sout_ref[bi, 2 * p] = s_p[0:HEAD_DIM, 0:HEAD_DIM]
            sout_ref[bi, 2 * p + 1] = s_p[HEAD_DIM:PW, HEAD_DIM:PW]


def rwkv_mixer(p, shift0, s0, layer, pvec, mu_l, w2p, a2p, g2p, vres, *, cols):
    b, t, _ = p.shape
    nc = t // CHUNK
    n_heads = s0.shape[2]
    cb = n_heads * HEAD_DIM
    rc, kc, vc, lc = cols
    has_vres = vres is not None
    n_cc = _batch_block(nc, RWKV_CHUNKS_PER_STEP)
    bb = _batch_block(b, RWKV_UNITS_PER_STEP // (n_cc * (n_heads // 2)))
    tok = lambda col, w: pl.BlockSpec((bb, n_cc * CHUNK, w), lambda bi, c, col=col: (bi, c, col))
    const2 = lambda shape: pl.BlockSpec(shape, lambda bi, c: (0, 0))
    perb = lambda w: pl.BlockSpec((bb, 1, w), lambda bi, c: (bi, 0, 0))
    state_spec = pl.BlockSpec((bb, n_heads, HEAD_DIM, HEAD_DIM), lambda bi, c: (bi, 0, 0, 0))
    state_in_spec = pl.BlockSpec((None, bb, n_heads, HEAD_DIM, HEAD_DIM), lambda bi, c: (layer, bi, 0, 0, 0))
    operands = [p, p, p, p, *shift0, s0, pvec, mu_l, w2p, a2p, g2p]
    in_specs = [tok(rc, cb), tok(kc, cb), tok(vc, cb), tok(lc, LORA_W),
                perb(cb), perb(cb), perb(cb), perb(LORA_W), state_in_spec,
                const2((PV_ROWS, cb)), const2((1, LORA_W)),
                const2((LORA_W, cb)), const2((LORA_W, cb)), const2((LORA_W, cb))]
    out_specs = [tok(0, cb), state_spec]
    out_shape = [jax.ShapeDtypeStruct((b, t, cb), BF16), jax.ShapeDtypeStruct(s0.shape[1:], F32)]
    if has_vres:
        v1p, v2p, v_first = vres
        operands += [v1p, v2p, v_first]
        in_specs += [const2(v1p.shape), const2(v2p.shape), tok(0, cb)]
    else:
        out_specs.append(tok(0, cb))
        out_shape.append(jax.ShapeDtypeStruct((b, t, cb), F32))
    outs = pl.pallas_call(
        functools.partial(_rwkv_kernel, n_heads=n_heads, has_vres=has_vres, bb=bb, n_cc=n_cc),
        grid=(b // bb, nc // n_cc),
        in_specs=in_specs,
        out_specs=out_specs,
        out_shape=out_shape,
        scratch_shapes=[pltpu.VMEM((bb * n_heads // 2, 2 * HEAD_DIM, 2 * HEAD_DIM), F32),
                        pltpu.VMEM((bb, 8, cb), F32), pltpu.VMEM((bb, 8, cb), F32), pltpu.VMEM((bb, 8, cb), F32),
                        pltpu.VMEM((bb, 8, LORA_W), F32)],
        compiler_params=_cparams(("parallel", "arbitrary")),
        name="rwkv_mixer",
    )(*operands)
    return (outs[0], outs[1], None) if has_vres else tuple(outs)


def _band_offsets(n_left):
    return np.arange(-(CHUNK - 1), (n_left + 1) * CHUNK) - n_left * CHUNK


def _lookup_static(table, idx):
    h = table.shape[0]
    pieces = []
    i, n = 0, len(idx)
    while i < n:
        j = i + 1
        step = int(idx[j] - idx[i]) if j < n else 0
        if step in (-1, 0, 1):
            while j < n and idx[j] - idx[j - 1] == step:
                j += 1
        first, last = int(idx[i]), int(idx[j - 1])
        if first == last:
            pieces.append(jnp.broadcast_to(table[:, first:first + 1], (h, j - i)))
        elif first < last:
            pieces.append(table[:, first:last + 1])
        else:
            pieces.append(jnp.flip(table[:, last:first + 1], axis=1))
        i = j
    return jnp.concatenate(pieces, axis=1)


def _toeplitz(e, n_rows, n_cols):
    h, n = e.shape
    z = jnp.concatenate([e, jnp.zeros((h, 1), e.dtype)], axis=1)
    shifted = jnp.tile(z, (1, n_rows))[:, :n_rows * n].reshape(h, n_rows, n)
    return shifted[:, :, n_rows - 1:n_rows - 1 + n_cols]


def _t5_bucket(rel):
    nb = T5_BUCKETS // 2
    exact = nb // 2
    n = np.abs(rel)
    nf = np.maximum(n, exact).astype(np.float32)
    large = exact + (np.log(nf / exact) / math.log(T5_MAX_DIST / exact) * (nb - exact)).astype(np.int32)
    return np.where(rel > 0, nb, 0) + np.where(n < exact, n, np.minimum(large, nb - 1))


def _band_bias_t(e, n_left):
    return _toeplitz(jnp.flip(e, axis=1), (n_left + 1) * CHUNK, CHUNK)


def _bias_a(t5_table):
    return _band_bias_t(_lookup_static(t5_table.T.astype(F32), _t5_bucket(_band_offsets(A_LEFT))), A_LEFT)


def _bias_c(rel_table):
    idx = np.clip(-_band_offsets(C_LEFT), -REL_CLIP, REL_CLIP) + REL_CLIP
    return _band_bias_t(_lookup_static(rel_table.astype(F32), idx), C_LEFT)


def _pad_rows(w, start, total):
    return jnp.zeros((total, w.shape[1]), w.dtype).at[start:start + w.shape[0]].set(w)


def kernel(x_prompt, x_sample, cache_a_k, cache_a_v, state_b_wkv, state_b_shift, cache_c_k, cache_c_v, state_d_conv, norm_mix_g, norm_ffn_g, norm_final_g, t5_table, w_in_e, w_out_e, a_sink, b_mu, b_w0, b_w2, b_a0, b_a2, b_g2, b_kk, b_ka, b_rk, b_lnx_w, b_lnx_b, b_v0, b_v1, b_v2, w_in_o, w_out_o, c_rel_table, d_dw_w, d_dw_b, d_ln_g, d_ln_b, ffn_w_gate, ffn_w_up, ffn_w_down):
    depth, d_model = norm_mix_g.shape
    h_a = a_sink.shape[1]
    g_a = h_a // KVH_A
    qa = h_a * HEAD_DIM
    kva = KVH_A * HEAD_DIM
    h_b = state_b_wkv.shape[2]
    cb = h_b * HEAD_DIM
    h_c = c_rel_table.shape[1]
    qc = h_c * HEAD_DIM
    cd = d_dw_w.shape[2]
    d_ff = ffn_w_gate.shape[2]
    tf = d_ff // 2 if (d_ff // 2) % 128 == 0 else d_ff

    groups = [x_prompt, x_sample]
    dims = [x.shape[:2] for x in groups]
    xs = [x.reshape(-1, d_model) for x in groups]
    tms = [min(TOKEN_ROWS, x.shape[0]) for x in xs]
    wg, wu, wd = ffn_w_gate.astype(BF16), ffn_w_up.astype(BF16), ffn_w_down.astype(BF16)
    conv_tq = [min(CONV_TQ, t) for _, t in dims]

    o_q, o_k, o_v, o_pb = 0, qa, qa + kva, qa + 2 * kva
    o_r, o_wd, o_kb, o_vb = o_pb, o_pb + cb, o_pb + cb + DECAY_LORA, o_pb + 2 * cb + DECAY_LORA
    o_ad = o_vb + cb
    o_gd = o_ad + ICLR_LORA
    perm_ranges = [(o_q, qa), (o_r, cb), (o_kb, cb), (o_vb, cb), (o_k, kva), (o_v, kva),
                   (o_wd, DECAY_LORA), (o_ad, ICLR_LORA), (o_gd, GATE_LORA)]
    n_r, n_kb, n_vb = qa, qa + cb, qa + 2 * cb
    n_ka = qa + 3 * cb
    n_va = n_ka + kva
    n_lo = n_va + kva
    pb_pieces = [[(o_r, cb)], [(o_kb, cb)], [(o_vb, cb)], [(o_wd, DECAY_LORA), (o_ad, ICLR_LORA), (o_gd, GATE_LORA)]]

    def split_pb(z):
        return [jnp.concatenate([z[..., a - o_pb:a - o_pb + n] for a, n in piece], axis=-1) for piece in pb_pieces]

    def join_pb(r_, k_, v_, lo_):
        return jnp.concatenate([r_, lo_[..., :DECAY_LORA], k_, v_, lo_[..., DECAY_LORA:]], axis=-1)

    bias_a = _bias_a(t5_table) * LOG2E
    st = [[[] for _ in range(7)] for _ in range(2)]
    v_first = [None, None]
    for i in range(depth):
        j = i // 2
        last = i == depth - 1
        if i % 2 == 0:
            w_in = jnp.concatenate([w_in_e[j][:, a:a + n] for a, n in perm_ranges], axis=1).astype(BF16)
            wo = w_out_e[j].astype(BF16)
            mu_r, mu_k, mu_v, mu_lo = split_pb(b_mu[j])
            rows = {PV_MU_R: mu_r, PV_MU_K: mu_k, PV_MU_V: mu_v, PV_W0: b_w0[j],
                    PV_A0: b_a0[j], PV_KK: b_kk[j], PV_KA: b_ka[j], PV_LNW: b_lnx_w[j], PV_LNB: b_lnx_b[j],
                    PV_RK: b_rk[j].reshape(cb)}
            if j > 0:
                rows[PV_V0] = b_v0[j - 1]
            pvec = jnp.stack([rows.get(ri, jnp.zeros((cb,), F32)) for ri in range(PV_ROWS)])
            mu_l = mu_lo.reshape(1, LORA_W)
            w2p = _pad_rows(b_w2[j], 0, LORA_W).astype(BF16)
            a2p = _pad_rows(b_a2[j], DECAY_LORA, LORA_W).astype(BF16)
            g2p = _pad_rows(b_g2[j], DECAY_LORA + ICLR_LORA, LORA_W).astype(BF16)
            if j > 0:
                lora_v = b_v1.shape[2]
                v1p = jnp.zeros((cb, 128), F32).at[:, :lora_v].set(b_v1[j - 1]).astype(BF16)
                v2p = _pad_rows(b_v2[j - 1], 0, 128).astype(BF16)
            bias = bias_a
        else:
            w_in = w_in_o[j].astype(BF16)
            wo = w_out_o[j].astype(BF16)
            bias = _bias_c(c_rel_table[j]) * LOG2E
        for gi in range(2):
            b, t = dims[gi]
            p = proj_in(xs[gi], norm_mix_g[i], w_in, tms[gi]).reshape(b, t, -1)
            if i % 2 == 0:
                attn_args = dict(n_heads=h_a, group=g_a, kcol=n_ka // kva, vcol=n_va // kva)
                if gi == 0:
                    shift0 = [jnp.zeros((b, 1, w_), F32) for w_ in (cb, cb, cb, LORA_W)]
                    s0, s0_layer = jnp.zeros((1, b, h_b, HEAD_DIM, HEAD_DIM), F32), 0
                    o1 = attention_prompt(p, bias, a_sink[j] * LOG2E, n_left=A_LEFT, **attn_args)
                else:
                    shift0 = [z[:, None, :] for z in split_pb(state_b_shift[j])]
                    s0, s0_layer = state_b_wkv, j
                    o1 = attention_sample(p, cache_a_k, cache_a_v, j, jnp.swapaxes(bias, 1, 2), a_sink[j] * LOG2E,
                                          **attn_args)
                vres = None if j == 0 else (v1p, v2p, v_first[gi])
                o2, wkv, vf = rwkv_mixer(p, shift0, s0, s0_layer, pvec, mu_l, w2p, a2p, g2p, vres,
                                         cols=(n_r // cb, n_kb // cb, n_vb // cb, n_lo // LORA_W))
                if j == 0:
                    v_first[gi] = vf
                keep = min(A_LEFT * CHUNK, t)
                nk = p[:, t - keep:, n_ka:n_ka + kva].reshape(b, keep, KVH_A, HEAD_DIM)
                nv = p[:, t - keep:, n_va:n_va + kva].reshape(b, keep, KVH_A, HEAD_DIM)
                nshift = join_pb(*[p[:, t - 1, c0:c0 + w_] for c0, w_ in ((n_r, cb), (n_kb, cb), (n_vb, cb), (n_lo, LORA_W))])
                for s_list, val in zip(st[gi][0:4], (nk, nv, wkv, nshift)):
                    s_list.append(val)
            else:
                attn_args = dict(n_heads=h_c, group=1, kcol=1, vcol=2)
                if gi == 0:
                    conv_prev = None
                    o1 = attention_prompt(p, bias, None, n_left=C_LEFT, **attn_args)
                else:
                    conv_prev = state_d_conv[j]
                    o1 = attention_sample(p, cache_c_k, cache_c_v, j, jnp.swapaxes(bias, 1, 2), None, **attn_args)
                o2, nconv = conv_module(p, conv_prev, d_dw_w[j], d_dw_b[j], d_ln_g[j], d_ln_b[j],
                                        acol=3 * qc // cd, gcol=3 * qc // cd + 1, tq=conv_tq[gi])
                keep = min(C_LEFT * CHUNK, t)
                nk = p[:, t - keep:, qc:2 * qc].reshape(b, keep, h_c, HEAD_DIM)
                nv = p[:, t - keep:, 2 * qc:3 * qc].reshape(b, keep, h_c, HEAD_DIM)
                for s_list, val in zip(st[gi][4:7], (nk, nv, nconv)):
                    s_list.append(val)
            half = o1.shape[-1]
            xs[gi] = post(xs[gi], o1.reshape(-1, half), o2.reshape(-1, half), wo, norm_ffn_g[i], wg, wu, wd,
                          norm_final_g, i, tm=tms[gi], tf=tf, final_norm=last)
    y_prompt = xs[0].reshape(x_prompt.shape)
    y_sample = xs[1].reshape(x_sample.shape)
    (pak, pav, pbw, pbs, pck, pcv, pdc), (sak, sav, sbw, sbs, sck, scv, sdc) = [[jnp.stack(s) for s in g] for g in st]
    return (y_prompt, y_sample, pak, pav, pbw, pbs, pck, pcv, pdc, sak, sav, sbw, sbs, sck, scv, sdc)
```

```python
import functools
import math

import jax
import jax.numpy as jnp
import numpy as np
from jax import lax
from jax.experimental import pallas as pl
from jax.experimental.pallas import tpu as pltpu

F32 = jnp.float32
BF16 = jnp.bfloat16

CHUNK = 64
HEAD_DIM = 64
RMS_EPS = 1e-6
NEG_INF = -1e30
LNX_EPS = 64e-5
LN_EPS = 1e-5
CONV_W = 31
T5_BUCKETS = 32
T5_MAX_DIST = 128
REL_CLIP = 128
A_LEFT = 2
C_LEFT = 8
KVH_A = 2
DECAY_LORA = 64
ICLR_LORA = 64
GATE_LORA = 128
LORA_W = DECAY_LORA + ICLR_LORA + GATE_LORA
CONV_HALO = 32
CONV_ROWS = 64
LOG2E = math.log2(math.e)
QK_SCALE = HEAD_DIM ** -0.5 * LOG2E
TOKEN_ROWS = 1024
CONV_TQ = 512
POST_ROWS = 256
PAIR_W = 2 * HEAD_DIM
ATTN_TQ = 256
ATTN_SAMPLE_BB = 4
ATTN_PROMPT_HEAD_GROUP = {False: 4, True: 8}

VMEM_LIMIT = 56 * 1024 * 1024


def _cparams(sem):
    return pltpu.CompilerParams(dimension_semantics=sem, vmem_limit_bytes=VMEM_LIMIT)


def _batch_block(b, limit):
    return max(d for d in range(1, limit + 1) if b % d == 0)


def _rms(x, g):
    return x * lax.rsqrt(jnp.mean(x * x, axis=-1, keepdims=True) + RMS_EPS) * g


def _dot(a, b):
    return jnp.dot(a, b, preferred_element_type=F32)


def _cumsum_rows(tri, x):
    hi = x.astype(BF16)
    r1 = x - hi.astype(F32)
    mid = r1.astype(BF16)
    lo = (r1 - mid.astype(F32)).astype(BF16)
    return _dot(tri, hi) + _dot(tri, mid) + _dot(tri, lo)


def _proj_in_kernel(x_ref, g_ref, w_ref, o_ref):
    h = _rms(x_ref[...], g_ref[...]).astype(BF16)
    o_ref[...] = _dot(h, w_ref[...])


def proj_in(x2d, g, w_bf16, tm):
    m, d = x2d.shape
    n = w_bf16.shape[1]
    return pl.pallas_call(
        _proj_in_kernel,
        grid=(m // tm,),
        in_specs=[pl.BlockSpec((tm, d), lambda i: (i, 0)),
                  pl.BlockSpec((1, d), lambda i: (0, 0)),
                  pl.BlockSpec((d, n), lambda i: (0, 0))],
        out_specs=pl.BlockSpec((tm, n), lambda i: (i, 0)),
        out_shape=jax.ShapeDtypeStruct((m, n), F32),
        compiler_params=_cparams(("parallel",)),
        name="proj_in",
    )(x2d, g.reshape(1, d), w_bf16)


def _post_kernel(x_ref, o1_ref, o2_ref, wo_ref, g_ref, wg_ref, wu_ref, wd_ref, gf_ref, out_ref, h_s, *, final_norm):
    f = pl.program_id(1)
    half = o1_ref.shape[-1]
    tm = h_s.shape[0]
    rs = min(tm, POST_ROWS)

    @pl.when(f == 0)
    def _():
        for r0 in range(0, tm, rs):
            rows = slice(r0, r0 + rs)
            xn = (x_ref[rows, :] + _dot(o1_ref[rows, :], wo_ref[0:half, :])
                  + _dot(o2_ref[rows, :], wo_ref[half:2 * half, :]))
            out_ref[rows, :] = xn
            h_s[rows, :] = _rms(xn, g_ref[...]).astype(BF16)

    for r0 in range(0, tm, rs):
        rows = slice(r0, r0 + rs)
        h = h_s[rows, :]
        gate = _dot(h, wg_ref[...])
        up = _dot(h, wu_ref[...])
        act = gate * jax.nn.sigmoid(gate) * up
        out_ref[rows, :] += _dot(act.astype(BF16), wd_ref[...])

    if final_norm:
        @pl.when(f == pl.num_programs(1) - 1)
        def _():
            for r0 in range(0, tm, rs):
                rows = slice(r0, r0 + rs)
                out_ref[rows, :] = _rms(out_ref[rows, :], gf_ref[...])


def post(x2d, o1, o2, wo, g, wg, wu, wd, gf, layer, *, tm, tf, final_norm):
    m, d = x2d.shape
    half = o1.shape[-1]
    dff = wg.shape[2]
    return pl.pallas_call(
        functools.partial(_post_kernel, final_norm=final_norm),
        grid=(m // tm, dff // tf),
        in_specs=[pl.BlockSpec((tm, d), lambda i, f: (i, 0)),
                  pl.BlockSpec((tm, half), lambda i, f: (i, 0)),
                  pl.BlockSpec((tm, half), lambda i, f: (i, 0)),
                  pl.BlockSpec((2 * half, d), lambda i, f: (0, 0)),
                  pl.BlockSpec((1, d), lambda i, f: (0, 0)),
                  pl.BlockSpec((None, d, tf), lambda i, f: (layer, 0, f)),
                  pl.BlockSpec((None, d, tf), lambda i, f: (layer, 0, f)),
                  pl.BlockSpec((None, tf, d), lambda i, f: (layer, f, 0)),
                  pl.BlockSpec((1, d), lambda i, f: (0, 0))],
        out_specs=pl.BlockSpec((tm, d), lambda i, f: (i, 0)),
        out_shape=jax.ShapeDtypeStruct((m, d), F32),
        scratch_shapes=[pltpu.VMEM((tm, d), BF16)],
        compiler_params=_cparams(("parallel", "arbitrary")),
        name="post",
    )(x2d, o1, o2, wo, g.reshape(1, d), wg, wu, wd, gf.reshape(1, d))


def _attn_prompt_kernel(*refs, n_left, tq, n_heads, group, has_sink):
    q_ref, k_ref, v_ref, bias_ref = refs[:4]
    sink_ref = refs[4] if has_sink else None
    o_ref = refs[5] if has_sink else refs[4]
    scratch = refs[6:] if has_sink else refs[5:]
    hist = n_left * CHUNK
    nk = hist + tq
    i = pl.program_id(1)
    nt = (((1,), (1,)), ((), ()))
    slot = lax.rem(i, 2)
    k_win, vt_win = scratch[0].at[slot], scratch[1].at[slot]
    k_next, vt_next = scratch[0].at[1 - slot], scratch[1].at[1 - slot]
    k_swap = scratch[2].at[slot] if group > 1 else None
    k_swap_next = scratch[2].at[1 - slot] if group > 1 else None

    @pl.when(i == 0)
    def _():
        k_win[0:hist, :] = jnp.zeros((hist, k_win.shape[1]), BF16)
        vt_win[:, 0:hist] = jnp.zeros((vt_win.shape[0], hist), BF16)
        if group > 1:
            k_swap[0:hist, :] = jnp.zeros((hist, k_swap.shape[1]), BF16)

    k_new = k_ref[0]
    k_win[hist:nk, :] = k_new.astype(BF16)
    if group > 1:
        k_swap[hist:nk, :] = pltpu.roll(k_new, HEAD_DIM, axis=1).astype(BF16)
    vt_win[:, hist:nk] = v_ref[0].T.astype(BF16)
    q = q_ref[0] * QK_SCALE
    lane_lo = lax.broadcasted_iota(jnp.int32, (1, PAIR_W), 1) < HEAD_DIM
    lane_hi = jnp.logical_not(lane_lo)

    head_group = ATTN_PROMPT_HEAD_GROUP[group > 1]
    for h0 in range(0, n_heads, head_group):
        hs = list(range(h0, h0 + head_group))
        qh = [jnp.where(lane_lo if h % 2 == 0 else lane_hi, q[:, (h // 2) * PAIR_W:(h // 2 + 1) * PAIR_W], 0.0)
              .astype(BF16) for h in hs]
        if group > 1:
            own = [h // group for h in hs]
            kt = [(k_win if kv == h % 2 else k_swap)[...] for kv, h in zip(own, hs)]
            vt = [vt_win[...] for _ in hs]
        else:
            own = [h % 2 for h in hs]
            kt = [k_win[:, (h // 2) * PAIR_W:(h // 2 + 1) * PAIR_W] for h in hs]
            vt = [vt_win[(h // 2) * PAIR_W:(h // 2 + 1) * PAIR_W, :] for h in hs]
        s = [lax.dot_general(k_, q_, nt, preferred_element_type=F32) + bias_ref[h] for k_, q_, h in zip(kt, qh, hs)]
        m = [jnp.max(z, axis=0, keepdims=True) for z in s]
        if has_sink:
            sk = [sink_ref[h:h + 1, :] for h in hs]
            m = [jnp.maximum(z, k_) for z, k_ in zip(m, sk)]
        e = [jnp.exp2(z - m_) for z, m_ in zip(s, m)]
        den = [jnp.sum(z, axis=0, keepdims=True) for z in e]
        if has_sink:
            den = [z + jnp.exp2(k_ - m_) for z, k_, m_ in zip(den, sk, m)]
        ot = [_dot(v_, z.astype(BF16))[o_ * HEAD_DIM:(o_ + 1) * HEAD_DIM, :] / d_
              for v_, z, o_, d_ in zip(vt, e, own, den)]
        for j in range(0, len(hs), 2):
            pair = hs[j] // 2
            o_ref[0, :, pair * PAIR_W:(pair + 1) * PAIR_W] = (
                jnp.concatenate([ot[j], ot[j + 1]], axis=0).T.astype(o_ref.dtype))

    k_next[0:hist, :] = k_win[tq:nk, :]
    vt_next[:, 0:hist] = vt_win[:, tq:nk]
    if group > 1:
        k_swap_next[0:hist, :] = k_swap[tq:nk, :]


def attention_prompt(p, bias, sink, *, n_left, n_heads, group, kcol, vcol):
    b, t, _ = p.shape
    kvw = (n_heads // group) * HEAD_DIM
    qw = n_heads * HEAD_DIM
    has_sink = sink is not None
    assert group == 1 or kvw == PAIR_W, "grouped-query path assumes two kv heads sharing one lane tile"
    tq = min(ATTN_TQ, t)
    nk = n_left * CHUNK + tq
    bias_full = jnp.concatenate(
        [jnp.pad(bias, ((0, 0), (qc * CHUNK, nk - qc * CHUNK - bias.shape[1]), (0, 0)), constant_values=NEG_INF)
         for qc in range(tq // CHUNK)], axis=2)
    hist = n_left * CHUNK
    n_var = -(-hist // tq) + 1
    row = jnp.arange(nk)[None, :, None]
    bias_var = jnp.stack([jnp.where(row >= hist - v * tq, bias_full, NEG_INF) for v in range(n_var - 1)] + [bias_full])
    sink_ops = [sink.reshape(n_heads, 1)] if has_sink else []
    sink_specs = [pl.BlockSpec((n_heads, 1), lambda bi, i: (0, 0))] if has_sink else []
    return pl.pallas_call(
        functools.partial(_attn_prompt_kernel, n_left=n_left, tq=tq, n_heads=n_heads, group=group, has_sink=has_sink),
        grid=(b, t // tq),
        in_specs=[pl.BlockSpec((1, tq, qw), lambda bi, i: (bi, i, 0)),
                  pl.BlockSpec((1, tq, kvw), lambda bi, i: (bi, i, kcol)),
                  pl.BlockSpec((1, tq, kvw), lambda bi, i: (bi, i, vcol)),
                  pl.BlockSpec((None,) + bias_full.shape, lambda bi, i: (jnp.minimum(i, n_var - 1), 0, 0, 0))]
        + sink_specs,
        out_specs=pl.BlockSpec((1, tq, qw), lambda bi, i: (bi, i, 0)),
        out_shape=jax.ShapeDtypeStruct((b, t, qw), BF16),
        scratch_shapes=([pltpu.VMEM((2, nk, kvw), BF16), pltpu.VMEM((2, kvw, nk), BF16)]
                        + ([pltpu.VMEM((2, nk, kvw), BF16)] if group > 1 else [])),
        compiler_params=_cparams(("parallel", "arbitrary")),
        name="attention_prompt",
    )(p, p, p, bias_var, *sink_ops)


def _attn_sample_kernel(*refs, n_heads, group, has_sink, bb):
    q_ref, kc_ref, vc_ref, kn_ref, vn_ref, bias_ref = refs[:6]
    sink_ref = refs[6] if has_sink else None
    o_ref = refs[-1]
    hist = kc_ref.shape[-1]
    nt = (((1,), (1,)), ((), ()))
    heads = range(n_heads)
    q_sl = [slice(h * HEAD_DIM, (h + 1) * HEAD_DIM) for h in heads]
    kv_sl = [slice((h // group) * HEAD_DIM, (h // group + 1) * HEAD_DIM) for h in heads]
    for bi in range(bb):
        q = (q_ref[bi] * QK_SCALE).astype(BF16)
        kn = kn_ref[bi].astype(BF16)
        vn = vn_ref[bi].astype(BF16)
        qh = [q[:, sl] for sl in q_sl]
        s = [jnp.concatenate([_dot(qh[h], kc_ref[bi, h // group].astype(BF16)),
                              lax.dot_general(qh[h], kn[:, kv_sl[h]], nt, preferred_element_type=F32)], axis=1)
             + bias_ref[h] for h in heads]
        m = [jnp.max(z, axis=-1, keepdims=True) for z in s]
        if has_sink:
            sk = [sink_ref[h:h + 1, :] for h in heads]
            m = [jnp.maximum(z, k_) for z, k_ in zip(m, sk)]
        e = [jnp.exp2(z - m_) for z, m_ in zip(s, m)]
        den = [jnp.sum(z, axis=-1, keepdims=True) for z in e]
        if has_sink:
            den = [z + jnp.exp2(k_ - m_) for z, k_, m_ in zip(den, sk, m)]
        for h in heads:
            eb = e[h].astype(BF16)
            o = (lax.dot_general(eb[:, 0:hist], vc_ref[bi, h // group].astype(BF16), nt, preferred_element_type=F32)
                 + _dot(eb[:, hist:], vn[:, kv_sl[h]]))
            o_ref[bi, :, q_sl[h]] = (o / den[h]).astype(o_ref.dtype)


def attention_sample(p, k_cache, v_cache, layer, bias, sink, *, n_heads, group, kcol, vcol):
    b, t, _ = p.shape
    qw = n_heads * HEAD_DIM
    kv_heads = n_heads // group
    kvw = kv_heads * HEAD_DIM
    hist = k_cache.shape[2]
    has_sink = sink is not None
    bb = _batch_block(b, ATTN_SAMPLE_BB)
    transposed = lambda z: jnp.transpose(z, (0, 1, 3, 4, 2))
    cache_spec = pl.BlockSpec((None, bb, kv_heads, HEAD_DIM, hist), lambda i: (layer, i, 0, 0, 0))
    sink_ops = [sink.reshape(n_heads, 1)] if has_sink else []
    sink_specs = [pl.BlockSpec((n_heads, 1), lambda i: (0, 0))] if has_sink else []
    return pl.pallas_call(
        functools.partial(_attn_sample_kernel, n_heads=n_heads, group=group, has_sink=has_sink, bb=bb),
        grid=(b // bb,),
        in_specs=[pl.BlockSpec((bb, t, qw), lambda i: (i, 0, 0)), cache_spec, cache_spec,
                  pl.BlockSpec((bb, t, kvw), lambda i: (i, 0, kcol)),
                  pl.BlockSpec((bb, t, kvw), lambda i: (i, 0, vcol)),
                  pl.BlockSpec(bias.shape, lambda i: (0, 0, 0))] + sink_specs,
        out_specs=pl.BlockSpec((bb, t, qw), lambda i: (i, 0, 0)),
        out_shape=jax.ShapeDtypeStruct((b, t, qw), BF16),
        compiler_params=_cparams(("parallel",)),
        name="attention_sample",
    )(p, transposed(k_cache), transposed(v_cache), p, p, bias, *sink_ops)


def _conv_kernel(*refs, tq, from_state):
    if from_state:
        a_ref, gt_ref, st_in_ref, dw_ref, db_ref, lg_ref, lb_ref, o_ref, st_ref, buf = refs
    else:
        a_ref, gt_ref, pa_ref, pg_ref, dw_ref, db_ref, lg_ref, lb_ref, o_ref, st_ref, buf = refs
    keep = CONV_W - 1
    pad = CONV_HALO - keep
    if from_state:
        buf[0:pad, :] = jnp.zeros((pad, buf.shape[1]), F32)
        buf[pad:CONV_HALO, :] = st_in_ref[0]
    else:
        prev = pa_ref[0] * jax.nn.sigmoid(pg_ref[0])
        buf[0:CONV_HALO, :] = jnp.where(pl.program_id(1) > 0, prev, 0.0)
    buf[CONV_HALO:CONV_HALO + tq, :] = a_ref[0] * jax.nn.sigmoid(gt_ref[0])
    rs = min(tq, CONV_ROWS)
    for r0 in range(0, tq, rs):
        z = jnp.zeros((rs, buf.shape[1]), F32) + db_ref[...]
        base = buf[r0:r0 + rs + CONV_HALO, :]
        for s in range(8):
            rolled = pltpu.roll(base, rs + CONV_HALO - (pad + s), axis=0)
            for j, w in enumerate(range(s, CONV_W, 8)):
                z = z + rolled[8 * j:8 * j + rs, :] * dw_ref[w:w + 1, :]
        mean = jnp.mean(z, axis=-1, keepdims=True)
        zc = z - mean
        var = jnp.mean(zc * zc, axis=-1, keepdims=True)
        zn = zc * lax.rsqrt(var + LN_EPS) * lg_ref[...] + lb_ref[...]
        o_ref[0, r0:r0 + rs, :] = (zn * jax.nn.sigmoid(zn)).astype(o_ref.dtype)
    st_ref[0] = buf[tq + pad:tq + CONV_HALO, :]


def conv_module(p, state, dw_w, dw_b, ln_g, ln_b, *, acol, gcol, tq):
    b, t, _ = p.shape
    cd = dw_w.shape[1]
    keep = CONV_W - 1
    from_state = state is not None
    operands = [p, p]
    in_specs = [pl.BlockSpec((1, tq, cd), lambda bi, i: (bi, i, acol)),
                pl.BlockSpec((1, tq, cd), lambda bi, i: (bi, i, gcol))]
    if from_state:
        operands.append(state)
        in_specs.append(pl.BlockSpec((1, keep, cd), lambda bi, i: (bi, 0, 0)))
    else:
        r = tq // CONV_HALO
        for col in (acol, gcol):
            operands.append(p)
            in_specs.append(pl.BlockSpec((1, CONV_HALO, cd),
                                         lambda bi, i, col=col: (bi, jnp.maximum(i * r - 1, 0), col)))
    operands += [dw_w, dw_b.reshape(1, cd), ln_g.reshape(1, cd), ln_b.reshape(1, cd)]
    in_specs += [pl.BlockSpec((CONV_W, cd), lambda bi, i: (0, 0))] + [pl.BlockSpec((1, cd), lambda bi, i: (0, 0))] * 3
    return pl.pallas_call(
        functools.partial(_conv_kernel, tq=tq, from_state=from_state),
        grid=(b, t // tq),
        in_specs=in_specs,
        out_specs=[pl.BlockSpec((1, tq, cd), lambda bi, i: (bi, i, 0)),
                   pl.BlockSpec((1, keep, cd), lambda bi, i: (bi, 0, 0))],
        out_shape=[jax.ShapeDtypeStruct((b, t, cd), BF16), jax.ShapeDtypeStruct((b, keep, cd), F32)],
        scratch_shapes=[pltpu.VMEM((CONV_HALO + tq, cd), F32)],
        compiler_params=_cparams(("parallel", "arbitrary")),
        name="conv_module",
    )(*operands)


PV_MU_R, PV_MU_K, PV_MU_V, PV_W0, PV_A0, PV_KK, PV_KA, PV_V0, PV_LNW, PV_LNB, PV_RK = range(11)
PV_ROWS = 16
RWKV_CHUNKS_PER_STEP = 4
RWKV_UNITS_PER_STEP = 32


def _shift_rows(x, first_row):
    rolled = pltpu.roll(x, 1, axis=0)
    row = lax.broadcasted_iota(jnp.int32, x.shape, 0)
    return jnp.where(row == 0, first_row, rolled)


def _rwkv_kernel(*refs, n_heads, has_vres, bb, n_cc):
    (xr_ref, xk_ref, xv_ref, xl_ref, shr_ref, shk_ref, shv_ref, shl_ref, s0_ref, pv_ref, mul_ref,
     w2_ref, a2_ref, g2_ref) = refs[:14]
    pos = 14
    if has_vres:
        v1_ref, v2_ref, vf_ref = refs[pos:pos + 3]
        pos += 3
    ob_ref, sout_ref = refs[pos:pos + 2]
    pos += 2
    if not has_vres:
        vf_out_ref = refs[pos]
        pos += 1
    s_s, lr_s, lk_s, lv_s, ll_s = refs[pos:pos + 5]
    L = CHUNK
    c = pl.program_id(1)

    n_pairs = n_heads // 2
    PW = 2 * HEAD_DIM

    @pl.when(c == 0)
    def _():
        zero = jnp.zeros((HEAD_DIM, HEAD_DIM), F32)
        for bi in range(bb):
            for p in range(n_pairs):
                top = jnp.concatenate([s0_ref[bi, 2 * p], zero], axis=1)
                bottom = jnp.concatenate([zero, s0_ref[bi, 2 * p + 1]], axis=1)
                s_s[bi * n_pairs + p] = jnp.concatenate([top, bottom], axis=0)
            lr_s[bi, 0:1, :] = shr_ref[bi]
            lk_s[bi, 0:1, :] = shk_ref[bi]
            lv_s[bi, 0:1, :] = shv_ref[bi]
            ll_s[bi, 0:1, :] = shl_ref[bi]

    tl = n_cc * L

    def token_shift(x_ref, last_s, mu, bi):
        x = x_ref[bi]
        prev = _shift_rows(x, last_s[bi, 0:1, :])
        last_s[bi, 0:1, :] = x[tl - 1:tl, :]
        return x + (prev - x) * mu

    pv = lambda i: pv_ref[i:i + 1, :]
    row = lax.broadcasted_iota(jnp.int32, (L, L), 0)
    col = lax.broadcasted_iota(jnp.int32, (L, L), 1)
    tri_incl = (row >= col).astype(BF16)

    full = {}
    for bi in range(bb):
        r = token_shift(xr_ref, lr_s, pv(PV_MU_R), bi)
        k = token_shift(xk_ref, lk_s, pv(PV_MU_K), bi)
        v = token_shift(xv_ref, lv_s, pv(PV_MU_V), bi)
        lo = token_shift(xl_ref, ll_s, mul_ref[...], bi)

        zw = pv(PV_W0) + _dot(jnp.tanh(lo).astype(BF16), w2_ref[...])
        w = -(jnp.maximum(-zw, 0.0) + jnp.log(1.0 + jnp.exp(-jnp.abs(zw)))) - 0.5
        d = -jnp.exp(w)
        iclr = jax.nn.sigmoid(pv(PV_A0) + _dot(lo.astype(BF16), a2_ref[...]))
        g = _dot(jax.nn.sigmoid(lo).astype(BF16), g2_ref[...])
        if has_vres:
            mix = jax.nn.sigmoid(pv(PV_V0) + _dot(_dot(v.astype(BF16), v1_ref[...]).astype(BF16), v2_ref[...]))
            v = v + (vf_ref[bi] - v) * mix
        else:
            vf_out_ref[bi] = v
        kks = k * pv(PV_KK)
        kh_all = k * (1.0 + (iclr - 1.0) * pv(PV_KA))
        for cc in range(n_cc):
            rows = slice(cc * L, (cc + 1) * L)
            dc = d[rows, :]
            cs = _cumsum_rows(tri_incl, dc)
            cprev = cs - dc
            c_last = cs[L - 1:L, :]
            c_mid = cs[L // 2:L // 2 + 1, :]
            full[bi, cc] = dict(
                r=r[rows, :], v=v[rows, :], g=g[rows, :], iclr=iclr[rows, :], kks=kks[rows, :], kh=kh_all[rows, :],
                e_mid=jnp.exp(c_mid),
                e_mid_prev=jnp.exp(cprev - c_mid), e_mid_incl=jnp.exp(cs - c_mid), e_from_mid=jnp.exp(c_mid - cs),
                e_to_end=jnp.exp(c_last - cs),
                e_chunk=jnp.exp(c_last))

    lane_lo = lax.broadcasted_iota(jnp.int32, (1, PW), 1) < HEAD_DIM
    r2 = lax.broadcasted_iota(jnp.int32, (2 * L, 2 * L), 0)
    c2 = lax.broadcasted_iota(jnp.int32, (2 * L, 2 * L), 1)
    same_head = (r2 >= L) == (c2 >= L)
    strict_bd = same_head & (jnp.bitwise_and(r2, L - 1) > jnp.bitwise_and(c2, L - 1))
    eye_bd = (r2 == c2).astype(F32)
    incl_c = (lax.broadcasted_iota(jnp.int32, (L, 2 * L), 0)
              >= jnp.bitwise_and(lax.broadcasted_iota(jnp.int32, (L, 2 * L), 1), L - 1))

    def bd(x):
        return jnp.concatenate([jnp.where(lane_lo, x, 0.0), jnp.where(lane_lo, 0.0, x)], axis=0)

    def tile2(x):
        return jnp.concatenate([x, x], axis=0)

    def head_sum(x):
        lo_sum = jnp.sum(jnp.where(lane_lo, x, 0.0), axis=-1, keepdims=True)
        hi_sum = jnp.sum(jnp.where(lane_lo, 0.0, x), axis=-1, keepdims=True)
        return jnp.where(lane_lo, lo_sum, hi_sum)

    bf = lambda z: z.astype(BF16)
    dot_nt = lambda p_, q_: lax.dot_general(p_, q_, (((1,), (1,)), ((), ())), preferred_element_type=F32)
    dot_tn = lambda p_, q_: lax.dot_general(p_, q_, (((0,), (0,)), ((), ())), preferred_element_type=F32)

    units = [(bi, cc, p) for cc in range(n_cc) for bi in range(bb) for p in range(n_pairs)]
    sls = [slice(p * PW, (p + 1) * PW) for _, _, p in units]
    get = lambda name: [full[bi, cc][name][:, sl] for (bi, cc, _), sl in zip(units, sls)]
    mul = lambda xs_, ys_: [x_ * y_ for x_, y_ in zip(xs_, ys_)]

    kk = get("kks")
    kk = [z / jnp.maximum(jnp.sqrt(head_sum(z * z)), 1e-12) for z in kk]
    a_p = [-z for z in kk]
    b_p = mul(kk, get("iclr"))
    k_p, r_p, v_p = get("kh"), get("r"), get("v")
    at = mul(a_p, get("e_mid_prev"))
    rt = mul(r_p, get("e_mid_incl"))
    e_from_mid, e_to_end = get("e_from_mid"), get("e_to_end")
    bt, kt = mul(b_p, e_from_mid), mul(k_p, e_from_mid)
    bh, kh = mul(b_p, e_to_end), mul(k_p, e_to_end)

    cat0 = lambda x_, y_: jnp.concatenate([x_, y_], axis=0)
    cat1 = lambda x_, y_: jnp.concatenate([x_, y_], axis=1)
    at_bd = [bf(bd(z)) for z in at]
    g_a = [dot_nt(x_, bf(cat0(tile2(b_), tile2(k_)))) for x_, b_, k_ in zip(at_bd, bt, kt)]
    n_ab = [jnp.where(strict_bd, z[:, 0:PW], 0.0) for z in g_a]
    n_ak = [jnp.where(strict_bd, z[:, PW:2 * PW], 0.0) for z in g_a]
    rt_b = [bf(z) for z in rt]
    n_r = [jnp.where(cat1(incl_c, incl_c), dot_nt(x_, bf(cat0(bd(b_), bd(k_)))), 0.0)
           for x_, b_, k_ in zip(rt_b, bt, kt)]
    y_inv = [eye_bd + z for z in n_ab]
    pw = [_dot(bf(z), bf(z)) for z in n_ab]
    for _ in range(int(math.log2(L)) - 2):
        lvl = [_dot(bf(p_), bf(cat1(y_, p_))) for p_, y_ in zip(pw, y_inv)]
        y_inv = [y_ + z[:, 0:PW] for y_, z in zip(y_inv, lvl)]
        pw = [z[:, PW:2 * PW] for z in lvl]
    y_inv = [y_ + _dot(bf(p_), bf(y_)) for p_, y_ in zip(pw, y_inv)]
    v_bd = [bf(bd(z)) for z in v_p]
    t2 = [_dot(bf(x_), y_) for x_, y_ in zip(n_ak, v_bd)]
    w12 = [_dot(bf(x_), cat1(a_, bf(t_))) for x_, a_, t_ in zip(y_inv, at_bd, t2)]
    e_mid, e_chunk, g_p = get("e_mid"), get("e_chunk"), get("g")
    n_seq = bb * n_pairs
    state = [s_s[i] for i in range(n_seq)]
    y = []
    for cc in range(n_cc):
        ids = range(cc * n_seq, (cc + 1) * n_seq)
        s_b = [bf(state[q] * e_mid[i]) for q, i in enumerate(ids)]
        u = [dot_nt(bf(w12[i][:, 0:PW]), s_) + w12[i][:, PW:2 * PW] for i, s_ in zip(ids, s_b)]
        y += [dot_nt(rt_b[i], s_) + _dot(bf(n_r[i]), cat0(bf(u_), v_bd[i])) for i, s_, u_ in zip(ids, s_b, u)]
        upd = [dot_tn(bf(cat0(u_[0:L] + u_[L:2 * L], v_p[i])), bf(cat0(bh[i], kh[i]))) for i, u_ in zip(ids, u)]
        state = [state[q] * e_chunk[i] + jnp.where(same_head, d_, 0.0) for q, (i, d_) in enumerate(zip(ids, upd))]
    for q in range(n_seq):
        s_s[q] = state[q]
    for i, ((bi, cc, _), sl) in enumerate(zip(units, sls)):
        mean = head_sum(y[i]) * (1.0 / HEAD_DIM)
        yc = y[i] - mean
        var = head_sum(yc * yc) * (1.0 / HEAD_DIM)
        yn = yc * lax.rsqrt(var + LNX_EPS) * pv(PV_LNW)[:, sl] + pv(PV_LNB)[:, sl]
        bonus = head_sum(r_p[i] * k_p[i] * pv(PV_RK)[:, sl])
        ob_ref[bi, cc * L:(cc + 1) * L, sl] = ((yn + bonus * v_p[i]) * g_p[i]).astype(ob_ref.dtype)

    @pl.when(c == pl.num_programs(1) - 1)
    def _():
        for q in range(n_seq):
            bi, p = q // n_pairs, q % n_pairs
            s_p = s_s[q]
            sout_ref[bi, 2 * p] = s_p[0:HEAD_DIM, 0:HEAD_DIM]
            sout_ref[bi, 2 * p + 1] = s_p[HEAD_DIM:PW, HEAD_DIM:PW]


def rwkv_mixer(p, shift0, s0, layer, pvec, mu_l, w2p, a2p, g2p, vres, *, cols):
    b, t, _ = p.shape
    nc = t // CHUNK
    n_heads = s0.shape[2]
    cb = n_heads * HEAD_DIM
    rc, kc, vc, lc = cols
    has_vres = vres is not None
    n_cc = _batch_block(nc, RWKV_CHUNKS_PER_STEP)
    bb = _batch_block(b, RWKV_UNITS_PER_STEP // (n_cc * (n_heads // 2)))
    tok = lambda col, w: pl.BlockSpec((bb, n_cc * CHUNK, w), lambda bi, c, col=col: (bi, c, col))
    const2 = lambda shape: pl.BlockSpec(shape, lambda bi, c: (0, 0))
    perb = lambda w: pl.BlockSpec((bb, 1, w), lambda bi, c: (bi, 0, 0))
    state_spec = pl.BlockSpec((bb, n_heads, HEAD_DIM, HEAD_DIM), lambda bi, c: (bi, 0, 0, 0))
    state_in_spec = pl.BlockSpec((None, bb, n_heads, HEAD_DIM, HEAD_DIM), lambda bi, c: (layer, bi, 0, 0, 0))
    operands = [p, p, p, p, *shift0, s0, pvec, mu_l, w2p, a2p, g2p]
    in_specs = [tok(rc, cb), tok(kc, cb), tok(vc, cb), tok(lc, LORA_W),
                perb(cb), perb(cb), perb(cb), perb(LORA_W), state_in_spec,
                const2((PV_ROWS, cb)), const2((1, LORA_W)),
                const2((LORA_W, cb)), const2((LORA_W, cb)), const2((LORA_W, cb))]
    out_specs = [tok(0, cb), state_spec]
    out_shape = [jax.ShapeDtypeStruct((b, t, cb), BF16), jax.ShapeDtypeStruct(s0.shape[1:], F32)]
    if has_vres:
        v1p, v2p, v_first = vres
        operands += [v1p, v2p, v_first]
        in_specs += [const2(v1p.shape), const2(v2p.shape), tok(0, cb)]
    else:
        out_specs.append(tok(0, cb))
        out_shape.append(jax.ShapeDtypeStruct((b, t, cb), F32))
    outs = pl.pallas_call(
        functools.partial(_rwkv_kernel, n_heads=n_heads, has_vres=has_vres, bb=bb, n_cc=n_cc),
        grid=(b // bb, nc // n_cc),
        in_specs=in_specs,
        out_specs=out_specs,
        out_shape=out_shape,
        scratch_shapes=[pltpu.VMEM((bb * n_heads // 2, 2 * HEAD_DIM, 2 * HEAD_DIM), F32),
                        pltpu.VMEM((bb, 8, cb), F32), pltpu.VMEM((bb, 8, cb), F32), pltpu.VMEM((bb, 8, cb), F32),
                        pltpu.VMEM((bb, 8, LORA_W), F32)],
        compiler_params=_cparams(("parallel", "arbitrary")),
        name="rwkv_mixer",
    )(*operands)
    return (outs[0], outs[1], None) if has_vres else tuple(outs)


def _band_offsets(n_left):
    return np.arange(-(CHUNK - 1), (n_left + 1) * CHUNK) - n_left * CHUNK


def _lookup_static(table, idx):
    h = table.shape[0]
    pieces = []
    i, n = 0, len(idx)
    while i < n:
        j = i + 1
        step = int(idx[j] - idx[i]) if j < n else 0
        if step in (-1, 0, 1):
            while j < n and idx[j] - idx[j - 1] == step:
                j += 1
        first, last = int(idx[i]), int(idx[j - 1])
        if first == last:
            pieces.append(jnp.broadcast_to(table[:, first:first + 1], (h, j - i)))
        elif first < last:
            pieces.append(table[:, first:last + 1])
        else:
            pieces.append(jnp.flip(table[:, last:first + 1], axis=1))
        i = j
    return jnp.concatenate(pieces, axis=1)


def _toeplitz(e, n_rows, n_cols):
    h, n = e.shape
    z = jnp.concatenate([e, jnp.zeros((h, 1), e.dtype)], axis=1)
    shifted = jnp.tile(z, (1, n_rows))[:, :n_rows * n].reshape(h, n_rows, n)
    return shifted[:, :, n_rows - 1:n_rows - 1 + n_cols]


def _t5_bucket(rel):
    nb = T5_BUCKETS // 2
    exact = nb // 2
    n = np.abs(rel)
    nf = np.maximum(n, exact).astype(np.float32)
    large = exact + (np.log(nf / exact) / math.log(T5_MAX_DIST / exact) * (nb - exact)).astype(np.int32)
    return np.where(rel > 0, nb, 0) + np.where(n < exact, n, np.minimum(large, nb - 1))


def _band_bias_t(e, n_left):
    return jnp.swapaxes(_toeplitz(e, CHUNK, (n_left + 1) * CHUNK), 1, 2)


def _bias_a(t5_table):
    return _band_bias_t(_lookup_static(t5_table.T.astype(F32), _t5_bucket(_band_offsets(A_LEFT))), A_LEFT)


def _bias_c(rel_table):
    idx = np.clip(-_band_offsets(C_LEFT), -REL_CLIP, REL_CLIP) + REL_CLIP
    return _band_bias_t(_lookup_static(rel_table.astype(F32), idx), C_LEFT)


def _pad_rows(w, start, total):
    return jnp.zeros((total, w.shape[1]), w.dtype).at[start:start + w.shape[0]].set(w)


def kernel(x_prompt, x_sample, cache_a_k, cache_a_v, state_b_wkv, state_b_shift, cache_c_k, cache_c_v, state_d_conv, norm_mix_g, norm_ffn_g, norm_final_g, t5_table, w_in_e, w_out_e, a_sink, b_mu, b_w0, b_w2, b_a0, b_a2, b_g2, b_kk, b_ka, b_rk, b_lnx_w, b_lnx_b, b_v0, b_v1, b_v2, w_in_o, w_out_o, c_rel_table, d_dw_w, d_dw_b, d_ln_g, d_ln_b, ffn_w_gate, ffn_w_up, ffn_w_down):
    depth, d_model = norm_mix_g.shape
    h_a = a_sink.shape[1]
    g_a = h_a // KVH_A
    qa = h_a * HEAD_DIM
    kva = KVH_A * HEAD_DIM
    h_b = state_b_wkv.shape[2]
    cb = h_b * HEAD_DIM
    h_c = c_rel_table.shape[1]
    qc = h_c * HEAD_DIM
    cd = d_dw_w.shape[2]
    d_ff = ffn_w_gate.shape[2]
    tf = d_ff // 2 if (d_ff // 2) % 128 == 0 else d_ff

    groups = [x_prompt, x_sample]
    dims = [x.shape[:2] for x in groups]
    xs = [x.reshape(-1, d_model) for x in groups]
    tms = [min(TOKEN_ROWS, x.shape[0]) for x in xs]
    wg, wu, wd = ffn_w_gate.astype(BF16), ffn_w_up.astype(BF16), ffn_w_down.astype(BF16)
    conv_tq = [min(CONV_TQ, t) for _, t in dims]

    o_q, o_k, o_v, o_pb = 0, qa, qa + kva, qa + 2 * kva
    o_r, o_wd, o_kb, o_vb = o_pb, o_pb + cb, o_pb + cb + DECAY_LORA, o_pb + 2 * cb + DECAY_LORA
    o_ad = o_vb + cb
    o_gd = o_ad + ICLR_LORA
    perm_ranges = [(o_q, qa), (o_r, cb), (o_kb, cb), (o_vb, cb), (o_k, kva), (o_v, kva),
                   (o_wd, DECAY_LORA), (o_ad, ICLR_LORA), (o_gd, GATE_LORA)]
    n_r, n_kb, n_vb = qa, qa + cb, qa + 2 * cb
    n_ka = qa + 3 * cb
    n_va = n_ka + kva
    n_lo = n_va + kva
    pb_pieces = [[(o_r, cb)], [(o_kb, cb)], [(o_vb, cb)], [(o_wd, DECAY_LORA), (o_ad, ICLR_LORA), (o_gd, GATE_LORA)]]

    def split_pb(z):
        return [jnp.concatenate([z[..., a - o_pb:a - o_pb + n] for a, n in piece], axis=-1) for piece in pb_pieces]

    def join_pb(r_, k_, v_, lo_):
        return jnp.concatenate([r_, lo_[..., :DECAY_LORA], k_, v_, lo_[..., DECAY_LORA:]], axis=-1)

    bias_a = _bias_a(t5_table) * LOG2E
    st = [[[] for _ in range(7)] for _ in range(2)]
    v_first = [None, None]
    for i in range(depth):
        j = i // 2
        last = i == depth - 1
        if i % 2 == 0:
            w_in = jnp.concatenate([w_in_e[j][:, a:a + n] for a, n in perm_ranges], axis=1).astype(BF16)
            wo = w_out_e[j].astype(BF16)
            mu_r, mu_k, mu_v, mu_lo = split_pb(b_mu[j])
            rows = {PV_MU_R: mu_r, PV_MU_K: mu_k, PV_MU_V: mu_v, PV_W0: b_w0[j],
                    PV_A0: b_a0[j], PV_KK: b_kk[j], PV_KA: b_ka[j], PV_LNW: b_lnx_w[j], PV_LNB: b_lnx_b[j],
                    PV_RK: b_rk[j].reshape(cb)}
            if j > 0:
                rows[PV_V0] = b_v0[j - 1]
            pvec = jnp.stack([rows.get(ri, jnp.zeros((cb,), F32)) for ri in range(PV_ROWS)])
            mu_l = mu_lo.reshape(1, LORA_W)
            w2p = _pad_rows(b_w2[j], 0, LORA_W).astype(BF16)
            a2p = _pad_rows(b_a2[j], DECAY_LORA, LORA_W).astype(BF16)
            g2p = _pad_rows(b_g2[j], DECAY_LORA + ICLR_LORA, LORA_W).astype(BF16)
            if j > 0:
                lora_v = b_v1.shape[2]
                v1p = jnp.zeros((cb, 128), F32).at[:, :lora_v].set(b_v1[j - 1]).astype(BF16)
                v2p = _pad_rows(b_v2[j - 1], 0, 128).astype(BF16)
            bias = bias_a
        else:
            w_in = w_in_o[j].astype(BF16)
            wo = w_out_o[j].astype(BF16)
            bias = _bias_c(c_rel_table[j]) * LOG2E
        for gi in range(2):
            b, t = dims[gi]
            p = proj_in(xs[gi], norm_mix_g[i], w_in, tms[gi]).reshape(b, t, -1)
            if i % 2 == 0:
                attn_args = dict(n_heads=h_a, group=g_a, kcol=n_ka // kva, vcol=n_va // kva)
                if gi == 0:
                    shift0 = [jnp.zeros((b, 1, w_), F32) for w_ in (cb, cb, cb, LORA_W)]
                    s0, s0_layer = jnp.zeros((1, b, h_b, HEAD_DIM, HEAD_DIM), F32), 0
                    o1 = attention_prompt(p, bias, a_sink[j] * LOG2E, n_left=A_LEFT, **attn_args)
                else:
                    shift0 = [z[:, None, :] for z in split_pb(state_b_shift[j])]
                    s0, s0_layer = state_b_wkv, j
                    o1 = attention_sample(p, cache_a_k, cache_a_v, j, jnp.swapaxes(bias, 1, 2), a_sink[j] * LOG2E,
                                          **attn_args)
                vres = None if j == 0 else (v1p, v2p, v_first[gi])
                o2, wkv, vf = rwkv_mixer(p, shift0, s0, s0_layer, pvec, mu_l, w2p, a2p, g2p, vres,
                                         cols=(n_r // cb, n_kb // cb, n_vb // cb, n_lo // LORA_W))
                if j == 0:
                    v_first[gi] = vf
                keep = min(A_LEFT * CHUNK, t)
                nk = p[:, t - keep:, n_ka:n_ka + kva].reshape(b, keep, KVH_A, HEAD_DIM)
                nv = p[:, t - keep:, n_va:n_va + kva].reshape(b, keep, KVH_A, HEAD_DIM)
                nshift = join_pb(*[p[:, t - 1, c0:c0 + w_] for c0, w_ in ((n_r, cb), (n_kb, cb), (n_vb, cb), (n_lo, LORA_W))])
                for s_list, val in zip(st[gi][0:4], (nk, nv, wkv, nshift)):
                    s_list.append(val)
            else:
                attn_args = dict(n_heads=h_c, group=1, kcol=1, vcol=2)
                if gi == 0:
                    conv_prev = None
                    o1 = attention_prompt(p, bias, None, n_left=C_LEFT, **attn_args)
                else:
                    conv_prev = state_d_conv[j]
                    o1 = attention_sample(p, cache_c_k, cache_c_v, j, jnp.swapaxes(bias, 1, 2), None, **attn_args)
                o2, nconv = conv_module(p, conv_prev, d_dw_w[j], d_dw_b[j], d_ln_g[j], d_ln_b[j],
                                        acol=3 * qc // cd, gcol=3 * qc // cd + 1, tq=conv_tq[gi])
                keep = min(C_LEFT * CHUNK, t)
                nk = p[:, t - keep:, qc:2 * qc].reshape(b, keep, h_c, HEAD_DIM)
                nv = p[:, t - keep:, 2 * qc:3 * qc].reshape(b, keep, h_c, HEAD_DIM)
                for s_list, val in zip(st[gi][4:7], (nk, nv, nconv)):
                    s_list.append(val)
            half = o1.shape[-1]
            xs[gi] = post(xs[gi], o1.reshape(-1, half), o2.reshape(-1, half), wo, norm_ffn_g[i], wg, wu, wd,
                          norm_final_g, i, tm=tms[gi], tf=tf, final_norm=last)
    y_prompt = xs[0].reshape(x_prompt.shape)
    y_sample = xs[1].reshape(x_sample.shape)
    (pak, pav, pbw, pbs, pck, pcv, pdc), (sak, sav, sbw, sbs, sck, scv, sdc) = [[jnp.stack(s) for s in g] for g in st]
    return (y_prompt, y_sample, pak, pav, pbw, pbs, pck, pcv, pdc, sak, sav, sbw, sbs, sck, scv, sdc)
```

```python
import functools
import math

import jax
import jax.numpy as jnp
import numpy as np
from jax import lax
from jax.experimental import pallas as pl
from jax.experimental.pallas import tpu as pltpu

F32 = jnp.float32
BF16 = jnp.bfloat16

CHUNK = 64
HEAD_DIM = 64
RMS_EPS = 1e-6
NEG_INF = -1e30
LNX_EPS = 64e-5
LN_EPS = 1e-5
CONV_W = 31
T5_BUCKETS = 32
T5_MAX_DIST = 128
REL_CLIP = 128
A_LEFT = 2
C_LEFT = 8
KVH_A = 2
DECAY_LORA = 64
ICLR_LORA = 64
GATE_LORA = 128
LORA_W = DECAY_LORA + ICLR_LORA + GATE_LORA
CONV_HALO = 32
CONV_ROWS = 64
LOG2E = math.log2(math.e)
QK_SCALE = HEAD_DIM ** -0.5 * LOG2E
TOKEN_ROWS = 1024
CONV_TQ = 512
POST_ROWS = 256
PAIR_W = 2 * HEAD_DIM
ATTN_TQ = 256
ATTN_SAMPLE_BB = 4
ATTN_PROMPT_HEAD_GROUP = {False: 4, True: 8}

VMEM_LIMIT = 56 * 1024 * 1024


def _cparams(sem):
    return pltpu.CompilerParams(dimension_semantics=sem, vmem_limit_bytes=VMEM_LIMIT)


def _batch_block(b, limit):
    return max(d for d in range(1, limit + 1) if b % d == 0)


def _rms(x, g):
    return x * lax.rsqrt(jnp.mean(x * x, axis=-1, keepdims=True) + RMS_EPS) * g


def _dot(a, b):
    return jnp.dot(a, b, preferred_element_type=F32)


def _cumsum_rows(tri, x):
    hi = x.astype(BF16)
    r1 = x - hi.astype(F32)
    mid = r1.astype(BF16)
    lo = (r1 - mid.astype(F32)).astype(BF16)
    return _dot(tri, hi) + _dot(tri, mid) + _dot(tri, lo)


def _proj_in_kernel(x_ref, g_ref, w_ref, o_ref):
    h = _rms(x_ref[...], g_ref[...]).astype(BF16)
    o_ref[...] = _dot(h, w_ref[...])


def proj_in(x2d, g, w_bf16, tm):
    m, d = x2d.shape
    n = w_bf16.shape[1]
    return pl.pallas_call(
        _proj_in_kernel,
        grid=(m // tm,),
        in_specs=[pl.BlockSpec((tm, d), lambda i: (i, 0)),
                  pl.BlockSpec((1, d), lambda i: (0, 0)),
                  pl.BlockSpec((d, n), lambda i: (0, 0))],
        out_specs=pl.BlockSpec((tm, n), lambda i: (i, 0)),
        out_shape=jax.ShapeDtypeStruct((m, n), F32),
        compiler_params=_cparams(("parallel",)),
        name="proj_in",
    )(x2d, g.reshape(1, d), w_bf16)


def _post_kernel(x_ref, o1_ref, o2_ref, wo_ref, g_ref, wg_ref, wu_ref, wd_ref, gf_ref, out_ref, h_s, *, final_norm):
    f = pl.program_id(1)
    half = o1_ref.shape[-1]
    tm = h_s.shape[0]
    rs = min(tm, POST_ROWS)

    @pl.when(f == 0)
    def _():
        for r0 in range(0, tm, rs):
            rows = slice(r0, r0 + rs)
            xn = (x_ref[rows, :] + _dot(o1_ref[rows, :], wo_ref[0:half, :])
                  + _dot(o2_ref[rows, :], wo_ref[half:2 * half, :]))
            out_ref[rows, :] = xn
            h_s[rows, :] = _rms(xn, g_ref[...]).astype(BF16)

    for r0 in range(0, tm, rs):
        rows = slice(r0, r0 + rs)
        h = h_s[rows, :]
        gate = _dot(h, wg_ref[...])
        up = _dot(h, wu_ref[...])
        act = gate * jax.nn.sigmoid(gate) * up
        out_ref[rows, :] += _dot(act.astype(BF16), wd_ref[...])

    if final_norm:
        @pl.when(f == pl.num_programs(1) - 1)
        def _():
            for r0 in range(0, tm, rs):
                rows = slice(r0, r0 + rs)
                out_ref[rows, :] = _rms(out_ref[rows, :], gf_ref[...])


def post(x2d, o1, o2, wo, g, wg, wu, wd, gf, layer, *, tm, tf, final_norm):
    m, d = x2d.shape
    half = o1.shape[-1]
    dff = wg.shape[2]
    once = dict(pipeline_mode=pl.Buffered(1))
    w_mode = once if tf == dff else {}
    return pl.pallas_call(
        functools.partial(_post_kernel, final_norm=final_norm),
        grid=(m // tm, dff // tf),
        in_specs=[pl.BlockSpec((tm, d), lambda i, f: (i, 0)),
                  pl.BlockSpec((tm, half), lambda i, f: (i, 0)),
                  pl.BlockSpec((tm, half), lambda i, f: (i, 0)),
                  pl.BlockSpec((2 * half, d), lambda i, f: (0, 0), **once),
                  pl.BlockSpec((1, d), lambda i, f: (0, 0)),
                  pl.BlockSpec((None, d, tf), lambda i, f: (layer, 0, f), **w_mode),
                  pl.BlockSpec((None, d, tf), lambda i, f: (layer, 0, f), **w_mode),
                  pl.BlockSpec((None, tf, d), lambda i, f: (layer, f, 0), **w_mode),
                  pl.BlockSpec((1, d), lambda i, f: (0, 0))],
        out_specs=pl.BlockSpec((tm, d), lambda i, f: (i, 0)),
        out_shape=jax.ShapeDtypeStruct((m, d), F32),
        scratch_shapes=[pltpu.VMEM((tm, d), BF16)],
        compiler_params=_cparams(("parallel", "arbitrary")),
        name="post",
    )(x2d, o1, o2, wo, g.reshape(1, d), wg, wu, wd, gf.reshape(1, d))


def _attn_prompt_kernel(*refs, n_left, tq, n_heads, group, has_sink):
    q_ref, k_ref, v_ref, bias_ref = refs[:4]
    sink_ref = refs[4] if has_sink else None
    o_ref = refs[5] if has_sink else refs[4]
    scratch = refs[6:] if has_sink else refs[5:]
    hist = n_left * CHUNK
    nk = hist + tq
    i = pl.program_id(1)
    nt = (((1,), (1,)), ((), ()))
    slot = lax.rem(i, 2)
    k_win, vt_win = scratch[0].at[slot], scratch[1].at[slot]
    k_next, vt_next = scratch[0].at[1 - slot], scratch[1].at[1 - slot]
    k_swap = scratch[2].at[slot] if group > 1 else None
    k_swap_next = scratch[2].at[1 - slot] if group > 1 else None

    @pl.when(i == 0)
    def _():
        k_win[0:hist, :] = jnp.zeros((hist, k_win.shape[1]), BF16)
        vt_win[:, 0:hist] = jnp.zeros((vt_win.shape[0], hist), BF16)
        if group > 1:
            k_swap[0:hist, :] = jnp.zeros((hist, k_swap.shape[1]), BF16)

    k_new = k_ref[0]
    k_win[hist:nk, :] = k_new.astype(BF16)
    if group > 1:
        k_swap[hist:nk, :] = pltpu.roll(k_new, HEAD_DIM, axis=1).astype(BF16)
    vt_win[:, hist:nk] = v_ref[0].T.astype(BF16)
    q = q_ref[0] * QK_SCALE
    lane_lo = lax.broadcasted_iota(jnp.int32, (1, PAIR_W), 1) < HEAD_DIM
    lane_hi = jnp.logical_not(lane_lo)

    head_group = ATTN_PROMPT_HEAD_GROUP[group > 1]
    for h0 in range(0, n_heads, head_group):
        hs = list(range(h0, h0 + head_group))
        qh = [jnp.where(lane_lo if h % 2 == 0 else lane_hi, q[:, (h // 2) * PAIR_W:(h // 2 + 1) * PAIR_W], 0.0)
              .astype(BF16) for h in hs]
        if group > 1:
            own = [h // group for h in hs]
            kt = [(k_win if kv == h % 2 else k_swap)[...] for kv, h in zip(own, hs)]
            vt = [vt_win[...] for _ in hs]
        else:
            own = [h % 2 for h in hs]
            kt = [k_win[:, (h // 2) * PAIR_W:(h // 2 + 1) * PAIR_W] for h in hs]
            vt = [vt_win[(h // 2) * PAIR_W:(h // 2 + 1) * PAIR_W, :] for h in hs]
        s = [lax.dot_general(k_, q_, nt, preferred_element_type=F32) + bias_ref[h] for k_, q_, h in zip(kt, qh, hs)]
        m = [jnp.max(z, axis=0, keepdims=True) for z in s]
        if has_sink:
            sk = [sink_ref[h:h + 1, :] for h in hs]
            m = [jnp.maximum(z, k_) for z, k_ in zip(m, sk)]
        e = [jnp.exp2(z - m_) for z, m_ in zip(s, m)]
        den = [jnp.sum(z, axis=0, keepdims=True) for z in e]
        if has_sink:
            den = [z + jnp.exp2(k_ - m_) for z, k_, m_ in zip(den, sk, m)]
        ot = [_dot(v_, z.astype(BF16))[o_ * HEAD_DIM:(o_ + 1) * HEAD_DIM, :] / d_
              for v_, z, o_, d_ in zip(vt, e, own, den)]
        for j in range(0, len(hs), 2):
            pair = hs[j] // 2
            o_ref[0, :, pair * PAIR_W:(pair + 1) * PAIR_W] = (
                jnp.concatenate([ot[j], ot[j + 1]], axis=0).T.astype(o_ref.dtype))

    k_next[0:hist, :] = k_win[tq:nk, :]
    vt_next[:, 0:hist] = vt_win[:, tq:nk]
    if group > 1:
        k_swap_next[0:hist, :] = k_swap[tq:nk, :]


def attention_prompt(p, bias, sink, *, n_left, n_heads, group, kcol, vcol):
    b, t, _ = p.shape
    kvw = (n_heads // group) * HEAD_DIM
    qw = n_heads * HEAD_DIM
    has_sink = sink is not None
    assert group == 1 or kvw == PAIR_W, "grouped-query path assumes two kv heads sharing one lane tile"
    tq = min(ATTN_TQ, t)
    nk = n_left * CHUNK + tq
    bias_full = jnp.concatenate(
        [jnp.pad(bias, ((0, 0), (qc * CHUNK, nk - qc * CHUNK - bias.shape[1]), (0, 0)), constant_values=NEG_INF)
         for qc in range(tq // CHUNK)], axis=2)
    hist = n_left * CHUNK
    n_var = -(-hist // tq) + 1
    row = jnp.arange(nk)[None, :, None]
    bias_var = jnp.stack([jnp.where(row >= hist - v * tq, bias_full, NEG_INF) for v in range(n_var - 1)] + [bias_full])
    sink_ops = [sink.reshape(n_heads, 1)] if has_sink else []
    sink_specs = [pl.BlockSpec((n_heads, 1), lambda bi, i: (0, 0))] if has_sink else []
    return pl.pallas_call(
        functools.partial(_attn_prompt_kernel, n_left=n_left, tq=tq, n_heads=n_heads, group=group, has_sink=has_sink),
        grid=(b, t // tq),
        in_specs=[pl.BlockSpec((1, tq, qw), lambda bi, i: (bi, i, 0)),
                  pl.BlockSpec((1, tq, kvw), lambda bi, i: (bi, i, kcol)),
                  pl.BlockSpec((1, tq, kvw), lambda bi, i: (bi, i, vcol)),
                  pl.BlockSpec((None,) + bias_full.shape, lambda bi, i: (jnp.minimum(i, n_var - 1), 0, 0, 0))]
        + sink_specs,
        out_specs=pl.BlockSpec((1, tq, qw), lambda bi, i: (bi, i, 0)),
        out_shape=jax.ShapeDtypeStruct((b, t, qw), BF16),
        scratch_shapes=([pltpu.VMEM((2, nk, kvw), BF16), pltpu.VMEM((2, kvw, nk), BF16)]
                        + ([pltpu.VMEM((2, nk, kvw), BF16)] if group > 1 else [])),
        compiler_params=_cparams(("parallel", "arbitrary")),
        name="attention_prompt",
    )(p, p, p, bias_var, *sink_ops)


def _attn_sample_kernel(*refs, n_heads, group, has_sink, bb):
    q_ref, kc_ref, vc_ref, kn_ref, vn_ref, bias_ref = refs[:6]
    sink_ref = refs[6] if has_sink else None
    o_ref = refs[-1]
    hist = kc_ref.shape[-1]
    nt = (((1,), (1,)), ((), ()))
    heads = range(n_heads)
    q_sl = [slice(h * HEAD_DIM, (h + 1) * HEAD_DIM) for h in heads]
    kv_sl = [slice((h // group) * HEAD_DIM, (h // group + 1) * HEAD_DIM) for h in heads]
    for bi in range(bb):
        q = (q_ref[bi] * QK_SCALE).astype(BF16)
        kn = kn_ref[bi].astype(BF16)
        vn = vn_ref[bi].astype(BF16)
        qh = [q[:, sl] for sl in q_sl]
        s = [jnp.concatenate([_dot(qh[h], kc_ref[bi, h // group].astype(BF16)),
                              lax.dot_general(qh[h], kn[:, kv_sl[h]], nt, preferred_element_type=F32)], axis=1)
             + bias_ref[h] for h in heads]
        m = [jnp.max(z, axis=-1, keepdims=True) for z in s]
        if has_sink:
            sk = [sink_ref[h:h + 1, :] for h in heads]
            m = [jnp.maximum(z, k_) for z, k_ in zip(m, sk)]
        e = [jnp.exp2(z - m_) for z, m_ in zip(s, m)]
        den = [jnp.sum(z, axis=-1, keepdims=True) for z in e]
        if has_sink:
            den = [z + jnp.exp2(k_ - m_) for z, k_, m_ in zip(den, sk, m)]
        for h in heads:
            eb = e[h].astype(BF16)
            o = (lax.dot_general(eb[:, 0:hist], vc_ref[bi, h // group].astype(BF16), nt, preferred_element_type=F32)
                 + _dot(eb[:, hist:], vn[:, kv_sl[h]]))
            o_ref[bi, :, q_sl[h]] = (o / den[h]).astype(o_ref.dtype)


def attention_sample(p, k_cache, v_cache, layer, bias, sink, *, n_heads, group, kcol, vcol):
    b, t, _ = p.shape
    qw = n_heads * HEAD_DIM
    kv_heads = n_heads // group
    kvw = kv_heads * HEAD_DIM
    hist = k_cache.shape[2]
    has_sink = sink is not None
    bb = _batch_block(b, ATTN_SAMPLE_BB)
    transposed = lambda z: jnp.transpose(z, (0, 1, 3, 4, 2))
    cache_spec = pl.BlockSpec((None, bb, kv_heads, HEAD_DIM, hist), lambda i: (layer, i, 0, 0, 0))
    sink_ops = [sink.reshape(n_heads, 1)] if has_sink else []
    sink_specs = [pl.BlockSpec((n_heads, 1), lambda i: (0, 0))] if has_sink else []
    return pl.pallas_call(
        functools.partial(_attn_sample_kernel, n_heads=n_heads, group=group, has_sink=has_sink, bb=bb),
        grid=(b // bb,),
        in_specs=[pl.BlockSpec((bb, t, qw), lambda i: (i, 0, 0)), cache_spec, cache_spec,
                  pl.BlockSpec((bb, t, kvw), lambda i: (i, 0, kcol)),
                  pl.BlockSpec((bb, t, kvw), lambda i: (i, 0, vcol)),
                  pl.BlockSpec(bias.shape, lambda i: (0, 0, 0))] + sink_specs,
        out_specs=pl.BlockSpec((bb, t, qw), lambda i: (i, 0, 0)),
        out_shape=jax.ShapeDtypeStruct((b, t, qw), BF16),
        compiler_params=_cparams(("parallel",)),
        name="attention_sample",
    )(p, transposed(k_cache), transposed(v_cache), p, p, bias, *sink_ops)


def _conv_kernel(*refs, tq, from_state):
    if from_state:
        a_ref, gt_ref, st_in_ref, dw_ref, db_ref, lg_ref, lb_ref, o_ref, st_ref, buf = refs
    else:
        a_ref, gt_ref, pa_ref, pg_ref, dw_ref, db_ref, lg_ref, lb_ref, o_ref, st_ref, buf = refs
    keep = CONV_W - 1
    pad = CONV_HALO - keep
    if from_state:
        buf[0:pad, :] = jnp.zeros((pad, buf.shape[1]), F32)
        buf[pad:CONV_HALO, :] = st_in_ref[0]
    else:
        prev = pa_ref[0] * jax.nn.sigmoid(pg_ref[0])
        buf[0:CONV_HALO, :] = jnp.where(pl.program_id(1) > 0, prev, 0.0)
    buf[CONV_HALO:CONV_HALO + tq, :] = a_ref[0] * jax.nn.sigmoid(gt_ref[0])
    rs = min(tq, CONV_ROWS)
    for r0 in range(0, tq, rs):
        z = jnp.zeros((rs, buf.shape[1]), F32) + db_ref[...]
        base = buf[r0:r0 + rs + CONV_HALO, :]
        for s in range(8):
            rolled = pltpu.roll(base, rs + CONV_HALO - (pad + s), axis=0)
            for j, w in enumerate(range(s, CONV_W, 8)):
                z = z + rolled[8 * j:8 * j + rs, :] * dw_ref[w:w + 1, :]
        mean = jnp.mean(z, axis=-1, keepdims=True)
        zc = z - mean
        var = jnp.mean(zc * zc, axis=-1, keepdims=True)
        zn = zc * lax.rsqrt(var + LN_EPS) * lg_ref[...] + lb_ref[...]
        o_ref[0, r0:r0 + rs, :] = (zn * jax.nn.sigmoid(zn)).astype(o_ref.dtype)
    st_ref[0] = buf[tq + pad:tq + CONV_HALO, :]


def conv_module(p, state, dw_w, dw_b, ln_g, ln_b, *, acol, gcol, tq):
    b, t, _ = p.shape
    cd = dw_w.shape[1]
    keep = CONV_W - 1
    from_state = state is not None
    operands = [p, p]
    in_specs = [pl.BlockSpec((1, tq, cd), lambda bi, i: (bi, i, acol)),
                pl.BlockSpec((1, tq, cd), lambda bi, i: (bi, i, gcol))]
    if from_state:
        operands.append(state)
        in_specs.append(pl.BlockSpec((1, keep, cd), lambda bi, i: (bi, 0, 0)))
    else:
        r = tq // CONV_HALO
        for col in (acol, gcol):
            operands.append(p)
            in_specs.append(pl.BlockSpec((1, CONV_HALO, cd),
                                         lambda bi, i, col=col: (bi, jnp.maximum(i * r - 1, 0), col)))
    operands += [dw_w, dw_b.reshape(1, cd), ln_g.reshape(1, cd), ln_b.reshape(1, cd)]
    in_specs += [pl.BlockSpec((CONV_W, cd), lambda bi, i: (0, 0))] + [pl.BlockSpec((1, cd), lambda bi, i: (0, 0))] * 3
    return pl.pallas_call(
        functools.partial(_conv_kernel, tq=tq, from_state=from_state),
        grid=(b, t // tq),
        in_specs=in_specs,
        out_specs=[pl.BlockSpec((1, tq, cd), lambda bi, i: (bi, i, 0)),
                   pl.BlockSpec((1, keep, cd), lambda bi, i: (bi, 0, 0))],
        out_shape=[jax.ShapeDtypeStruct((b, t, cd), BF16), jax.ShapeDtypeStruct((b, keep, cd), F32)],
        scratch_shapes=[pltpu.VMEM((CONV_HALO + tq, cd), F32)],
        compiler_params=_cparams(("parallel", "arbitrary")),
        name="conv_module",
    )(*operands)


PV_MU_R, PV_MU_K, PV_MU_V, PV_W0, PV_A0, PV_KK, PV_KA, PV_V0, PV_LNW, PV_LNB, PV_RK = range(11)
PV_ROWS = 16
RWKV_CHUNKS_PER_STEP = 4
RWKV_UNITS_PER_STEP = 32


def _shift_rows(x, first_row):
    rolled = pltpu.roll(x, 1, axis=0)
    row = lax.broadcasted_iota(jnp.int32, x.shape, 0)
    return jnp.where(row == 0, first_row, rolled)


def _rwkv_kernel(*refs, n_heads, has_vres, bb, n_cc):
    (xr_ref, xk_ref, xv_ref, xl_ref, shr_ref, shk_ref, shv_ref, shl_ref, s0_ref, pv_ref, mul_ref,
     w2_ref, a2_ref, g2_ref) = refs[:14]
    pos = 14
    if has_vres:
        v1_ref, v2_ref, vf_ref = refs[pos:pos + 3]
        pos += 3
    ob_ref, sout_ref = refs[pos:pos + 2]
    pos += 2
    if not has_vres:
        vf_out_ref = refs[pos]
        pos += 1
    s_s, lr_s, lk_s, lv_s, ll_s = refs[pos:pos + 5]
    L = CHUNK
    c = pl.program_id(1)

    n_pairs = n_heads // 2
    PW = 2 * HEAD_DIM

    @pl.when(c == 0)
    def _():
        zero = jnp.zeros((HEAD_DIM, HEAD_DIM), F32)
        for bi in range(bb):
            for p in range(n_pairs):
                top = jnp.concatenate([s0_ref[bi, 2 * p], zero], axis=1)
                bottom = jnp.concatenate([zero, s0_ref[bi, 2 * p + 1]], axis=1)
                s_s[bi * n_pairs + p] = jnp.concatenate([top, bottom], axis=0)
            lr_s[bi, 0:1, :] = shr_ref[bi]
            lk_s[bi, 0:1, :] = shk_ref[bi]
            lv_s[bi, 0:1, :] = shv_ref[bi]
            ll_s[bi, 0:1, :] = shl_ref[bi]

    tl = n_cc * L

    def token_shift(x_ref, last_s, mu, bi):
        x = x_ref[bi]
        prev = _shift_rows(x, last_s[bi, 0:1, :])
        last_s[bi, 0:1, :] = x[tl - 1:tl, :]
        return x + (prev - x) * mu

    pv = lambda i: pv_ref[i:i + 1, :]
    row = lax.broadcasted_iota(jnp.int32, (L, L), 0)
    col = lax.broadcasted_iota(jnp.int32, (L, L), 1)
    tri_incl = (row >= col).astype(BF16)

    full = {}
    for bi in range(bb):
        r = token_shift(xr_ref, lr_s, pv(PV_MU_R), bi)
        k = token_shift(xk_ref, lk_s, pv(PV_MU_K), bi)
        v = token_shift(xv_ref, lv_s, pv(PV_MU_V), bi)
        lo = token_shift(xl_ref, ll_s, mul_ref[...], bi)

        zw = pv(PV_W0) + _dot(jnp.tanh(lo).astype(BF16), w2_ref[...])
        w = -(jnp.maximum(-zw, 0.0) + jnp.log(1.0 + jnp.exp(-jnp.abs(zw)))) - 0.5
        d = -jnp.exp(w)
        iclr = jax.nn.sigmoid(pv(PV_A0) + _dot(lo.astype(BF16), a2_ref[...]))
        g = _dot(jax.nn.sigmoid(lo).astype(BF16), g2_ref[...])
        if has_vres:
            mix = jax.nn.sigmoid(pv(PV_V0) + _dot(_dot(v.astype(BF16), v1_ref[...]).astype(BF16), v2_ref[...]))
            v = v + (vf_ref[bi] - v) * mix
        else:
            vf_out_ref[bi] = v
        kks = k * pv(PV_KK)
        kh_all = k * (1.0 + (iclr - 1.0) * pv(PV_KA))
        for cc in range(n_cc):
            rows = slice(cc * L, (cc + 1) * L)
            dc = d[rows, :]
            cs = _cumsum_rows(tri_incl, dc)
            cprev = cs - dc
            c_last = cs[L - 1:L, :]
            c_mid = cs[L // 2:L // 2 + 1, :]
            full[bi, cc] = dict(
                r=r[rows, :], v=v[rows, :], g=g[rows, :], iclr=iclr[rows, :], kks=kks[rows, :], kh=kh_all[rows, :],
                e_mid=jnp.exp(c_mid),
                e_mid_prev=jnp.exp(cprev - c_mid), e_mid_incl=jnp.exp(cs - c_mid), e_from_mid=jnp.exp(c_mid - cs),
                e_to_end=jnp.exp(c_last - cs),
                e_chunk=jnp.exp(c_last))

    lane_lo = lax.broadcasted_iota(jnp.int32, (1, PW), 1) < HEAD_DIM
    r2 = lax.broadcasted_iota(jnp.int32, (2 * L, 2 * L), 0)
    c2 = lax.broadcasted_iota(jnp.int32, (2 * L, 2 * L), 1)
    same_head = (r2 >= L) == (c2 >= L)
    strict_bd = same_head & (jnp.bitwise_and(r2, L - 1) > jnp.bitwise_and(c2, L - 1))
    eye_bd = (r2 == c2).astype(F32)
    incl_c = (lax.broadcasted_iota(jnp.int32, (L, 2 * L), 0)
              >= jnp.bitwise_and(lax.broadcasted_iota(jnp.int32, (L, 2 * L), 1), L - 1))

    def bd(x):
        return jnp.concatenate([jnp.where(lane_lo, x, 0.0), jnp.where(lane_lo, 0.0, x)], axis=0)

    def tile2(x):
        return jnp.concatenate([x, x], axis=0)

    def head_sum(x):
        lo_sum = jnp.sum(jnp.where(lane_lo, x, 0.0), axis=-1, keepdims=True)
        hi_sum = jnp.sum(jnp.where(lane_lo, 0.0, x), axis=-1, keepdims=True)
        return jnp.where(lane_lo, lo_sum, hi_sum)

    bf = lambda z: z.astype(BF16)
    dot_nt = lambda p_, q_: lax.dot_general(p_, q_, (((1,), (1,)), ((), ())), preferred_element_type=F32)
    dot_tn = lambda p_, q_: lax.dot_general(p_, q_, (((0,), (0,)), ((), ())), preferred_element_type=F32)

    units = [(bi, cc, p) for cc in range(n_cc) for bi in range(bb) for p in range(n_pairs)]
    sls = [slice(p * PW, (p + 1) * PW) for _, _, p in units]
    get = lambda name: [full[bi, cc][name][:, sl] for (bi, cc, _), sl in zip(units, sls)]
    mul = lambda xs_, ys_: [x_ * y_ for x_, y_ in zip(xs_, ys_)]

    kk = get("kks")
    kk = [z / jnp.maximum(jnp.sqrt(head_sum(z * z)), 1e-12) for z in kk]
    a_p = [-z for z in kk]
    b_p = mul(kk, get("iclr"))
    k_p, r_p, v_p = get("kh"), get("r"), get("v")
    at = mul(a_p, get("e_mid_prev"))
    rt = mul(r_p, get("e_mid_incl"))
    e_from_mid, e_to_end = get("e_from_mid"), get("e_to_end")
    bt, kt = mul(b_p, e_from_mid), mul(k_p, e_from_mid)
    bh, kh = mul(b_p, e_to_end), mul(k_p, e_to_end)

    cat0 = lambda x_, y_: jnp.concatenate([x_, y_], axis=0)
    cat1 = lambda x_, y_: jnp.concatenate([x_, y_], axis=1)
    at_bd = [bf(bd(z)) for z in at]
    g_a = [dot_nt(x_, bf(cat0(tile2(b_), tile2(k_)))) for x_, b_, k_ in zip(at_bd, bt, kt)]
    n_ab = [jnp.where(strict_bd, z[:, 0:PW], 0.0) for z in g_a]
    n_ak = [jnp.where(strict_bd, z[:, PW:2 * PW], 0.0) for z in g_a]
    rt_b = [bf(z) for z in rt]
    n_r = [jnp.where(cat1(incl_c, incl_c), dot_nt(x_, bf(cat0(bd(b_), bd(k_)))), 0.0)
           for x_, b_, k_ in zip(rt_b, bt, kt)]
    y_inv = [eye_bd + z for z in n_ab]
    pw = [_dot(bf(z), bf(z)) for z in n_ab]
    for _ in range(int(math.log2(L)) - 2):
        lvl = [_dot(bf(p_), bf(cat1(y_, p_))) for p_, y_ in zip(pw, y_inv)]
        y_inv = [y_ + z[:, 0:PW] for y_, z in zip(y_inv, lvl)]
        pw = [z[:, PW:2 * PW] for z in lvl]
    y_inv = [y_ + _dot(bf(p_), bf(y_)) for p_, y_ in zip(pw, y_inv)]
    v_bd = [bf(bd(z)) for z in v_p]
    t2 = [_dot(bf(x_), y_) for x_, y_ in zip(n_ak, v_bd)]
    w12 = [_dot(bf(x_), cat1(a_, bf(t_))) for x_, a_, t_ in zip(y_inv, at_bd, t2)]
    e_mid, e_chunk, g_p = get("e_mid"), get("e_chunk"), get("g")
    n_seq = bb * n_pairs
    state = [s_s[i] for i in range(n_seq)]
    y = []
    for cc in range(n_cc):
        ids = range(cc * n_seq, (cc + 1) * n_seq)
        s_b = [bf(state[q] * e_mid[i]) for q, i in enumerate(ids)]
        u = [dot_nt(bf(w12[i][:, 0:PW]), s_) + w12[i][:, PW:2 * PW] for i, s_ in zip(ids, s_b)]
        y += [dot_nt(rt_b[i], s_) + _dot(bf(n_r[i]), cat0(bf(u_), v_bd[i])) for i, s_, u_ in zip(ids, s_b, u)]
        upd = [dot_tn(bf(cat0(u_[0:L] + u_[L:2 * L], v_p[i])), bf(cat0(bh[i], kh[i]))) for i, u_ in zip(ids, u)]
        state = [state[q] * e_chunk[i] + jnp.where(same_head, d_, 0.0) for q, (i, d_) in enumerate(zip(ids, upd))]
    for q in range(n_seq):
        s_s[q] = state[q]
    for i, ((bi, cc, _), sl) in enumerate(zip(units, sls)):
        mean = head_sum(y[i]) * (1.0 / HEAD_DIM)
        yc = y[i] - mean
        var = head_sum(yc * yc) * (1.0 / HEAD_DIM)
        yn = yc * lax.rsqrt(var + LNX_EPS) * pv(PV_LNW)[:, sl] + pv(PV_LNB)[:, sl]
        bonus = head_sum(r_p[i] * k_p[i] * pv(PV_RK)[:, sl])
        ob_ref[bi, cc * L:(cc + 1) * L, sl] = ((yn + bonus * v_p[i]) * g_p[i]).astype(ob_ref.dtype)

    @pl.when(c == pl.num_programs(1) - 1)
    def _():
        for q in range(n_seq):
            bi, p = q // n_pairs, q % n_pairs
            s_p = s_s[q]
            sout_ref[bi, 2 * p] = s_p[0:HEAD_DIM, 0:HEAD_DIM]
            sout_ref[bi, 2 * p + 1] = s_p[HEAD_DIM:PW, HEAD_DIM:PW]


def rwkv_mixer(p, shift0, s0, layer, pvec, mu_l, w2p, a2p, g2p, vres, *, cols):
    b, t, _ = p.shape
    nc = t // CHUNK
    n_heads = s0.shape[2]
    cb = n_heads * HEAD_DIM
    rc, kc, vc, lc = cols
    has_vres = vres is not None
    n_cc = _batch_block(nc, RWKV_CHUNKS_PER_STEP)
    bb = _batch_block(b, RWKV_UNITS_PER_STEP // (n_cc * (n_heads // 2)))
    tok = lambda col, w: pl.BlockSpec((bb, n_cc * CHUNK, w), lambda bi, c, col=col: (bi, c, col))
    const2 = lambda shape: pl.BlockSpec(shape, lambda bi, c: (0, 0))
    perb = lambda w: pl.BlockSpec((bb, 1, w), lambda bi, c: (bi, 0, 0))
    state_spec = pl.BlockSpec((bb, n_heads, HEAD_DIM, HEAD_DIM), lambda bi, c: (bi, 0, 0, 0))
    state_in_spec = pl.BlockSpec((None, bb, n_heads, HEAD_DIM, HEAD_DIM), lambda bi, c: (layer, bi, 0, 0, 0))
    operands = [p, p, p, p, *shift0, s0, pvec, mu_l, w2p, a2p, g2p]
    in_specs = [tok(rc, cb), tok(kc, cb), tok(vc, cb), tok(lc, LORA_W),
                perb(cb), perb(cb), perb(cb), perb(LORA_W), state_in_spec,
                const2((PV_ROWS, cb)), const2((1, LORA_W)),
                const2((LORA_W, cb)), const2((LORA_W, cb)), const2((LORA_W, cb))]
    out_specs = [tok(0, cb), state_spec]
    out_shape = [jax.ShapeDtypeStruct((b, t, cb), BF16), jax.ShapeDtypeStruct(s0.shape[1:], F32)]
    if has_vres:
        v1p, v2p, v_first = vres
        operands += [v1p, v2p, v_first]
        in_specs += [const2(v1p.shape), const2(v2p.shape), tok(0, cb)]
    else:
        out_specs.append(tok(0, cb))
        out_shape.append(jax.ShapeDtypeStruct((b, t, cb), F32))
    outs = pl.pallas_call(
        functools.partial(_rwkv_kernel, n_heads=n_heads, has_vres=has_vres, bb=bb, n_cc=n_cc),
        grid=(b // bb, nc // n_cc),
        in_specs=in_specs,
        out_specs=out_specs,
        out_shape=out_shape,
        scratch_shapes=[pltpu.VMEM((bb * n_heads // 2, 2 * HEAD_DIM, 2 * HEAD_DIM), F32),
                        pltpu.VMEM((bb, 8, cb), F32), pltpu.VMEM((bb, 8, cb), F32), pltpu.VMEM((bb, 8, cb), F32),
                        pltpu.VMEM((bb, 8, LORA_W), F32)],
        compiler_params=_cparams(("parallel", "arbitrary")),
        name="rwkv_mixer",
    )(*operands)
    return (outs[0], outs[1], None) if has_vres else tuple(outs)


def _band_offsets(n_left):
    return np.arange(-(CHUNK - 1), (n_left + 1) * CHUNK) - n_left * CHUNK


def _lookup_static(table, idx):
    h = table.shape[0]
    pieces = []
    i, n = 0, len(idx)
    while i < n:
        j = i + 1
        step = int(idx[j] - idx[i]) if j < n else 0
        if step in (-1, 0, 1):
            while j < n and idx[j] - idx[j - 1] == step:
                j += 1
        first, last = int(idx[i]), int(idx[j - 1])
        if first == last:
            pieces.append(jnp.broadcast_to(table[:, first:first + 1], (h, j - i)))
        elif first < last:
            pieces.append(table[:, first:last + 1])
        else:
            pieces.append(jnp.flip(table[:, last:first + 1], axis=1))
        i = j
    return jnp.concatenate(pieces, axis=1)


def _toeplitz(e, n_rows, n_cols):
    h, n = e.shape
    z = jnp.concatenate([e, jnp.zeros((h, 1), e.dtype)], axis=1)
    shifted = jnp.tile(z, (1, n_rows))[:, :n_rows * n].reshape(h, n_rows, n)
    return shifted[:, :, n_rows - 1:n_rows - 1 + n_cols]


def _t5_bucket(rel):
    nb = T5_BUCKETS // 2
    exact = nb // 2
    n = np.abs(rel)
    nf = np.maximum(n, exact).astype(np.float32)
    large = exact + (np.log(nf / exact) / math.log(T5_MAX_DIST / exact) * (nb - exact)).astype(np.int32)
    return np.where(rel > 0, nb, 0) + np.where(n < exact, n, np.minimum(large, nb - 1))


def _band_bias_t(e, n_left):
    return jnp.swapaxes(_toeplitz(e, CHUNK, (n_left + 1) * CHUNK), 1, 2)


def _bias_a(t5_table):
    return _band_bias_t(_lookup_static(t5_table.T.astype(F32), _t5_bucket(_band_offsets(A_LEFT))), A_LEFT)


def _bias_c(rel_table):
    idx = np.clip(-_band_offsets(C_LEFT), -REL_CLIP, REL_CLIP) + REL_CLIP
    return _band_bias_t(_lookup_static(rel_table.astype(F32), idx), C_LEFT)


def _pad_rows(w, start, total):
    return jnp.zeros((total, w.shape[1]), w.dtype).at[start:start + w.shape[0]].set(w)


def kernel(x_prompt, x_sample, cache_a_k, cache_a_v, state_b_wkv, state_b_shift, cache_c_k, cache_c_v, state_d_conv, norm_mix_g, norm_ffn_g, norm_final_g, t5_table, w_in_e, w_out_e, a_sink, b_mu, b_w0, b_w2, b_a0, b_a2, b_g2, b_kk, b_ka, b_rk, b_lnx_w, b_lnx_b, b_v0, b_v1, b_v2, w_in_o, w_out_o, c_rel_table, d_dw_w, d_dw_b, d_ln_g, d_ln_b, ffn_w_gate, ffn_w_up, ffn_w_down):
    depth, d_model = norm_mix_g.shape
    h_a = a_sink.shape[1]
    g_a = h_a // KVH_A
    qa = h_a * HEAD_DIM
    kva = KVH_A * HEAD_DIM
    h_b = state_b_wkv.shape[2]
    cb = h_b * HEAD_DIM
    h_c = c_rel_table.shape[1]
    qc = h_c * HEAD_DIM
    cd = d_dw_w.shape[2]
    d_ff = ffn_w_gate.shape[2]
    tf = d_ff

    groups = [x_prompt, x_sample]
    dims = [x.shape[:2] for x in groups]
    xs = [x.reshape(-1, d_model) for x in groups]
    tms = [min(TOKEN_ROWS, x.shape[0]) for x in xs]
    wg, wu, wd = ffn_w_gate.astype(BF16), ffn_w_up.astype(BF16), ffn_w_down.astype(BF16)
    conv_tq = [min(CONV_TQ, t) for _, t in dims]

    o_q, o_k, o_v, o_pb = 0, qa, qa + kva, qa + 2 * kva
    o_r, o_wd, o_kb, o_vb = o_pb, o_pb + cb, o_pb + cb + DECAY_LORA, o_pb + 2 * cb + DECAY_LORA
    o_ad = o_vb + cb
    o_gd = o_ad + ICLR_LORA
    perm_ranges = [(o_q, qa), (o_r, cb), (o_kb, cb), (o_vb, cb), (o_k, kva), (o_v, kva),
                   (o_wd, DECAY_LORA), (o_ad, ICLR_LORA), (o_gd, GATE_LORA)]
    n_r, n_kb, n_vb = qa, qa + cb, qa + 2 * cb
    n_ka = qa + 3 * cb
    n_va = n_ka + kva
    n_lo = n_va + kva
    pb_pieces = [[(o_r, cb)], [(o_kb, cb)], [(o_vb, cb)], [(o_wd, DECAY_LORA), (o_ad, ICLR_LORA), (o_gd, GATE_LORA)]]

    def split_pb(z):
        return [jnp.concatenate([z[..., a - o_pb:a - o_pb + n] for a, n in piece], axis=-1) for piece in pb_pieces]

    def join_pb(r_, k_, v_, lo_):
        return jnp.concatenate([r_, lo_[..., :DECAY_LORA], k_, v_, lo_[..., DECAY_LORA:]], axis=-1)

    bias_a = _bias_a(t5_table) * LOG2E
    st = [[[] for _ in range(7)] for _ in range(2)]
    v_first = [None, None]
    for i in range(depth):
        j = i // 2
        last = i == depth - 1
        if i % 2 == 0:
            w_in = jnp.concatenate([w_in_e[j][:, a:a + n] for a, n in perm_ranges], axis=1).astype(BF16)
            wo = w_out_e[j].astype(BF16)
            mu_r, mu_k, mu_v, mu_lo = split_pb(b_mu[j])
            rows = {PV_MU_R: mu_r, PV_MU_K: mu_k, PV_MU_V: mu_v, PV_W0: b_w0[j],
                    PV_A0: b_a0[j], PV_KK: b_kk[j], PV_KA: b_ka[j], PV_LNW: b_lnx_w[j], PV_LNB: b_lnx_b[j],
                    PV_RK: b_rk[j].reshape(cb)}
            if j > 0:
                rows[PV_V0] = b_v0[j - 1]
            pvec = jnp.stack([rows.get(ri, jnp.zeros((cb,), F32)) for ri in range(PV_ROWS)])
            mu_l = mu_lo.reshape(1, LORA_W)
            w2p = _pad_rows(b_w2[j], 0, LORA_W).astype(BF16)
            a2p = _pad_rows(b_a2[j], DECAY_LORA, LORA_W).astype(BF16)
            g2p = _pad_rows(b_g2[j], DECAY_LORA + ICLR_LORA, LORA_W).astype(BF16)
            if j > 0:
                lora_v = b_v1.shape[2]
                v1p = jnp.zeros((cb, 128), F32).at[:, :lora_v].set(b_v1[j - 1]).astype(BF16)
                v2p = _pad_rows(b_v2[j - 1], 0, 128).astype(BF16)
            bias = bias_a
        else:
            w_in = w_in_o[j].astype(BF16)
            wo = w_out_o[j].astype(BF16)
            bias = _bias_c(c_rel_table[j]) * LOG2E
        for gi in range(2):
            b, t = dims[gi]
            p = proj_in(xs[gi], norm_mix_g[i], w_in, tms[gi]).reshape(b, t, -1)
            if i % 2 == 0:
                attn_args = dict(n_heads=h_a, group=g_a, kcol=n_ka // kva, vcol=n_va // kva)
                if gi == 0:
                    shift0 = [jnp.zeros((b, 1, w_), F32) for w_ in (cb, cb, cb, LORA_W)]
                    s0, s0_layer = jnp.zeros((1, b, h_b, HEAD_DIM, HEAD_DIM), F32), 0
                    o1 = attention_prompt(p, bias, a_sink[j] * LOG2E, n_left=A_LEFT, **attn_args)
                else:
                    shift0 = [z[:, None, :] for z in split_pb(state_b_shift[j])]
                    s0, s0_layer = state_b_wkv, j
                    o1 = attention_sample(p, cache_a_k, cache_a_v, j, jnp.swapaxes(bias, 1, 2), a_sink[j] * LOG2E,
                                          **attn_args)
                vres = None if j == 0 else (v1p, v2p, v_first[gi])
                o2, wkv, vf = rwkv_mixer(p, shift0, s0, s0_layer, pvec, mu_l, w2p, a2p, g2p, vres,
                                         cols=(n_r // cb, n_kb // cb, n_vb // cb, n_lo // LORA_W))
                if j == 0:
                    v_first[gi] = vf
                keep = min(A_LEFT * CHUNK, t)
                nk = p[:, t - keep:, n_ka:n_ka + kva].reshape(b, keep, KVH_A, HEAD_DIM)
                nv = p[:, t - keep:, n_va:n_va + kva].reshape(b, keep, KVH_A, HEAD_DIM)
                nshift = join_pb(*[p[:, t - 1, c0:c0 + w_] for c0, w_ in ((n_r, cb), (n_kb, cb), (n_vb, cb), (n_lo, LORA_W))])
                for s_list, val in zip(st[gi][0:4], (nk, nv, wkv, nshift)):
                    s_list.append(val)
            else:
                attn_args = dict(n_heads=h_c, group=1, kcol=1, vcol=2)
                if gi == 0:
                    conv_prev = None
                    o1 = attention_prompt(p, bias, None, n_left=C_LEFT, **attn_args)
                else:
                    conv_prev = state_d_conv[j]
                    o1 = attention_sample(p, cache_c_k, cache_c_v, j, jnp.swapaxes(bias, 1, 2), None, **attn_args)
                o2, nconv = conv_module(p, conv_prev, d_dw_w[j], d_dw_b[j], d_ln_g[j], d_ln_b[j],
                                        acol=3 * qc // cd, gcol=3 * qc // cd + 1, tq=conv_tq[gi])
                keep = min(C_LEFT * CHUNK, t)
                nk = p[:, t - keep:, qc:2 * qc].reshape(b, keep, h_c, HEAD_DIM)
                nv = p[:, t - keep:, 2 * qc:3 * qc].reshape(b, keep, h_c, HEAD_DIM)
                for s_list, val in zip(st[gi][4:7], (nk, nv, nconv)):
                    s_list.append(val)
            half = o1.shape[-1]
            xs[gi] = post(xs[gi], o1.reshape(-1, half), o2.reshape(-1, half), wo, norm_ffn_g[i], wg, wu, wd,
                          norm_final_g, i, tm=tms[gi], tf=tf, final_norm=last)
    y_prompt = xs[0].reshape(x_prompt.shape)
    y_sample = xs[1].reshape(x_sample.shape)
    (pak, pav, pbw, pbs, pck, pcv, pdc), (sak, sav, sbw, sbs, sck, scv, sdc) = [[jnp.stack(s) for s in g] for g in st]
    return (y_prompt, y_sample, pak, pav, pbw, pbs, pck, pcv, pdc, sak, sav, sbw, sbs, sck, scv, sdc)
```

```python
import functools
import math

import jax
import jax.numpy as jnp
import numpy as np
from jax import lax
from jax.experimental import pallas as pl
from jax.experimental.pallas import tpu as pltpu

F32 = jnp.float32
BF16 = jnp.bfloat16

CHUNK = 64
HEAD_DIM = 64
RMS_EPS = 1e-6
NEG_INF = -1e30
LNX_EPS = 64e-5
LN_EPS = 1e-5
CONV_W = 31
T5_BUCKETS = 32
T5_MAX_DIST = 128
REL_CLIP = 128
A_LEFT = 2
C_LEFT = 8
KVH_A = 2
DECAY_LORA = 64
ICLR_LORA = 64
GATE_LORA = 128
LORA_W = DECAY_LORA + ICLR_LORA + GATE_LORA
CONV_HALO = 32
CONV_ROWS = 64
LOG2E = math.log2(math.e)
QK_SCALE = HEAD_DIM ** -0.5 * LOG2E
TOKEN_ROWS = 1024
CONV_TQ = 512
POST_ROWS = 256
PAIR_W = 2 * HEAD_DIM
ATTN_TQ = 256
ATTN_SAMPLE_BB = 4
ATTN_PROMPT_HEAD_GROUP = {False: 4, True: 8}

VMEM_LIMIT = 56 * 1024 * 1024


def _cparams(sem):
    return pltpu.CompilerParams(dimension_semantics=sem, vmem_limit_bytes=VMEM_LIMIT)


def _batch_block(b, limit):
    return max(d for d in range(1, limit + 1) if b % d == 0)


def _rms(x, g):
    return x * lax.rsqrt(jnp.mean(x * x, axis=-1, keepdims=True) + RMS_EPS) * g


def _dot(a, b):
    return jnp.dot(a, b, preferred_element_type=F32)


def _cumsum_rows(tri, x):
    hi = x.astype(BF16)
    r1 = x - hi.astype(F32)
    mid = r1.astype(BF16)
    lo = (r1 - mid.astype(F32)).astype(BF16)
    return _dot(tri, hi) + _dot(tri, mid) + _dot(tri, lo)


def _proj_in_kernel(x_ref, g_ref, w_ref, o_ref):
    h = _rms(x_ref[...], g_ref[...]).astype(BF16)
    o_ref[...] = _dot(h, w_ref[...])


def proj_in(x2d, g, w_bf16, tm):
    m, d = x2d.shape
    n = w_bf16.shape[1]
    return pl.pallas_call(
        _proj_in_kernel,
        grid=(m // tm,),
        in_specs=[pl.BlockSpec((tm, d), lambda i: (i, 0)),
                  pl.BlockSpec((1, d), lambda i: (0, 0)),
                  pl.BlockSpec((d, n), lambda i: (0, 0))],
        out_specs=pl.BlockSpec((tm, n), lambda i: (i, 0)),
        out_shape=jax.ShapeDtypeStruct((m, n), F32),
        compiler_params=_cparams(("parallel",)),
        name="proj_in",
    )(x2d, g.reshape(1, d), w_bf16)


def _post_kernel(x_ref, o1_ref, o2_ref, wo_ref, g_ref, wg_ref, wu_ref, wd_ref, gf_ref, out_ref, h_s, *, final_norm):
    f = pl.program_id(1)
    half = o1_ref.shape[-1]
    tm = h_s.shape[0]
    rs = min(tm, POST_ROWS)

    @pl.when(f == 0)
    def _():
        for r0 in range(0, tm, rs):
            rows = slice(r0, r0 + rs)
            xn = (x_ref[rows, :] + _dot(o1_ref[rows, :], wo_ref[0:half, :])
                  + _dot(o2_ref[rows, :], wo_ref[half:2 * half, :]))
            out_ref[rows, :] = xn
            h_s[rows, :] = _rms(xn, g_ref[...]).astype(BF16)

    for r0 in range(0, tm, rs):
        rows = slice(r0, r0 + rs)
        h = h_s[rows, :]
        gate = _dot(h, wg_ref[...])
        up = _dot(h, wu_ref[...])
        act = gate * jax.nn.sigmoid(gate) * up
        out_ref[rows, :] += _dot(act.astype(BF16), wd_ref[...])

    if final_norm:
        @pl.when(f == pl.num_programs(1) - 1)
        def _():
            for r0 in range(0, tm, rs):
                rows = slice(r0, r0 + rs)
                out_ref[rows, :] = _rms(out_ref[rows, :], gf_ref[...])


def post(x2d, o1, o2, wo, g, wg, wu, wd, gf, layer, *, tm, tf, final_norm):
    m, d = x2d.shape
    half = o1.shape[-1]
    dff = wg.shape[2]
    once = dict(pipeline_mode=pl.Buffered(1))
    w_mode = once if tf == dff else {}
    return pl.pallas_call(
        functools.partial(_post_kernel, final_norm=final_norm),
        grid=(m // tm, dff // tf),
        in_specs=[pl.BlockSpec((tm, d), lambda i, f: (i, 0)),
                  pl.BlockSpec((tm, half), lambda i, f: (i, 0)),
                  pl.BlockSpec((tm, half), lambda i, f: (i, 0)),
                  pl.BlockSpec((2 * half, d), lambda i, f: (0, 0), **once),
                  pl.BlockSpec((1, d), lambda i, f: (0, 0)),
                  pl.BlockSpec((None, d, tf), lambda i, f: (layer, 0, f), **w_mode),
                  pl.BlockSpec((None, d, tf), lambda i, f: (layer, 0, f), **w_mode),
                  pl.BlockSpec((None, tf, d), lambda i, f: (layer, f, 0), **w_mode),
                  pl.BlockSpec((1, d), lambda i, f: (0, 0))],
        out_specs=pl.BlockSpec((tm, d), lambda i, f: (i, 0)),
        out_shape=jax.ShapeDtypeStruct((m, d), F32),
        scratch_shapes=[pltpu.VMEM((tm, d), BF16)],
        compiler_params=_cparams(("parallel", "arbitrary")),
        name="post",
    )(x2d, o1, o2, wo, g.reshape(1, d), wg, wu, wd, gf.reshape(1, d))


def _attn_prompt_kernel(*refs, n_left, tq, n_heads, group, has_sink):
    q_ref, k_ref, v_ref, bias_ref = refs[:4]
    sink_ref = refs[4] if has_sink else None
    o_ref = refs[5] if has_sink else refs[4]
    scratch = refs[6:] if has_sink else refs[5:]
    hist = n_left * CHUNK
    nk = hist + tq
    i = pl.program_id(1)
    nt = (((1,), (1,)), ((), ()))
    slot = lax.rem(i, 2)
    k_win, vt_win = scratch[0].at[slot], scratch[1].at[slot]
    k_next, vt_next = scratch[0].at[1 - slot], scratch[1].at[1 - slot]
    k_swap = scratch[2].at[slot] if group > 1 else None
    k_swap_next = scratch[2].at[1 - slot] if group > 1 else None

    @pl.when(i == 0)
    def _():
        k_win[0:hist, :] = jnp.zeros((hist, k_win.shape[1]), BF16)
        vt_win[:, 0:hist] = jnp.zeros((vt_win.shape[0], hist), BF16)
        if group > 1:
            k_swap[0:hist, :] = jnp.zeros((hist, k_swap.shape[1]), BF16)

    k_new = k_ref[0]
    k_win[hist:nk, :] = k_new.astype(BF16)
    if group > 1:
        k_swap[hist:nk, :] = pltpu.roll(k_new, HEAD_DIM, axis=1).astype(BF16)
    vt_win[:, hist:nk] = v_ref[0].T.astype(BF16)
    q = q_ref[0] * QK_SCALE
    lane_lo = lax.broadcasted_iota(jnp.int32, (1, PAIR_W), 1) < HEAD_DIM
    lane_hi = jnp.logical_not(lane_lo)

    head_group = ATTN_PROMPT_HEAD_GROUP[group > 1]
    for h0 in range(0, n_heads, head_group):
        hs = list(range(h0, h0 + head_group))
        qh = [jnp.where(lane_lo if h % 2 == 0 else lane_hi, q[:, (h // 2) * PAIR_W:(h // 2 + 1) * PAIR_W], 0.0)
              .astype(BF16) for h in hs]
        if group > 1:
            kv = [h // group for h in hs]
            kt = [(k_win if kv_ == h % 2 else k_swap)[...] for kv_, h in zip(kv, hs)]
        else:
            kv = hs
            kt = [k_win[:, (h // 2) * PAIR_W:(h // 2 + 1) * PAIR_W] for h in hs]
        vt = [vt_win[kv_ * HEAD_DIM:(kv_ + 1) * HEAD_DIM, :] for kv_ in kv]
        s = [lax.dot_general(k_, q_, nt, preferred_element_type=F32) + bias_ref[h] for k_, q_, h in zip(kt, qh, hs)]
        m = [jnp.max(z, axis=0, keepdims=True) for z in s]
        if has_sink:
            sk = [sink_ref[h:h + 1, :] for h in hs]
            m = [jnp.maximum(z, k_) for z, k_ in zip(m, sk)]
        e = [jnp.exp2(z - m_) for z, m_ in zip(s, m)]
        den = [jnp.sum(z, axis=0, keepdims=True) for z in e]
        if has_sink:
            den = [z + jnp.exp2(k_ - m_) for z, k_, m_ in zip(den, sk, m)]
        ot = [_dot(v_, z.astype(BF16)) / d_ for v_, z, d_ in zip(vt, e, den)]
        for j in range(0, len(hs), 2):
            pair = hs[j] // 2
            o_ref[0, :, pair * PAIR_W:(pair + 1) * PAIR_W] = (
                jnp.concatenate([ot[j], ot[j + 1]], axis=0).T.astype(o_ref.dtype))

    k_next[0:hist, :] = k_win[tq:nk, :]
    vt_next[:, 0:hist] = vt_win[:, tq:nk]
    if group > 1:
        k_swap_next[0:hist, :] = k_swap[tq:nk, :]


def attention_prompt(p, bias, sink, *, n_left, n_heads, group, kcol, vcol):
    b, t, _ = p.shape
    kvw = (n_heads // group) * HEAD_DIM
    qw = n_heads * HEAD_DIM
    has_sink = sink is not None
    assert group == 1 or kvw == PAIR_W, "grouped-query path assumes two kv heads sharing one lane tile"
    tq = min(ATTN_TQ, t)
    nk = n_left * CHUNK + tq
    bias_full = jnp.concatenate(
        [jnp.pad(bias, ((0, 0), (qc * CHUNK, nk - qc * CHUNK - bias.shape[1]), (0, 0)), constant_values=NEG_INF)
         for qc in range(tq // CHUNK)], axis=2)
    hist = n_left * CHUNK
    n_var = -(-hist // tq) + 1
    row = jnp.arange(nk)[None, :, None]
    bias_var = jnp.stack([jnp.where(row >= hist - v * tq, bias_full, NEG_INF) for v in range(n_var - 1)] + [bias_full])
    sink_ops = [sink.reshape(n_heads, 1)] if has_sink else []
    sink_specs = [pl.BlockSpec((n_heads, 1), lambda bi, i: (0, 0))] if has_sink else []
    return pl.pallas_call(
        functools.partial(_attn_prompt_kernel, n_left=n_left, tq=tq, n_heads=n_heads, group=group, has_sink=has_sink),
        grid=(b, t // tq),
        in_specs=[pl.BlockSpec((1, tq, qw), lambda bi, i: (bi, i, 0)),
                  pl.BlockSpec((1, tq, kvw), lambda bi, i: (bi, i, kcol)),
                  pl.BlockSpec((1, tq, kvw), lambda bi, i: (bi, i, vcol)),
                  pl.BlockSpec((None,) + bias_full.shape, lambda bi, i: (jnp.minimum(i, n_var - 1), 0, 0, 0))]
        + sink_specs,
        out_specs=pl.BlockSpec((1, tq, qw), lambda bi, i: (bi, i, 0)),
        out_shape=jax.ShapeDtypeStruct((b, t, qw), BF16),
        scratch_shapes=([pltpu.VMEM((2, nk, kvw), BF16), pltpu.VMEM((2, kvw, nk), BF16)]
                        + ([pltpu.VMEM((2, nk, kvw), BF16)] if group > 1 else [])),
        compiler_params=_cparams(("parallel", "arbitrary")),
        name="attention_prompt",
    )(p, p, p, bias_var, *sink_ops)


def _attn_sample_kernel(*refs, n_heads, group, has_sink, bb):
    q_ref, kc_ref, vc_ref, kn_ref, vn_ref, bias_ref = refs[:6]
    sink_ref = refs[6] if has_sink else None
    o_ref = refs[-1]
    hist = kc_ref.shape[-1]
    nt = (((1,), (1,)), ((), ()))
    heads = range(n_heads)
    q_sl = [slice(h * HEAD_DIM, (h + 1) * HEAD_DIM) for h in heads]
    kv_sl = [slice((h // group) * HEAD_DIM, (h // group + 1) * HEAD_DIM) for h in heads]
    for bi in range(bb):
        q = (q_ref[bi] * QK_SCALE).astype(BF16)
        kn = kn_ref[bi].astype(BF16)
        vn = vn_ref[bi].astype(BF16)
        qh = [q[:, sl] for sl in q_sl]
        s = [jnp.concatenate([_dot(qh[h], kc_ref[bi, h // group].astype(BF16)),
                              lax.dot_general(qh[h], kn[:, kv_sl[h]], nt, preferred_element_type=F32)], axis=1)
             + bias_ref[h] for h in heads]
        m = [jnp.max(z, axis=-1, keepdims=True) for z in s]
        if has_sink:
            sk = [sink_ref[h:h + 1, :] for h in heads]
            m = [jnp.maximum(z, k_) for z, k_ in zip(m, sk)]
        e = [jnp.exp2(z - m_) for z, m_ in zip(s, m)]
        den = [jnp.sum(z, axis=-1, keepdims=True) for z in e]
        if has_sink:
            den = [z + jnp.exp2(k_ - m_) for z, k_, m_ in zip(den, sk, m)]
        for h in heads:
            eb = e[h].astype(BF16)
            o = (lax.dot_general(eb[:, 0:hist], vc_ref[bi, h // group].astype(BF16), nt, preferred_element_type=F32)
                 + _dot(eb[:, hist:], vn[:, kv_sl[h]]))
            o_ref[bi, :, q_sl[h]] = (o / den[h]).astype(o_ref.dtype)


def attention_sample(p, k_cache, v_cache, layer, bias, sink, *, n_heads, group, kcol, vcol):
    b, t, _ = p.shape
    qw = n_heads * HEAD_DIM
    kv_heads = n_heads // group
    kvw = kv_heads * HEAD_DIM
    hist = k_cache.shape[2]
    has_sink = sink is not None
    bb = _batch_block(b, ATTN_SAMPLE_BB)
    transposed = lambda z: jnp.transpose(z, (0, 1, 3, 4, 2))
    cache_spec = pl.BlockSpec((None, bb, kv_heads, HEAD_DIM, hist), lambda i: (layer, i, 0, 0, 0))
    sink_ops = [sink.reshape(n_heads, 1)] if has_sink else []
    sink_specs = [pl.BlockSpec((n_heads, 1), lambda i: (0, 0))] if has_sink else []
    return pl.pallas_call(
        functools.partial(_attn_sample_kernel, n_heads=n_heads, group=group, has_sink=has_sink, bb=bb),
        grid=(b // bb,),
        in_specs=[pl.BlockSpec((bb, t, qw), lambda i: (i, 0, 0)), cache_spec, cache_spec,
                  pl.BlockSpec((bb, t, kvw), lambda i: (i, 0, kcol)),
                  pl.BlockSpec((bb, t, kvw), lambda i: (i, 0, vcol)),
                  pl.BlockSpec(bias.shape, lambda i: (0, 0, 0))] + sink_specs,
        out_specs=pl.BlockSpec((bb, t, qw), lambda i: (i, 0, 0)),
        out_shape=jax.ShapeDtypeStruct((b, t, qw), BF16),
        compiler_params=_cparams(("parallel",)),
        name="attention_sample",
    )(p, transposed(k_cache), transposed(v_cache), p, p, bias, *sink_ops)


def _conv_kernel(*refs, tq, from_state):
    if from_state:
        a_ref, gt_ref, st_in_ref, dw_ref, db_ref, lg_ref, lb_ref, o_ref, st_ref, buf = refs
    else:
        a_ref, gt_ref, pa_ref, pg_ref, dw_ref, db_ref, lg_ref, lb_ref, o_ref, st_ref, buf = refs
    keep = CONV_W - 1
    pad = CONV_HALO - keep
    if from_state:
        buf[0:pad, :] = jnp.zeros((pad, buf.shape[1]), F32)
        buf[pad:CONV_HALO, :] = st_in_ref[0]
    else:
        prev = pa_ref[0] * jax.nn.sigmoid(pg_ref[0])
        buf[0:CONV_HALO, :] = jnp.where(pl.program_id(1) > 0, prev, 0.0)
    buf[CONV_HALO:CONV_HALO + tq, :] = a_ref[0] * jax.nn.sigmoid(gt_ref[0])
    rs = min(tq, CONV_ROWS)
    for r0 in range(0, tq, rs):
        z = jnp.zeros((rs, buf.shape[1]), F32) + db_ref[...]
        base = buf[r0:r0 + rs + CONV_HALO, :]
        for s in range(8):
            rolled = pltpu.roll(base, rs + CONV_HALO - (pad + s), axis=0)
            for j, w in enumerate(range(s, CONV_W, 8)):
                z = z + rolled[8 * j:8 * j + rs, :] * dw_ref[w:w + 1, :]
        mean = jnp.mean(z, axis=-1, keepdims=True)
        zc = z - mean
        var = jnp.mean(zc * zc, axis=-1, keepdims=True)
        zn = zc * lax.rsqrt(var + LN_EPS) * lg_ref[...] + lb_ref[...]
        o_ref[0, r0:r0 + rs, :] = (zn * jax.nn.sigmoid(zn)).astype(o_ref.dtype)
    st_ref[0] = buf[tq + pad:tq + CONV_HALO, :]


def conv_module(p, state, dw_w, dw_b, ln_g, ln_b, *, acol, gcol, tq):
    b, t, _ = p.shape
    cd = dw_w.shape[1]
    keep = CONV_W - 1
    from_state = state is not None
    operands = [p, p]
    in_specs = [pl.BlockSpec((1, tq, cd), lambda bi, i: (bi, i, acol)),
                pl.BlockSpec((1, tq, cd), lambda bi, i: (bi, i, gcol))]
    if from_state:
        operands.append(state)
        in_specs.append(pl.BlockSpec((1, keep, cd), lambda bi, i: (bi, 0, 0)))
    else:
        r = tq // CONV_HALO
        for col in (acol, gcol):
            operands.append(p)
            in_specs.append(pl.BlockSpec((1, CONV_HALO, cd),
                                         lambda bi, i, col=col: (bi, jnp.maximum(i * r - 1, 0), col)))
    operands += [dw_w, dw_b.reshape(1, cd), ln_g.reshape(1, cd), ln_b.reshape(1, cd)]
    in_specs += [pl.BlockSpec((CONV_W, cd), lambda bi, i: (0, 0))] + [pl.BlockSpec((1, cd), lambda bi, i: (0, 0))] * 3
    return pl.pallas_call(
        functools.partial(_conv_kernel, tq=tq, from_state=from_state),
        grid=(b, t // tq),
        in_specs=in_specs,
        out_specs=[pl.BlockSpec((1, tq, cd), lambda bi, i: (bi, i, 0)),
                   pl.BlockSpec((1, keep, cd), lambda bi, i: (bi, 0, 0))],
        out_shape=[jax.ShapeDtypeStruct((b, t, cd), BF16), jax.ShapeDtypeStruct((b, keep, cd), F32)],
        scratch_shapes=[pltpu.VMEM((CONV_HALO + tq, cd), F32)],
        compiler_params=_cparams(("parallel", "arbitrary")),
        name="conv_module",
    )(*operands)


PV_MU_R, PV_MU_K, PV_MU_V, PV_W0, PV_A0, PV_KK, PV_KA, PV_V0, PV_LNW, PV_LNB, PV_RK = range(11)
PV_ROWS = 16
RWKV_CHUNKS_PER_STEP = 4
RWKV_UNITS_PER_STEP = 32


def _shift_rows(x, first_row):
    rolled = pltpu.roll(x, 1, axis=0)
    row = lax.broadcasted_iota(jnp.int32, x.shape, 0)
    return jnp.where(row == 0, first_row, rolled)


def _rwkv_kernel(*refs, n_heads, has_vres, bb, n_cc):
    (xr_ref, xk_ref, xv_ref, xl_ref, shr_ref, shk_ref, shv_ref, shl_ref, s0_ref, pv_ref, mul_ref,
     w2_ref, a2_ref, g2_ref) = refs[:14]
    pos = 14
    if has_vres:
        v1_ref, v2_ref, vf_ref = refs[pos:pos + 3]
        pos += 3
    ob_ref, sout_ref = refs[pos:pos + 2]
    pos += 2
    if not has_vres:
        vf_out_ref = refs[pos]
        pos += 1
    s_s, lr_s, lk_s, lv_s, ll_s = refs[pos:pos + 5]
    L = CHUNK
    c = pl.program_id(1)

    n_pairs = n_heads // 2
    PW = 2 * HEAD_DIM

    @pl.when(c == 0)
    def _():
        zero = jnp.zeros((HEAD_DIM, HEAD_DIM), F32)
        for bi in range(bb):
            for p in range(n_pairs):
                top = jnp.concatenate([s0_ref[bi, 2 * p], zero], axis=1)
                bottom = jnp.concatenate([zero, s0_ref[bi, 2 * p + 1]], axis=1)
                s_s[bi * n_pairs + p] = jnp.concatenate([top, bottom], axis=0)
            lr_s[bi, 0:1, :] = shr_ref[bi]
            lk_s[bi, 0:1, :] = shk_ref[bi]
            lv_s[bi, 0:1, :] = shv_ref[bi]
            ll_s[bi, 0:1, :] = shl_ref[bi]

    tl = n_cc * L

    def token_shift(x_ref, last_s, mu, bi):
        x = x_ref[bi]
        prev = _shift_rows(x, last_s[bi, 0:1, :])
        last_s[bi, 0:1, :] = x[tl - 1:tl, :]
        return x + (prev - x) * mu

    pv = lambda i: pv_ref[i:i + 1, :]
    row = lax.broadcasted_iota(jnp.int32, (L, L), 0)
    col = lax.broadcasted_iota(jnp.int32, (L, L), 1)
    tri_incl = (row >= col).astype(BF16)

    full = {}
    for bi in range(bb):
        r = token_shift(xr_ref, lr_s, pv(PV_MU_R), bi)
        k = token_shift(xk_ref, lk_s, pv(PV_MU_K), bi)
        v = token_shift(xv_ref, lv_s, pv(PV_MU_V), bi)
        lo = token_shift(xl_ref, ll_s, mul_ref[...], bi)

        zw = pv(PV_W0) + _dot(jnp.tanh(lo).astype(BF16), w2_ref[...])
        w = -(jnp.maximum(-zw, 0.0) + jnp.log(1.0 + jnp.exp(-jnp.abs(zw)))) - 0.5
        d = -jnp.exp(w)
        iclr = jax.nn.sigmoid(pv(PV_A0) + _dot(lo.astype(BF16), a2_ref[...]))
        g = _dot(jax.nn.sigmoid(lo).astype(BF16), g2_ref[...])
        if has_vres:
            mix = jax.nn.sigmoid(pv(PV_V0) + _dot(_dot(v.astype(BF16), v1_ref[...]).astype(BF16), v2_ref[...]))
            v = v + (vf_ref[bi] - v) * mix
        else:
            vf_out_ref[bi] = v
        kks = k * pv(PV_KK)
        kh_all = k * (1.0 + (iclr - 1.0) * pv(PV_KA))
        for cc in range(n_cc):
            rows = slice(cc * L, (cc + 1) * L)
            dc = d[rows, :]
            cs = _cumsum_rows(tri_incl, dc)
            cprev = cs - dc
            c_last = cs[L - 1:L, :]
            c_mid = cs[L // 2:L // 2 + 1, :]
            full[bi, cc] = dict(
                r=r[rows, :], v=v[rows, :], g=g[rows, :], iclr=iclr[rows, :], kks=kks[rows, :], kh=kh_all[rows, :],
                e_mid=jnp.exp(c_mid),
                e_mid_prev=jnp.exp(cprev - c_mid), e_mid_incl=jnp.exp(cs - c_mid), e_from_mid=jnp.exp(c_mid - cs),
                e_to_end=jnp.exp(c_last - cs),
                e_chunk=jnp.exp(c_last))

    lane_lo = lax.broadcasted_iota(jnp.int32, (1, PW), 1) < HEAD_DIM
    r2 = lax.broadcasted_iota(jnp.int32, (2 * L, 2 * L), 0)
    c2 = lax.broadcasted_iota(jnp.int32, (2 * L, 2 * L), 1)
    same_head = (r2 >= L) == (c2 >= L)
    strict_bd = same_head & (jnp.bitwise_and(r2, L - 1) > jnp.bitwise_and(c2, L - 1))
    eye_bd = (r2 == c2).astype(F32)
    incl_c = (lax.broadcasted_iota(jnp.int32, (L, 2 * L), 0)
              >= jnp.bitwise_and(lax.broadcasted_iota(jnp.int32, (L, 2 * L), 1), L - 1))

    def bd(x):
        return jnp.concatenate([jnp.where(lane_lo, x, 0.0), jnp.where(lane_lo, 0.0, x)], axis=0)

    def tile2(x):
        return jnp.concatenate([x, x], axis=0)

    def head_sum(x):
        lo_sum = jnp.sum(jnp.where(lane_lo, x, 0.0), axis=-1, keepdims=True)
        hi_sum = jnp.sum(jnp.where(lane_lo, 0.0, x), axis=-1, keepdims=True)
        return jnp.where(lane_lo, lo_sum, hi_sum)

    bf = lambda z: z.astype(BF16)
    dot_nt = lambda p_, q_: lax.dot_general(p_, q_, (((1,), (1,)), ((), ())), preferred_element_type=F32)
    dot_tn = lambda p_, q_: lax.dot_general(p_, q_, (((0,), (0,)), ((), ())), preferred_element_type=F32)

    units = [(bi, cc, p) for cc in range(n_cc) for bi in range(bb) for p in range(n_pairs)]
    sls = [slice(p * PW, (p + 1) * PW) for _, _, p in units]
    get = lambda name: [full[bi, cc][name][:, sl] for (bi, cc, _), sl in zip(units, sls)]
    mul = lambda xs_, ys_: [x_ * y_ for x_, y_ in zip(xs_, ys_)]

    kk = get("kks")
    kk = [z / jnp.maximum(jnp.sqrt(head_sum(z * z)), 1e-12) for z in kk]
    a_p = [-z for z in kk]
    b_p = mul(kk, get("iclr"))
    k_p, r_p, v_p = get("kh"), get("r"), get("v")
    at = mul(a_p, get("e_mid_prev"))
    rt = mul(r_p, get("e_mid_incl"))
    e_from_mid, e_to_end = get("e_from_mid"), get("e_to_end")
    bt, kt = mul(b_p, e_from_mid), mul(k_p, e_from_mid)
    bh, kh = mul(b_p, e_to_end), mul(k_p, e_to_end)

    cat0 = lambda x_, y_: jnp.concatenate([x_, y_], axis=0)
    cat1 = lambda x_, y_: jnp.concatenate([x_, y_], axis=1)
    at_bd = [bf(bd(z)) for z in at]
    g_a = [dot_nt(x_, bf(cat0(tile2(b_), tile2(k_)))) for x_, b_, k_ in zip(at_bd, bt, kt)]
    n_ab = [jnp.where(strict_bd, z[:, 0:PW], 0.0) for z in g_a]
    n_ak = [jnp.where(strict_bd, z[:, PW:2 * PW], 0.0) for z in g_a]
    rt_b = [bf(z) for z in rt]
    n_r = [jnp.where(cat1(incl_c, incl_c), dot_nt(x_, bf(cat0(bd(b_), bd(k_)))), 0.0)
           for x_, b_, k_ in zip(rt_b, bt, kt)]
    y_inv = [eye_bd + z for z in n_ab]
    pw = [_dot(bf(z), bf(z)) for z in n_ab]
    for _ in range(int(math.log2(L)) - 2):
        lvl = [_dot(bf(p_), bf(cat1(y_, p_))) for p_, y_ in zip(pw, y_inv)]
        y_inv = [y_ + z[:, 0:PW] for y_, z in zip(y_inv, lvl)]
        pw = [z[:, PW:2 * PW] for z in lvl]
    y_inv = [y_ + _dot(bf(p_), bf(y_)) for p_, y_ in zip(pw, y_inv)]
    v_bd = [bf(bd(z)) for z in v_p]
    t2 = [_dot(bf(x_), y_) for x_, y_ in zip(n_ak, v_bd)]
    w12 = [_dot(bf(x_), cat1(a_, bf(t_))) for x_, a_, t_ in zip(y_inv, at_bd, t2)]
    e_mid, e_chunk, g_p = get("e_mid"), get("e_chunk"), get("g")
    n_seq = bb * n_pairs
    state = [s_s[i] for i in range(n_seq)]
    y = []
    for cc in range(n_cc):
        ids = range(cc * n_seq, (cc + 1) * n_seq)
        s_b = [bf(state[q] * e_mid[i]) for q, i in enumerate(ids)]
        u = [dot_nt(bf(w12[i][:, 0:PW]), s_) + w12[i][:, PW:2 * PW] for i, s_ in zip(ids, s_b)]
        y += [dot_nt(rt_b[i], s_) + _dot(bf(n_r[i]), cat0(bf(u_), v_bd[i])) for i, s_, u_ in zip(ids, s_b, u)]
        upd = [dot_tn(bf(cat0(u_[0:L] + u_[L:2 * L], v_p[i])), bf(cat0(bh[i], kh[i]))) for i, u_ in zip(ids, u)]
        state = [state[q] * e_chunk[i] + jnp.where(same_head, d_, 0.0) for q, (i, d_) in enumerate(zip(ids, upd))]
    for q in range(n_seq):
        s_s[q] = state[q]
    for i, ((bi, cc, _), sl) in enumerate(zip(units, sls)):
        mean = head_sum(y[i]) * (1.0 / HEAD_DIM)
        yc = y[i] - mean
        var = head_sum(yc * yc) * (1.0 / HEAD_DIM)
        yn = yc * lax.rsqrt(var + LNX_EPS) * pv(PV_LNW)[:, sl] + pv(PV_LNB)[:, sl]
        bonus = head_sum(r_p[i] * k_p[i] * pv(PV_RK)[:, sl])
        ob_ref[bi, cc * L:(cc + 1) * L, sl] = ((yn + bonus * v_p[i]) * g_p[i]).astype(ob_ref.dtype)

    @pl.when(c == pl.num_programs(1) - 1)
    def _():
        for q in range(n_seq):
            bi, p = q // n_pairs, q % n_pairs
            s_p = s_s[q]
            sout_ref[bi, 2 * p] = s_p[0:HEAD_DIM, 0:HEAD_DIM]
            sout_ref[bi, 2 * p + 1] = s_p[HEAD_DIM:PW, HEAD_DIM:PW]


def rwkv_mixer(p, shift0, s0, layer, pvec, mu_l, w2p, a2p, g2p, vres, *, cols):
    b, t, _ = p.shape
    nc = t // CHUNK
    n_heads = s0.shape[2]
    cb = n_heads * HEAD_DIM
    rc, kc, vc, lc = cols
    has_vres = vres is not None
    n_cc = _batch_block(nc, RWKV_CHUNKS_PER_STEP)
    bb = _batch_block(b, RWKV_UNITS_PER_STEP // (n_cc * (n_heads // 2)))
    tok = lambda col, w: pl.BlockSpec((bb, n_cc * CHUNK, w), lambda bi, c, col=col: (bi, c, col))
    const2 = lambda shape: pl.BlockSpec(shape, lambda bi, c: (0, 0))
    perb = lambda w: pl.BlockSpec((bb, 1, w), lambda bi, c: (bi, 0, 0))
    state_spec = pl.BlockSpec((bb, n_heads, HEAD_DIM, HEAD_DIM), lambda bi, c: (bi, 0, 0, 0))
    state_in_spec = pl.BlockSpec((None, bb, n_heads, HEAD_DIM, HEAD_DIM), lambda bi, c: (layer, bi, 0, 0, 0))
    operands = [p, p, p, p, *shift0, s0, pvec, mu_l, w2p, a2p, g2p]
    in_specs = [tok(rc, cb), tok(kc, cb), tok(vc, cb), tok(lc, LORA_W),
                perb(cb), perb(cb), perb(cb), perb(LORA_W), state_in_spec,
                const2((PV_ROWS, cb)), const2((1, LORA_W)),
                const2((LORA_W, cb)), const2((LORA_W, cb)), const2((LORA_W, cb))]
    out_specs = [tok(0, cb), state_spec]
    out_shape = [jax.ShapeDtypeStruct((b, t, cb), BF16), jax.ShapeDtypeStruct(s0.shape[1:], F32)]
    if has_vres:
        v1p, v2p, v_first = vres
        operands += [v1p, v2p, v_first]
        in_specs += [const2(v1p.shape), const2(v2p.shape), tok(0, cb)]
    else:
        out_specs.append(tok(0, cb))
        out_shape.append(jax.ShapeDtypeStruct((b, t, cb), F32))
    outs = pl.pallas_call(
        functools.partial(_rwkv_kernel, n_heads=n_heads, has_vres=has_vres, bb=bb, n_cc=n_cc),
        grid=(b // bb, nc // n_cc),
        in_specs=in_specs,
        out_specs=out_specs,
        out_shape=out_shape,
        scratch_shapes=[pltpu.VMEM((bb * n_heads // 2, 2 * HEAD_DIM, 2 * HEAD_DIM), F32),
                        pltpu.VMEM((bb, 8, cb), F32), pltpu.VMEM((bb, 8, cb), F32), pltpu.VMEM((bb, 8, cb), F32),
                        pltpu.VMEM((bb, 8, LORA_W), F32)],
        compiler_params=_cparams(("parallel", "arbitrary")),
        name="rwkv_mixer",
    )(*operands)
    return (outs[0], outs[1], None) if has_vres else tuple(outs)


def _band_offsets(n_left):
    return np.arange(-(CHUNK - 1), (n_left + 1) * CHUNK) - n_left * CHUNK


def _lookup_static(table, idx):
    h = table.shape[0]
    pieces = []
    i, n = 0, len(idx)
    while i < n:
        j = i + 1
        step = int(idx[j] - idx[i]) if j < n else 0
        if step in (-1, 0, 1):
            while j < n and idx[j] - idx[j - 1] == step:
                j += 1
        first, last = int(idx[i]), int(idx[j - 1])
        if first == last:
            pieces.append(jnp.broadcast_to(table[:, first:first + 1], (h, j - i)))
        elif first < last:
            pieces.append(table[:, first:last + 1])
        else:
            pieces.append(jnp.flip(table[:, last:first + 1], axis=1))
        i = j
    return jnp.concatenate(pieces, axis=1)


def _toeplitz(e, n_rows, n_cols):
    h, n = e.shape
    z = jnp.concatenate([e, jnp.zeros((h, 1), e.dtype)], axis=1)
    shifted = jnp.tile(z, (1, n_rows))[:, :n_rows * n].reshape(h, n_rows, n)
    return shifted[:, :, n_rows - 1:n_rows - 1 + n_cols]


def _t5_bucket(rel):
    nb = T5_BUCKETS // 2
    exact = nb // 2
    n = np.abs(rel)
    nf = np.maximum(n, exact).astype(np.float32)
    large = exact + (np.log(nf / exact) / math.log(T5_MAX_DIST / exact) * (nb - exact)).astype(np.int32)
    return np.where(rel > 0, nb, 0) + np.where(n < exact, n, np.minimum(large, nb - 1))


def _band_bias_t(e, n_left):
    return jnp.swapaxes(_toeplitz(e, CHUNK, (n_left + 1) * CHUNK), 1, 2)


def _bias_a(t5_table):
    return _band_bias_t(_lookup_static(t5_table.T.astype(F32), _t5_bucket(_band_offsets(A_LEFT))), A_LEFT)


def _bias_c(rel_table):
    idx = np.clip(-_band_offsets(C_LEFT), -REL_CLIP, REL_CLIP) + REL_CLIP
    return _band_bias_t(_lookup_static(rel_table.astype(F32), idx), C_LEFT)


def _pad_rows(w, start, total):
    return jnp.zeros((total, w.shape[1]), w.dtype).at[start:start + w.shape[0]].set(w)


def kernel(x_prompt, x_sample, cache_a_k, cache_a_v, state_b_wkv, state_b_shift, cache_c_k, cache_c_v, state_d_conv, norm_mix_g, norm_ffn_g, norm_final_g, t5_table, w_in_e, w_out_e, a_sink, b_mu, b_w0, b_w2, b_a0, b_a2, b_g2, b_kk, b_ka, b_rk, b_lnx_w, b_lnx_b, b_v0, b_v1, b_v2, w_in_o, w_out_o, c_rel_table, d_dw_w, d_dw_b, d_ln_g, d_ln_b, ffn_w_gate, ffn_w_up, ffn_w_down):
    depth, d_model = norm_mix_g.shape
    h_a = a_sink.shape[1]
    g_a = h_a // KVH_A
    qa = h_a * HEAD_DIM
    kva = KVH_A * HEAD_DIM
    h_b = state_b_wkv.shape[2]
    cb = h_b * HEAD_DIM
    h_c = c_rel_table.shape[1]
    qc = h_c * HEAD_DIM
    cd = d_dw_w.shape[2]
    d_ff = ffn_w_gate.shape[2]
    tf = d_ff

    groups = [x_prompt, x_sample]
    dims = [x.shape[:2] for x in groups]
    xs = [x.reshape(-1, d_model) for x in groups]
    tms = [min(TOKEN_ROWS, x.shape[0]) for x in xs]
    wg, wu, wd = ffn_w_gate.astype(BF16), ffn_w_up.astype(BF16), ffn_w_down.astype(BF16)
    conv_tq = [min(CONV_TQ, t) for _, t in dims]

    o_q, o_k, o_v, o_pb = 0, qa, qa + kva, qa + 2 * kva
    o_r, o_wd, o_kb, o_vb = o_pb, o_pb + cb, o_pb + cb + DECAY_LORA, o_pb + 2 * cb + DECAY_LORA
    o_ad = o_vb + cb
    o_gd = o_ad + ICLR_LORA
    perm_ranges = [(o_q, qa), (o_r, cb), (o_kb, cb), (o_vb, cb), (o_k, kva), (o_v, kva),
                   (o_wd, DECAY_LORA), (o_ad, ICLR_LORA), (o_gd, GATE_LORA)]
    n_r, n_kb, n_vb = qa, qa + cb, qa + 2 * cb
    n_ka = qa + 3 * cb
    n_va = n_ka + kva
    n_lo = n_va + kva
    pb_pieces = [[(o_r, cb)], [(o_kb, cb)], [(o_vb, cb)], [(o_wd, DECAY_LORA), (o_ad, ICLR_LORA), (o_gd, GATE_LORA)]]

    def split_pb(z):
        return [jnp.concatenate([z[..., a - o_pb:a - o_pb + n] for a, n in piece], axis=-1) for piece in pb_pieces]

    def join_pb(r_, k_, v_, lo_):
        return jnp.concatenate([r_, lo_[..., :DECAY_LORA], k_, v_, lo_[..., DECAY_LORA:]], axis=-1)

    bias_a = _bias_a(t5_table) * LOG2E
    st = [[[] for _ in range(7)] for _ in range(2)]
    v_first = [None, None]
    for i in range(depth):
        j = i // 2
        last = i == depth - 1
        if i % 2 == 0:
            w_in = jnp.concatenate([w_in_e[j][:, a:a + n] for a, n in perm_ranges], axis=1).astype(BF16)
            wo = w_out_e[j].astype(BF16)
            mu_r, mu_k, mu_v, mu_lo = split_pb(b_mu[j])
            rows = {PV_MU_R: mu_r, PV_MU_K: mu_k, PV_MU_V: mu_v, PV_W0: b_w0[j],
                    PV_A0: b_a0[j], PV_KK: b_kk[j], PV_KA: b_ka[j], PV_LNW: b_lnx_w[j], PV_LNB: b_lnx_b[j],
                    PV_RK: b_rk[j].reshape(cb)}
            if j > 0:
                rows[PV_V0] = b_v0[j - 1]
            pvec = jnp.stack([rows.get(ri, jnp.zeros((cb,), F32)) for ri in range(PV_ROWS)])
            mu_l = mu_lo.reshape(1, LORA_W)
            w2p = _pad_rows(b_w2[j], 0, LORA_W).astype(BF16)
            a2p = _pad_rows(b_a2[j], DECAY_LORA, LORA_W).astype(BF16)
            g2p = _pad_rows(b_g2[j], DECAY_LORA + ICLR_LORA, LORA_W).astype(BF16)
            if j > 0:
                lora_v = b_v1.shape[2]
                v1p = jnp.zeros((cb, 128), F32).at[:, :lora_v].set(b_v1[j - 1]).astype(BF16)
                v2p = _pad_rows(b_v2[j - 1], 0, 128).astype(BF16)
            bias = bias_a
        else:
            w_in = w_in_o[j].astype(BF16)
            wo = w_out_o[j].astype(BF16)
            bias = _bias_c(c_rel_table[j]) * LOG2E
        for gi in range(2):
            b, t = dims[gi]
            p = proj_in(xs[gi], norm_mix_g[i], w_in, tms[gi]).reshape(b, t, -1)
            if i % 2 == 0:
                attn_args = dict(n_heads=h_a, group=g_a, kcol=n_ka // kva, vcol=n_va // kva)
                if gi == 0:
                    shift0 = [jnp.zeros((b, 1, w_), F32) for w_ in (cb, cb, cb, LORA_W)]
                    s0, s0_layer = jnp.zeros((1, b, h_b, HEAD_DIM, HEAD_DIM), F32), 0
                    o1 = attention_prompt(p, bias, a_sink[j] * LOG2E, n_left=A_LEFT, **attn_args)
                else:
                    shift0 = [z[:, None, :] for z in split_pb(state_b_shift[j])]
                    s0, s0_layer = state_b_wkv, j
                    o1 = attention_sample(p, cache_a_k, cache_a_v, j, jnp.swapaxes(bias, 1, 2), a_sink[j] * LOG2E,
                                          **attn_args)
                vres = None if j == 0 else (v1p, v2p, v_first[gi])
                o2, wkv, vf = rwkv_mixer(p, shift0, s0, s0_layer, pvec, mu_l, w2p, a2p, g2p, vres,
                                         cols=(n_r // cb, n_kb // cb, n_vb // cb, n_lo // LORA_W))
                if j == 0:
                    v_first[gi] = vf
                keep = min(A_LEFT * CHUNK, t)
                nk = p[:, t - keep:, n_ka:n_ka + kva].reshape(b, keep, KVH_A, HEAD_DIM)
                nv = p[:, t - keep:, n_va:n_va + kva].reshape(b, keep, KVH_A, HEAD_DIM)
                nshift = join_pb(*[p[:, t - 1, c0:c0 + w_] for c0, w_ in ((n_r, cb), (n_kb, cb), (n_vb, cb), (n_lo, LORA_W))])
                for s_list, val in zip(st[gi][0:4], (nk, nv, wkv, nshift)):
                    s_list.append(val)
            else:
                attn_args = dict(n_heads=h_c, group=1, kcol=1, vcol=2)
                if gi == 0:
                    conv_prev = None
                    o1 = attention_prompt(p, bias, None, n_left=C_LEFT, **attn_args)
                else:
                    conv_prev = state_d_conv[j]
                    o1 = attention_sample(p, cache_c_k, cache_c_v, j, jnp.swapaxes(bias, 1, 2), None, **attn_args)
                o2, nconv = conv_module(p, conv_prev, d_dw_w[j], d_dw_b[j], d_ln_g[j], d_ln_b[j],
                                        acol=3 * qc // cd, gcol=3 * qc // cd + 1, tq=conv_tq[gi])
                keep = min(C_LEFT * CHUNK, t)
                nk = p[:, t - keep:, qc:2 * qc].reshape(b, keep, h_c, HEAD_DIM)
                nv = p[:, t - keep:, 2 * qc:3 * qc].reshape(b, keep, h_c, HEAD_DIM)
                for s_list, val in zip(st[gi][4:7], (nk, nv, nconv)):
                    s_list.append(val)
            half = o1.shape[-1]
            xs[gi] = post(xs[gi], o1.reshape(-1, half), o2.reshape(-1, half), wo, norm_ffn_g[i], wg, wu, wd,
                          norm_final_g, i, tm=tms[gi], tf=tf, final_norm=last)
    y_prompt = xs[0].reshape(x_prompt.shape)
    y_sample = xs[1].reshape(x_sample.shape)
    (pak, pav, pbw, pbs, pck, pcv, pdc), (sak, sav, sbw, sbs, sck, scv, sdc) = [[jnp.stack(s) for s in g] for g in st]
    return (y_prompt, y_sample, pak, pav, pbw, pbs, pck, pcv, pdc, sak, sav, sbw, sbs, sck, scv, sdc)
```

```python
import functools
import math

import jax
import jax.numpy as jnp
import numpy as np
from jax import lax
from jax.experimental import pallas as pl
from jax.experimental.pallas import tpu as pltpu

F32 = jnp.float32
BF16 = jnp.bfloat16

CHUNK = 64
HEAD_DIM = 64
RMS_EPS = 1e-6
NEG_INF = -1e30
LNX_EPS = 64e-5
LN_EPS = 1e-5
CONV_W = 31
T5_BUCKETS = 32
T5_MAX_DIST = 128
REL_CLIP = 128
A_LEFT = 2
C_LEFT = 8
KVH_A = 2
DECAY_LORA = 64
ICLR_LORA = 64
GATE_LORA = 128
LORA_W = DECAY_LORA + ICLR_LORA + GATE_LORA
CONV_HALO = 32
CONV_ROWS = 64
LOG2E = math.log2(math.e)
QK_SCALE = HEAD_DIM ** -0.5 * LOG2E
TOKEN_ROWS = 1024
CONV_TQ = 512
POST_ROWS = 256
PAIR_W = 2 * HEAD_DIM
ATTN_TQ = 256
ATTN_SAMPLE_BB = 4
ATTN_PROMPT_HEAD_GROUP = {False: 4, True: 8}

VMEM_LIMIT = 56 * 1024 * 1024


def _cparams(sem):
    return pltpu.CompilerParams(dimension_semantics=sem, vmem_limit_bytes=VMEM_LIMIT)


def _batch_block(b, limit):
    return max(d for d in range(1, limit + 1) if b % d == 0)


def _rms(x, g):
    return x * lax.rsqrt(jnp.mean(x * x, axis=-1, keepdims=True) + RMS_EPS) * g


def _dot(a, b):
    return jnp.dot(a, b, preferred_element_type=F32)


def _cumsum_rows(tri, x):
    hi = x.astype(BF16)
    r1 = x - hi.astype(F32)
    mid = r1.astype(BF16)
    lo = (r1 - mid.astype(F32)).astype(BF16)
    return _dot(tri, hi) + _dot(tri, mid) + _dot(tri, lo)


def _proj_in_kernel(x_ref, g_ref, w_ref, o_ref):
    h = _rms(x_ref[...], g_ref[...]).astype(BF16)
    o_ref[...] = _dot(h, w_ref[...])


def proj_in(x2d, g, w_bf16, tm):
    m, d = x2d.shape
    n = w_bf16.shape[1]
    return pl.pallas_call(
        _proj_in_kernel,
        grid=(m // tm,),
        in_specs=[pl.BlockSpec((tm, d), lambda i: (i, 0)),
                  pl.BlockSpec((1, d), lambda i: (0, 0)),
                  pl.BlockSpec((d, n), lambda i: (0, 0))],
        out_specs=pl.BlockSpec((tm, n), lambda i: (i, 0)),
        out_shape=jax.ShapeDtypeStruct((m, n), F32),
        compiler_params=_cparams(("parallel",)),
        name="proj_in",
    )(x2d, g.reshape(1, d), w_bf16)


def _post_kernel(x_ref, o1_ref, o2_ref, wo_ref, g_ref, wg_ref, wu_ref, wd_ref, gf_ref, out_ref, h_s, *, final_norm):
    f = pl.program_id(1)
    half = o1_ref.shape[-1]
    tm = h_s.shape[0]
    rs = min(tm, POST_ROWS)

    @pl.when(f == 0)
    def _():
        for r0 in range(0, tm, rs):
            rows = slice(r0, r0 + rs)
            xn = (x_ref[rows, :] + _dot(o1_ref[rows, :], wo_ref[0:half, :])
                  + _dot(o2_ref[rows, :], wo_ref[half:2 * half, :]))
            out_ref[rows, :] = xn
            h_s[rows, :] = _rms(xn, g_ref[...]).astype(BF16)

    for r0 in range(0, tm, rs):
        rows = slice(r0, r0 + rs)
        h = h_s[rows, :]
        gate = _dot(h, wg_ref[...])
        up = _dot(h, wu_ref[...])
        act = gate * jax.nn.sigmoid(gate) * up
        out_ref[rows, :] += _dot(act.astype(BF16), wd_ref[...])

    if final_norm:
        @pl.when(f == pl.num_programs(1) - 1)
        def _():
            for r0 in range(0, tm, rs):
                rows = slice(r0, r0 + rs)
                out_ref[rows, :] = _rms(out_ref[rows, :], gf_ref[...])


def post(x2d, o1, o2, wo, g, wg, wu, wd, gf, layer, *, tm, tf, final_norm):
    m, d = x2d.shape
    half = o1.shape[-1]
    dff = wg.shape[2]
    once = dict(pipeline_mode=pl.Buffered(1))
    w_mode = once if tf == dff else {}
    return pl.pallas_call(
        functools.partial(_post_kernel, final_norm=final_norm),
        grid=(m // tm, dff // tf),
        in_specs=[pl.BlockSpec((tm, d), lambda i, f: (i, 0)),
                  pl.BlockSpec((tm, half), lambda i, f: (i, 0)),
                  pl.BlockSpec((tm, half), lambda i, f: (i, 0)),
                  pl.BlockSpec((2 * half, d), lambda i, f: (0, 0), **once),
                  pl.BlockSpec((1, d), lambda i, f: (0, 0)),
                  pl.BlockSpec((None, d, tf), lambda i, f: (layer, 0, f), **w_mode),
                  pl.BlockSpec((None, d, tf), lambda i, f: (layer, 0, f), **w_mode),
                  pl.BlockSpec((None, tf, d), lambda i, f: (layer, f, 0), **w_mode),
                  pl.BlockSpec((1, d), lambda i, f: (0, 0))],
        out_specs=pl.BlockSpec((tm, d), lambda i, f: (i, 0)),
        out_shape=jax.ShapeDtypeStruct((m, d), F32),
        scratch_shapes=[pltpu.VMEM((tm, d), BF16)],
        compiler_params=_cparams(("parallel", "arbitrary")),
        name="post",
    )(x2d, o1, o2, wo, g.reshape(1, d), wg, wu, wd, gf.reshape(1, d))


def _attn_prompt_kernel(*refs, n_left, tq, n_heads, group, has_sink):
    q_ref, k_ref, v_ref, bias_ref = refs[:4]
    sink_ref = refs[4] if has_sink else None
    o_ref = refs[5] if has_sink else refs[4]
    scratch = refs[6:] if has_sink else refs[5:]
    hist = n_left * CHUNK
    nk = hist + tq
    i = pl.program_id(1)
    nt = (((1,), (1,)), ((), ()))
    slot = lax.rem(i, 2)
    k_win, vt_win = scratch[0].at[slot], scratch[1].at[slot]
    k_next, vt_next = scratch[0].at[1 - slot], scratch[1].at[1 - slot]
    k_swap = scratch[2].at[slot] if group > 1 else None
    k_swap_next = scratch[2].at[1 - slot] if group > 1 else None

    @pl.when(i == 0)
    def _():
        k_win[0:hist, :] = jnp.zeros((hist, k_win.shape[1]), BF16)
        vt_win[:, 0:hist] = jnp.zeros((vt_win.shape[0], hist), BF16)
        if group > 1:
            k_swap[0:hist, :] = jnp.zeros((hist, k_swap.shape[1]), BF16)

    k_new = k_ref[0]
    k_win[hist:nk, :] = k_new.astype(BF16)
    if group > 1:
        k_swap[hist:nk, :] = pltpu.roll(k_new, HEAD_DIM, axis=1).astype(BF16)
    vt_win[:, hist:nk] = v_ref[0].T.astype(BF16)
    q = q_ref[0] * QK_SCALE
    lane_lo = lax.broadcasted_iota(jnp.int32, (1, PAIR_W), 1) < HEAD_DIM
    lane_hi = jnp.logical_not(lane_lo)

    head_group = ATTN_PROMPT_HEAD_GROUP[group > 1]
    qw = PAIR_W
    halves = [(hq * qw, hq * qw + hist + qw) for hq in range(tq // qw)]
    for h0 in range(0, n_heads, head_group):
        jobs = [(h, hq) for hq in range(len(halves)) for h in range(h0, h0 + head_group)]
        rows = [halves[hq] for _, hq in jobs]
        qh = [jnp.where(lane_lo if h % 2 == 0 else lane_hi,
                        q[hq * qw:(hq + 1) * qw, (h // 2) * PAIR_W:(h // 2 + 1) * PAIR_W], 0.0).astype(BF16)
              for h, hq in jobs]
        if group > 1:
            kv = [h // group for h, _ in jobs]
            kt = [(k_win if kv_ == h % 2 else k_swap)[r0:r1, :] for kv_, (h, _), (r0, r1) in zip(kv, jobs, rows)]
        else:
            kv = [h for h, _ in jobs]
            kt = [k_win[r0:r1, (h // 2) * PAIR_W:(h // 2 + 1) * PAIR_W] for (h, _), (r0, r1) in zip(jobs, rows)]
        vt = [vt_win[kv_ * HEAD_DIM:(kv_ + 1) * HEAD_DIM, r0:r1] for kv_, (r0, r1) in zip(kv, rows)]
        s = [lax.dot_general(k_, q_, nt, preferred_element_type=F32) + bias_ref[h, r0:r1, hq * qw:(hq + 1) * qw]
             for k_, q_, (h, hq), (r0, r1) in zip(kt, qh, jobs, rows)]
        m = [jnp.max(z, axis=0, keepdims=True) for z in s]
        if has_sink:
            sk = [sink_ref[h:h + 1, :] for h, _ in jobs]
            m = [jnp.maximum(z, k_) for z, k_ in zip(m, sk)]
        e = [jnp.exp2(z - m_) for z, m_ in zip(s, m)]
        den = [jnp.sum(z, axis=0, keepdims=True) for z in e]
        if has_sink:
            den = [z + jnp.exp2(k_ - m_) for z, k_, m_ in zip(den, sk, m)]
        ot = [_dot(v_, z.astype(BF16)) / d_ for v_, z, d_ in zip(vt, e, den)]
        for j in range(0, len(jobs), 2):
            h, hq = jobs[j]
            o_ref[0, hq * qw:(hq + 1) * qw, (h // 2) * PAIR_W:(h // 2 + 1) * PAIR_W] = (
                jnp.concatenate([ot[j], ot[j + 1]], axis=0).T.astype(o_ref.dtype))

    k_next[0:hist, :] = k_win[tq:nk, :]
    vt_next[:, 0:hist] = vt_win[:, tq:nk]
    if group > 1:
        k_swap_next[0:hist, :] = k_swap[tq:nk, :]


def attention_prompt(p, bias, sink, *, n_left, n_heads, group, kcol, vcol):
    b, t, _ = p.shape
    kvw = (n_heads // group) * HEAD_DIM
    qw = n_heads * HEAD_DIM
    has_sink = sink is not None
    assert group == 1 or kvw == PAIR_W, "grouped-query path assumes two kv heads sharing one lane tile"
    tq = min(ATTN_TQ, t)
    nk = n_left * CHUNK + tq
    bias_full = jnp.concatenate(
        [jnp.pad(bias, ((0, 0), (qc * CHUNK, nk - qc * CHUNK - bias.shape[1]), (0, 0)), constant_values=NEG_INF)
         for qc in range(tq // CHUNK)], axis=2)
    hist = n_left * CHUNK
    n_var = -(-hist // tq) + 1
    row = jnp.arange(nk)[None, :, None]
    bias_var = jnp.stack([jnp.where(row >= hist - v * tq, bias_full, NEG_INF) for v in range(n_var - 1)] + [bias_full])
    sink_ops = [sink.reshape(n_heads, 1)] if has_sink else []
    sink_specs = [pl.BlockSpec((n_heads, 1), lambda bi, i: (0, 0))] if has_sink else []
    return pl.pallas_call(
        functools.partial(_attn_prompt_kernel, n_left=n_left, tq=tq, n_heads=n_heads, group=group, has_sink=has_sink),
        grid=(b, t // tq),
        in_specs=[pl.BlockSpec((1, tq, qw), lambda bi, i: (bi, i, 0)),
                  pl.BlockSpec((1, tq, kvw), lambda bi, i: (bi, i, kcol)),
                  pl.BlockSpec((1, tq, kvw), lambda bi, i: (bi, i, vcol)),
                  pl.BlockSpec((None,) + bias_full.shape, lambda bi, i: (jnp.minimum(i, n_var - 1), 0, 0, 0))]
        + sink_specs,
        out_specs=pl.BlockSpec((1, tq, qw), lambda bi, i: (bi, i, 0)),
        out_shape=jax.ShapeDtypeStruct((b, t, qw), BF16),
        scratch_shapes=([pltpu.VMEM((2, nk, kvw), BF16), pltpu.VMEM((2, kvw, nk), BF16)]
                        + ([pltpu.VMEM((2, nk, kvw), BF16)] if group > 1 else [])),
        compiler_params=_cparams(("parallel", "arbitrary")),
        name="attention_prompt",
    )(p, p, p, bias_var, *sink_ops)


def _attn_sample_kernel(*refs, n_heads, group, has_sink, bb):
    q_ref, kc_ref, vc_ref, kn_ref, vn_ref, bias_ref = refs[:6]
    sink_ref = refs[6] if has_sink else None
    o_ref = refs[-1]
    hist = kc_ref.shape[-1]
    nt = (((1,), (1,)), ((), ()))
    heads = range(n_heads)
    q_sl = [slice(h * HEAD_DIM, (h + 1) * HEAD_DIM) for h in heads]
    kv_sl = [slice((h // group) * HEAD_DIM, (h // group + 1) * HEAD_DIM) for h in heads]
    for bi in range(bb):
        q = (q_ref[bi] * QK_SCALE).astype(BF16)
        kn = kn_ref[bi].astype(BF16)
        vn = vn_ref[bi].astype(BF16)
        qh = [q[:, sl] for sl in q_sl]
        s = [jnp.concatenate([_dot(qh[h], kc_ref[bi, h // group].astype(BF16)),
                              lax.dot_general(qh[h], kn[:, kv_sl[h]], nt, preferred_element_type=F32)], axis=1)
             + bias_ref[h] for h in heads]
        m = [jnp.max(z, axis=-1, keepdims=True) for z in s]
        if has_sink:
            sk = [sink_ref[h:h + 1, :] for h in heads]
            m = [jnp.maximum(z, k_) for z, k_ in zip(m, sk)]
        e = [jnp.exp2(z - m_) for z, m_ in zip(s, m)]
        den = [jnp.sum(z, axis=-1, keepdims=True) for z in e]
        if has_sink:
            den = [z + jnp.exp2(k_ - m_) for z, k_, m_ in zip(den, sk, m)]
        for h in heads:
            eb = e[h].astype(BF16)
            o = (lax.dot_general(eb[:, 0:hist], vc_ref[bi, h // group].astype(BF16), nt, preferred_element_type=F32)
                 + _dot(eb[:, hist:], vn[:, kv_sl[h]]))
            o_ref[bi, :, q_sl[h]] = (o / den[h]).astype(o_ref.dtype)


def attention_sample(p, k_cache, v_cache, layer, bias, sink, *, n_heads, group, kcol, vcol):
    b, t, _ = p.shape
    qw = n_heads * HEAD_DIM
    kv_heads = n_heads // group
    kvw = kv_heads * HEAD_DIM
    hist = k_cache.shape[2]
    has_sink = sink is not None
    bb = _batch_block(b, ATTN_SAMPLE_BB)
    transposed = lambda z: jnp.transpose(z, (0, 1, 3, 4, 2))
    cache_spec = pl.BlockSpec((None, bb, kv_heads, HEAD_DIM, hist), lambda i: (layer, i, 0, 0, 0))
    sink_ops = [sink.reshape(n_heads, 1)] if has_sink else []
    sink_specs = [pl.BlockSpec((n_heads, 1), lambda i: (0, 0))] if has_sink else []
    return pl.pallas_call(
        functools.partial(_attn_sample_kernel, n_heads=n_heads, group=group, has_sink=has_sink, bb=bb),
        grid=(b // bb,),
        in_specs=[pl.BlockSpec((bb, t, qw), lambda i: (i, 0, 0)), cache_spec, cache_spec,
                  pl.BlockSpec((bb, t, kvw), lambda i: (i, 0, kcol)),
                  pl.BlockSpec((bb, t, kvw), lambda i: (i, 0, vcol)),
                  pl.BlockSpec(bias.shape, lambda i: (0, 0, 0))] + sink_specs,
        out_specs=pl.BlockSpec((bb, t, qw), lambda i: (i, 0, 0)),
        out_shape=jax.ShapeDtypeStruct((b, t, qw), BF16),
        compiler_params=_cparams(("parallel",)),
        name="attention_sample",
    )(p, transposed(k_cache), transposed(v_cache), p, p, bias, *sink_ops)


def _conv_kernel(*refs, tq, from_state):
    if from_state:
        a_ref, gt_ref, st_in_ref, dw_ref, db_ref, lg_ref, lb_ref, o_ref, st_ref, buf = refs
    else:
        a_ref, gt_ref, pa_ref, pg_ref, dw_ref, db_ref, lg_ref, lb_ref, o_ref, st_ref, buf = refs
    keep = CONV_W - 1
    pad = CONV_HALO - keep
    if from_state:
        buf[0:pad, :] = jnp.zeros((pad, buf.shape[1]), F32)
        buf[pad:CONV_HALO, :] = st_in_ref[0]
    else:
        prev = pa_ref[0] * jax.nn.sigmoid(pg_ref[0])
        buf[0:CONV_HALO, :] = jnp.where(pl.program_id(1) > 0, prev, 0.0)
    buf[CONV_HALO:CONV_HALO + tq, :] = a_ref[0] * jax.nn.sigmoid(gt_ref[0])
    rs = min(tq, CONV_ROWS)
    for r0 in range(0, tq, rs):
        z = jnp.zeros((rs, buf.shape[1]), F32) + db_ref[...]
        base = buf[r0:r0 + rs + CONV_HALO, :]
        for s in range(8):
            rolled = pltpu.roll(base, rs + CONV_HALO - (pad + s), axis=0)
            for j, w in enumerate(range(s, CONV_W, 8)):
                z = z + rolled[8 * j:8 * j + rs, :] * dw_ref[w:w + 1, :]
        mean = jnp.mean(z, axis=-1, keepdims=True)
        zc = z - mean
        var = jnp.mean(zc * zc, axis=-1, keepdims=True)
        zn = zc * lax.rsqrt(var + LN_EPS) * lg_ref[...] + lb_ref[...]
        o_ref[0, r0:r0 + rs, :] = (zn * jax.nn.sigmoid(zn)).astype(o_ref.dtype)
    st_ref[0] = buf[tq + pad:tq + CONV_HALO, :]


def conv_module(p, state, dw_w, dw_b, ln_g, ln_b, *, acol, gcol, tq):
    b, t, _ = p.shape
    cd = dw_w.shape[1]
    keep = CONV_W - 1
    from_state = state is not None
    operands = [p, p]
    in_specs = [pl.BlockSpec((1, tq, cd), lambda bi, i: (bi, i, acol)),
                pl.BlockSpec((1, tq, cd), lambda bi, i: (bi, i, gcol))]
    if from_state:
        operands.append(state)
        in_specs.append(pl.BlockSpec((1, keep, cd), lambda bi, i: (bi, 0, 0)))
    else:
        r = tq // CONV_HALO
        for col in (acol, gcol):
            operands.append(p)
            in_specs.append(pl.BlockSpec((1, CONV_HALO, cd),
                                         lambda bi, i, col=col: (bi, jnp.maximum(i * r - 1, 0), col)))
    operands += [dw_w, dw_b.reshape(1, cd), ln_g.reshape(1, cd), ln_b.reshape(1, cd)]
    in_specs += [pl.BlockSpec((CONV_W, cd), lambda bi, i: (0, 0))] + [pl.BlockSpec((1, cd), lambda bi, i: (0, 0))] * 3
    return pl.pallas_call(
        functools.partial(_conv_kernel, tq=tq, from_state=from_state),
        grid=(b, t // tq),
        in_specs=in_specs,
        out_specs=[pl.BlockSpec((1, tq, cd), lambda bi, i: (bi, i, 0)),
                   pl.BlockSpec((1, keep, cd), lambda bi, i: (bi, 0, 0))],
        out_shape=[jax.ShapeDtypeStruct((b, t, cd), BF16), jax.ShapeDtypeStruct((b, keep, cd), F32)],
        scratch_shapes=[pltpu.VMEM((CONV_HALO + tq, cd), F32)],
        compiler_params=_cparams(("parallel", "arbitrary")),
        name="conv_module",
    )(*operands)


PV_MU_R, PV_MU_K, PV_MU_V, PV_W0, PV_A0, PV_KK, PV_KA, PV_V0, PV_LNW, PV_LNB, PV_RK = range(11)
PV_ROWS = 16
RWKV_CHUNKS_PER_STEP = 4
RWKV_UNITS_PER_STEP = 32


def _shift_rows(x, first_row):
    rolled = pltpu.roll(x, 1, axis=0)
    row = lax.broadcasted_iota(jnp.int32, x.shape, 0)
    return jnp.where(row == 0, first_row, rolled)


def _rwkv_kernel(*refs, n_heads, has_vres, bb, n_cc):
    (xr_ref, xk_ref, xv_ref, xl_ref, shr_ref, shk_ref, shv_ref, shl_ref, s0_ref, pv_ref, mul_ref,
     w2_ref, a2_ref, g2_ref) = refs[:14]
    pos = 14
    if has_vres:
        v1_ref, v2_ref, vf_ref = refs[pos:pos + 3]
        pos += 3
    ob_ref, sout_ref = refs[pos:pos + 2]
    pos += 2
    if not has_vres:
        vf_out_ref = refs[pos]
        pos += 1
    s_s, lr_s, lk_s, lv_s, ll_s = refs[pos:pos + 5]
    L = CHUNK
    c = pl.program_id(1)

    n_pairs = n_heads // 2
    PW = 2 * HEAD_DIM

    @pl.when(c == 0)
    def _():
        zero = jnp.zeros((HEAD_DIM, HEAD_DIM), F32)
        for bi in range(bb):
            for p in range(n_pairs):
                top = jnp.concatenate([s0_ref[bi, 2 * p], zero], axis=1)
                bottom = jnp.concatenate([zero, s0_ref[bi, 2 * p + 1]], axis=1)
                s_s[bi * n_pairs + p] = jnp.concatenate([top, bottom], axis=0)
            lr_s[bi, 0:1, :] = shr_ref[bi]
            lk_s[bi, 0:1, :] = shk_ref[bi]
            lv_s[bi, 0:1, :] = shv_ref[bi]
            ll_s[bi, 0:1, :] = shl_ref[bi]

    tl = n_cc * L

    def token_shift(x_ref, last_s, mu, bi):
        x = x_ref[bi]
        prev = _shift_rows(x, last_s[bi, 0:1, :])
        last_s[bi, 0:1, :] = x[tl - 1:tl, :]
        return x + (prev - x) * mu

    pv = lambda i: pv_ref[i:i + 1, :]
    row = lax.broadcasted_iota(jnp.int32, (L, L), 0)
    col = lax.broadcasted_iota(jnp.int32, (L, L), 1)
    tri_incl = (row >= col).astype(BF16)

    full = {}
    for bi in range(bb):
        r = token_shift(xr_ref, lr_s, pv(PV_MU_R), bi)
        k = token_shift(xk_ref, lk_s, pv(PV_MU_K), bi)
        v = token_shift(xv_ref, lv_s, pv(PV_MU_V), bi)
        lo = token_shift(xl_ref, ll_s, mul_ref[...], bi)

        zw = pv(PV_W0) + _dot(jnp.tanh(lo).astype(BF16), w2_ref[...])
        w = -(jnp.maximum(-zw, 0.0) + jnp.log(1.0 + jnp.exp(-jnp.abs(zw)))) - 0.5
        d = -jnp.exp(w)
        iclr = jax.nn.sigmoid(pv(PV_A0) + _dot(lo.astype(BF16), a2_ref[...]))
        g = _dot(jax.nn.sigmoid(lo).astype(BF16), g2_ref[...])
        if has_vres:
            mix = jax.nn.sigmoid(pv(PV_V0) + _dot(_dot(v.astype(BF16), v1_ref[...]).astype(BF16), v2_ref[...]))
            v = v + (vf_ref[bi] - v) * mix
        else:
            vf_out_ref[bi] = v
        kks = k * pv(PV_KK)
        kh_all = k * (1.0 + (iclr - 1.0) * pv(PV_KA))
        for cc in range(n_cc):
            rows = slice(cc * L, (cc + 1) * L)
            dc = d[rows, :]
            cs = _cumsum_rows(tri_incl, dc)
            cprev = cs - dc
            c_last = cs[L - 1:L, :]
            c_mid = cs[L // 2:L // 2 + 1, :]
            full[bi, cc] = dict(
                r=r[rows, :], v=v[rows, :], g=g[rows, :], iclr=iclr[rows, :], kks=kks[rows, :], kh=kh_all[rows, :],
                e_mid=jnp.exp(c_mid),
                e_mid_prev=jnp.exp(cprev - c_mid), e_mid_incl=jnp.exp(cs - c_mid), e_from_mid=jnp.exp(c_mid - cs),
                e_to_end=jnp.exp(c_last - cs),
                e_chunk=jnp.exp(c_last))

    lane_lo = lax.broadcasted_iota(jnp.int32, (1, PW), 1) < HEAD_DIM
    r2 = lax.broadcasted_iota(jnp.int32, (2 * L, 2 * L), 0)
    c2 = lax.broadcasted_iota(jnp.int32, (2 * L, 2 * L), 1)
    same_head = (r2 >= L) == (c2 >= L)
    strict_bd = same_head & (jnp.bitwise_and(r2, L - 1) > jnp.bitwise_and(c2, L - 1))
    eye_bd = (r2 == c2).astype(F32)
    incl_c = (lax.broadcasted_iota(jnp.int32, (L, 2 * L), 0)
              >= jnp.bitwise_and(lax.broadcasted_iota(jnp.int32, (L, 2 * L), 1), L - 1))

    def bd(x):
        return jnp.concatenate([jnp.where(lane_lo, x, 0.0), jnp.where(lane_lo, 0.0, x)], axis=0)

    def tile2(x):
        return jnp.concatenate([x, x], axis=0)

    def head_sum(x):
        lo_sum = jnp.sum(jnp.where(lane_lo, x, 0.0), axis=-1, keepdims=True)
        hi_sum = jnp.sum(jnp.where(lane_lo, 0.0, x), axis=-1, keepdims=True)
        return jnp.where(lane_lo, lo_sum, hi_sum)

    bf = lambda z: z.astype(BF16)
    dot_nt = lambda p_, q_: lax.dot_general(p_, q_, (((1,), (1,)), ((), ())), preferred_element_type=F32)
    dot_tn = lambda p_, q_: lax.dot_general(p_, q_, (((0,), (0,)), ((), ())), preferred_element_type=F32)

    units = [(bi, cc, p) for cc in range(n_cc) for bi in range(bb) for p in range(n_pairs)]
    sls = [slice(p * PW, (p + 1) * PW) for _, _, p in units]
    get = lambda name: [full[bi, cc][name][:, sl] for (bi, cc, _), sl in zip(units, sls)]
    mul = lambda xs_, ys_: [x_ * y_ for x_, y_ in zip(xs_, ys_)]

    kk = get("kks")
    kk = [z / jnp.maximum(jnp.sqrt(head_sum(z * z)), 1e-12) for z in kk]
    a_p = [-z for z in kk]
    b_p = mul(kk, get("iclr"))
    k_p, r_p, v_p = get("kh"), get("r"), get("v")
    at = mul(a_p, get("e_mid_prev"))
    rt = mul(r_p, get("e_mid_incl"))
    e_from_mid, e_to_end = get("e_from_mid"), get("e_to_end")
    bt, kt = mul(b_p, e_from_mid), mul(k_p, e_from_mid)
    bh, kh = mul(b_p, e_to_end), mul(k_p, e_to_end)

    cat0 = lambda x_, y_: jnp.concatenate([x_, y_], axis=0)
    cat1 = lambda x_, y_: jnp.concatenate([x_, y_], axis=1)
    at_bd = [bf(bd(z)) for z in at]
    g_a = [dot_nt(x_, bf(cat0(tile2(b_), tile2(k_)))) for x_, b_, k_ in zip(at_bd, bt, kt)]
    n_ab = [jnp.where(strict_bd, z[:, 0:PW], 0.0) for z in g_a]
    n_ak = [jnp.where(strict_bd, z[:, PW:2 * PW], 0.0) for z in g_a]
    rt_b = [bf(z) for z in rt]
    n_r = [jnp.where(cat1(incl_c, incl_c), dot_nt(x_, bf(cat0(bd(b_), bd(k_)))), 0.0)
           for x_, b_, k_ in zip(rt_b, bt, kt)]
    y_inv = [eye_bd + z for z in n_ab]
    pw = [_dot(bf(z), bf(z)) for z in n_ab]
    for _ in range(int(math.log2(L)) - 2):
        lvl = [_dot(bf(p_), bf(cat1(y_, p_))) for p_, y_ in zip(pw, y_inv)]
        y_inv = [y_ + z[:, 0:PW] for y_, z in zip(y_inv, lvl)]
        pw = [z[:, PW:2 * PW] for z in lvl]
    y_inv = [y_ + _dot(bf(p_), bf(y_)) for p_, y_ in zip(pw, y_inv)]
    v_bd = [bf(bd(z)) for z in v_p]
    t2 = [_dot(bf(x_), y_) for x_, y_ in zip(n_ak, v_bd)]
    w12 = [_dot(bf(x_), cat1(a_, bf(t_))) for x_, a_, t_ in zip(y_inv, at_bd, t2)]
    e_mid, e_chunk, g_p = get("e_mid"), get("e_chunk"), get("g")
    n_seq = bb * n_pairs
    state = [s_s[i] for i in range(n_seq)]
    y = []
    for cc in range(n_cc):
        ids = range(cc * n_seq, (cc + 1) * n_seq)
        s_b = [bf(state[q] * e_mid[i]) for q, i in enumerate(ids)]
        u = [dot_nt(bf(w12[i][:, 0:PW]), s_) + w12[i][:, PW:2 * PW] for i, s_ in zip(ids, s_b)]
        y += [dot_nt(rt_b[i], s_) + _dot(bf(n_r[i]), cat0(bf(u_), v_bd[i])) for i, s_, u_ in zip(ids, s_b, u)]
        upd = [dot_tn(bf(cat0(u_[0:L] + u_[L:2 * L], v_p[i])), bf(cat0(bh[i], kh[i]))) for i, u_ in zip(ids, u)]
        state = [state[q] * e_chunk[i] + jnp.where(same_head, d_, 0.0) for q, (i, d_) in enumerate(zip(ids, upd))]
    for q in range(n_seq):
        s_s[q] = state[q]
    for i, ((bi, cc, _), sl) in enumerate(zip(units, sls)):
        mean = head_sum(y[i]) * (1.0 / HEAD_DIM)
        yc = y[i] - mean
        var = head_sum(yc * yc) * (1.0 / HEAD_DIM)
        yn = yc * lax.rsqrt(var + LNX_EPS) * pv(PV_LNW)[:, sl] + pv(PV_LNB)[:, sl]
        bonus = head_sum(r_p[i] * k_p[i] * pv(PV_RK)[:, sl])
        ob_ref[bi, cc * L:(cc + 1) * L, sl] = ((yn + bonus * v_p[i]) * g_p[i]).astype(ob_ref.dtype)

    @pl.when(c == pl.num_programs(1) - 1)
    def _():
        for q in range(n_seq):
            bi, p = q // n_pairs, q % n_pairs
            s_p = s_s[q]
            sout_ref[bi, 2 * p] = s_p[0:HEAD_DIM, 0:HEAD_DIM]
            sout_ref[bi, 2 * p + 1] = s_p[HEAD_DIM:PW, HEAD_DIM:PW]


def rwkv_mixer(p, shift0, s0, layer, pvec, mu_l, w2p, a2p, g2p, vres, *, cols):
    b, t, _ = p.shape
    nc = t // CHUNK
    n_heads = s0.shape[2]
    cb = n_heads * HEAD_DIM
    rc, kc, vc, lc = cols
    has_vres = vres is not None
    n_cc = _batch_block(nc, RWKV_CHUNKS_PER_STEP)
    bb = _batch_block(b, RWKV_UNITS_PER_STEP // (n_cc * (n_heads // 2)))
    tok = lambda col, w: pl.BlockSpec((bb, n_cc * CHUNK, w), lambda bi, c, col=col: (bi, c, col))
    const2 = lambda shape: pl.BlockSpec(shape, lambda bi, c: (0, 0))
    perb = lambda w: pl.BlockSpec((bb, 1, w), lambda bi, c: (bi, 0, 0))
    state_spec = pl.BlockSpec((bb, n_heads, HEAD_DIM, HEAD_DIM), lambda bi, c: (bi, 0, 0, 0))
    state_in_spec = pl.BlockSpec((None, bb, n_heads, HEAD_DIM, HEAD_DIM), lambda bi, c: (layer, bi, 0, 0, 0))
    operands = [p, p, p, p, *shift0, s0, pvec, mu_l, w2p, a2p, g2p]
    in_specs = [tok(rc, cb), tok(kc, cb), tok(vc, cb), tok(lc, LORA_W),
                perb(cb), perb(cb), perb(cb), perb(LORA_W), state_in_spec,
                const2((PV_ROWS, cb)), const2((1, LORA_W)),
                const2((LORA_W, cb)), const2((LORA_W, cb)), const2((LORA_W, cb))]
    out_specs = [tok(0, cb), state_spec]
    out_shape = [jax.ShapeDtypeStruct((b, t, cb), BF16), jax.ShapeDtypeStruct(s0.shape[1:], F32)]
    if has_vres:
        v1p, v2p, v_first = vres
        operands += [v1p, v2p, v_first]
        in_specs += [const2(v1p.shape), const2(v2p.shape), tok(0, cb)]
    else:
        out_specs.append(tok(0, cb))
        out_shape.append(jax.ShapeDtypeStruct((b, t, cb), F32))
    outs = pl.pallas_call(
        functools.partial(_rwkv_kernel, n_heads=n_heads, has_vres=has_vres, bb=bb, n_cc=n_cc),
        grid=(b // bb, nc // n_cc),
        in_specs=in_specs,
        out_specs=out_specs,
        out_shape=out_shape,
        scratch_shapes=[pltpu.VMEM((bb * n_heads // 2, 2 * HEAD_DIM, 2 * HEAD_DIM), F32),
                        pltpu.VMEM((bb, 8, cb), F32), pltpu.VMEM((bb, 8, cb), F32), pltpu.VMEM((bb, 8, cb), F32),
                        pltpu.VMEM((bb, 8, LORA_W), F32)],
        compiler_params=_cparams(("parallel", "arbitrary")),
        name="rwkv_mixer",
    )(*operands)
    return (outs[0], outs[1], None) if has_vres else tuple(outs)


def _band_offsets(n_left):
    return np.arange(-(CHUNK - 1), (n_left + 1) * CHUNK) - n_left * CHUNK


def _lookup_static(table, idx):
    h = table.shape[0]
    pieces = []
    i, n = 0, len(idx)
    while i < n:
        j = i + 1
        step = int(idx[j] - idx[i]) if j < n else 0
        if step in (-1, 0, 1):
            while j < n and idx[j] - idx[j - 1] == step:
                j += 1
        first, last = int(idx[i]), int(idx[j - 1])
        if first == last:
            pieces.append(jnp.broadcast_to(table[:, first:first + 1], (h, j - i)))
        elif first < last:
            pieces.append(table[:, first:last + 1])
        else:
            pieces.append(jnp.flip(table[:, last:first + 1], axis=1))
        i = j
    return jnp.concatenate(pieces, axis=1)


def _toeplitz(e, n_rows, n_cols):
    h, n = e.shape
    z = jnp.concatenate([e, jnp.zeros((h, 1), e.dtype)], axis=1)
    shifted = jnp.tile(z, (1, n_rows))[:, :n_rows * n].reshape(h, n_rows, n)
    return shifted[:, :, n_rows - 1:n_rows - 1 + n_cols]


def _t5_bucket(rel):
    nb = T5_BUCKETS // 2
    exact = nb // 2
    n = np.abs(rel)
    nf = np.maximum(n, exact).astype(np.float32)
    large = exact + (np.log(nf / exact) / math.log(T5_MAX_DIST / exact) * (nb - exact)).astype(np.int32)
    return np.where(rel > 0, nb, 0) + np.where(n < exact, n, np.minimum(large, nb - 1))


def _band_bias_t(e, n_left):
    return jnp.swapaxes(_toeplitz(e, CHUNK, (n_left + 1) * CHUNK), 1, 2)


def _bias_a(t5_table):
    return _band_bias_t(_lookup_static(t5_table.T.astype(F32), _t5_bucket(_band_offsets(A_LEFT))), A_LEFT)


def _bias_c(rel_table):
    idx = np.clip(-_band_offsets(C_LEFT), -REL_CLIP, REL_CLIP) + REL_CLIP
    return _band_bias_t(_lookup_static(rel_table.astype(F32), idx), C_LEFT)


def _pad_rows(w, start, total):
    return jnp.zeros((total, w.shape[1]), w.dtype).at[start:start + w.shape[0]].set(w)


def kernel(x_prompt, x_sample, cache_a_k, cache_a_v, state_b_wkv, state_b_shift, cache_c_k, cache_c_v, state_d_conv, norm_mix_g, norm_ffn_g, norm_final_g, t5_table, w_in_e, w_out_e, a_sink, b_mu, b_w0, b_w2, b_a0, b_a2, b_g2, b_kk, b_ka, b_rk, b_lnx_w, b_lnx_b, b_v0, b_v1, b_v2, w_in_o, w_out_o, c_rel_table, d_dw_w, d_dw_b, d_ln_g, d_ln_b, ffn_w_gate, ffn_w_up, ffn_w_down):
    depth, d_model = norm_mix_g.shape
    h_a = a_sink.shape[1]
    g_a = h_a // KVH_A
    qa = h_a * HEAD_DIM
    kva = KVH_A * HEAD_DIM
    h_b = state_b_wkv.shape[2]
    cb = h_b * HEAD_DIM
    h_c = c_rel_table.shape[1]
    qc = h_c * HEAD_DIM
    cd = d_dw_w.shape[2]
    d_ff = ffn_w_gate.shape[2]
    tf = d_ff

    groups = [x_prompt, x_sample]
    dims = [x.shape[:2] for x in groups]
    xs = [x.reshape(-1, d_model) for x in groups]
    tms = [min(TOKEN_ROWS, x.shape[0]) for x in xs]
    wg, wu, wd = ffn_w_gate.astype(BF16), ffn_w_up.astype(BF16), ffn_w_down.astype(BF16)
    conv_tq = [min(CONV_TQ, t) for _, t in dims]

    o_q, o_k, o_v, o_pb = 0, qa, qa + kva, qa + 2 * kva
    o_r, o_wd, o_kb, o_vb = o_pb, o_pb + cb, o_pb + cb + DECAY_LORA, o_pb + 2 * cb + DECAY_LORA
    o_ad = o_vb + cb
    o_gd = o_ad + ICLR_LORA
    perm_ranges = [(o_q, qa), (o_r, cb), (o_kb, cb), (o_vb, cb), (o_k, kva), (o_v, kva),
                   (o_wd, DECAY_LORA), (o_ad, ICLR_LORA), (o_gd, GATE_LORA)]
    n_r, n_kb, n_vb = qa, qa + cb, qa + 2 * cb
    n_ka = qa + 3 * cb
    n_va = n_ka + kva
    n_lo = n_va + kva
    pb_pieces = [[(o_r, cb)], [(o_kb, cb)], [(o_vb, cb)], [(o_wd, DECAY_LORA), (o_ad, ICLR_LORA), (o_gd, GATE_LORA)]]

    def split_pb(z):
        return [jnp.concatenate([z[..., a - o_pb:a - o_pb + n] for a, n in piece], axis=-1) for piece in pb_pieces]

    def join_pb(r_, k_, v_, lo_):
        return jnp.concatenate([r_, lo_[..., :DECAY_LORA], k_, v_, lo_[..., DECAY_LORA:]], axis=-1)

    bias_a = _bias_a(t5_table) * LOG2E
    st = [[[] for _ in range(7)] for _ in range(2)]
    v_first = [None, None]
    for i in range(depth):
        j = i // 2
        last = i == depth - 1
        if i % 2 == 0:
            w_in = jnp.concatenate([w_in_e[j][:, a:a + n] for a, n in perm_ranges], axis=1).astype(BF16)
            wo = w_out_e[j].astype(BF16)
            mu_r, mu_k, mu_v, mu_lo = split_pb(b_mu[j])
            rows = {PV_MU_R: mu_r, PV_MU_K: mu_k, PV_MU_V: mu_v, PV_W0: b_w0[j],
                    PV_A0: b_a0[j], PV_KK: b_kk[j], PV_KA: b_ka[j], PV_LNW: b_lnx_w[j], PV_LNB: b_lnx_b[j],
                    PV_RK: b_rk[j].reshape(cb)}
            if j > 0:
                rows[PV_V0] = b_v0[j - 1]
            pvec = jnp.stack([rows.get(ri, jnp.zeros((cb,), F32)) for ri in range(PV_ROWS)])
            mu_l = mu_lo.reshape(1, LORA_W)
            w2p = _pad_rows(b_w2[j], 0, LORA_W).astype(BF16)
            a2p = _pad_rows(b_a2[j], DECAY_LORA, LORA_W).astype(BF16)
            g2p = _pad_rows(b_g2[j], DECAY_LORA + ICLR_LORA, LORA_W).astype(BF16)
            if j > 0:
                lora_v = b_v1.shape[2]
                v1p = jnp.zeros((cb, 128), F32).at[:, :lora_v].set(b_v1[j - 1]).astype(BF16)
                v2p = _pad_rows(b_v2[j - 1], 0, 128).astype(BF16)
            bias = bias_a
        else:
            w_in = w_in_o[j].astype(BF16)
            wo = w_out_o[j].astype(BF16)
            bias = _bias_c(c_rel_table[j]) * LOG2E
        for gi in range(2):
            b, t = dims[gi]
            p = proj_in(xs[gi], norm_mix_g[i], w_in, tms[gi]).reshape(b, t, -1)
            if i % 2 == 0:
                attn_args = dict(n_heads=h_a, group=g_a, kcol=n_ka // kva, vcol=n_va // kva)
                if gi == 0:
                    shift0 = [jnp.zeros((b, 1, w_), F32) for w_ in (cb, cb, cb, LORA_W)]
                    s0, s0_layer = jnp.zeros((1, b, h_b, HEAD_DIM, HEAD_DIM), F32), 0
                    o1 = attention_prompt(p, bias, a_sink[j] * LOG2E, n_left=A_LEFT, **attn_args)
                else:
                    shift0 = [z[:, None, :] for z in split_pb(state_b_shift[j])]
                    s0, s0_layer = state_b_wkv, j
                    o1 = attention_sample(p, cache_a_k, cache_a_v, j, jnp.swapaxes(bias, 1, 2), a_sink[j] * LOG2E,
                                          **attn_args)
                vres = None if j == 0 else (v1p, v2p, v_first[gi])
                o2, wkv, vf = rwkv_mixer(p, shift0, s0, s0_layer, pvec, mu_l, w2p, a2p, g2p, vres,
                                         cols=(n_r // cb, n_kb // cb, n_vb // cb, n_lo // LORA_W))
                if j == 0:
                    v_first[gi] = vf
                keep = min(A_LEFT * CHUNK, t)
                nk = p[:, t - keep:, n_ka:n_ka + kva].reshape(b, keep, KVH_A, HEAD_DIM)
                nv = p[:, t - keep:, n_va:n_va + kva].reshape(b, keep, KVH_A, HEAD_DIM)
                nshift = join_pb(*[p[:, t - 1, c0:c0 + w_] for c0, w_ in ((n_r, cb), (n_kb, cb), (n_vb, cb), (n_lo, LORA_W))])
                for s_list, val in zip(st[gi][0:4], (nk, nv, wkv, nshift)):
                    s_list.append(val)
            else:
                attn_args = dict(n_heads=h_c, group=1, kcol=1, vcol=2)
                if gi == 0:
                    conv_prev = None
                    o1 = attention_prompt(p, bias, None, n_left=C_LEFT, **attn_args)
                else:
                    conv_prev = state_d_conv[j]
                    o1 = attention_sample(p, cache_c_k, cache_c_v, j, jnp.swapaxes(bias, 1, 2), None, **attn_args)
                o2, nconv = conv_module(p, conv_prev, d_dw_w[j], d_dw_b[j], d_ln_g[j], d_ln_b[j],
                                        acol=3 * qc // cd, gcol=3 * qc // cd + 1, tq=conv_tq[gi])
                keep = min(C_LEFT * CHUNK, t)
                nk = p[:, t - keep:, qc:2 * qc].reshape(b, keep, h_c, HEAD_DIM)
                nv = p[:, t - keep:, 2 * qc:3 * qc].reshape(b, keep, h_c, HEAD_DIM)
                for s_list, val in zip(st[gi][4:7], (nk, nv, nconv)):
                    s_list.append(val)
            half = o1.shape[-1]
            xs[gi] = post(xs[gi], o1.reshape(-1, half), o2.reshape(-1, half), wo, norm_ffn_g[i], wg, wu, wd,
                          norm_final_g, i, tm=tms[gi], tf=tf, final_norm=last)
    y_prompt = xs[0].reshape(x_prompt.shape)
    y_sample = xs[1].reshape(x_sample.shape)
    (pak, pav, pbw, pbs, pck, pcv, pdc), (sak, sav, sbw, sbs, sck, scv, sdc) = [[jnp.stack(s) for s in g] for g in st]
    return (y_prompt, y_sample, pak, pav, pbw, pbs, pck, pcv, pdc, sak, sav, sbw, sbs, sck, scv, sdc)
```
